```python
import math
import jax
import jax.numpy as jnp
from jax import lax
import numpy as np

D_MODEL = 1024
BATCH = 32
SEQ = 256
DEPTH = 2
DEC_BATCH = 8
DEC_SEQ = 2048
PAST_LEN = 256

GRID_W = 64
N_MIXERS = 2
N_SSD_LAYERS = (DEPTH + 1) // 2
N_GDN_LAYERS = DEPTH // 2

SSD_D_INNER = 2 * D_MODEL
SSD_HEADDIM = 64
SSD_HEADS = SSD_D_INNER // SSD_HEADDIM
SSD_GROUPS = 8
SSD_STATE = 128
SSD_CONV = 3
SSD_CHUNK = 128
SSD_XBC = SSD_D_INNER + 2 * SSD_GROUPS * SSD_STATE
SSD_IN = SSD_D_INNER + SSD_XBC + 2 * SSD_HEADS

GDN_K_HEADS = 8
GDN_V_HEADS = 16
GDN_HEAD_K = 128
GDN_HEAD_V = 128
GDN_QK = GDN_K_HEADS * GDN_HEAD_K
GDN_V = GDN_V_HEADS * GDN_HEAD_V
GDN_QKV = 2 * GDN_QK + GDN_V
GDN_CONV = 3
GDN_CHUNK = 64
GDN_IN = GDN_QKV + GDN_V + 4 * GDN_V_HEADS

D_FF = 2816
FFN_CONV = 3

DEEPNORM_ALPHA = (2.0 * DEPTH) ** 0.25
DEEPNORM_BETA = (8.0 * DEPTH) ** -0.25
LN_EPS = 1e-5
RMS_EPS = 1e-6

kernel_name = 'bidir_ssd_gdn_prefix_dit_step'


def layer_norm(x, g, b):
    xf = x.astype(jnp.float32)
    mu = jnp.mean(xf, axis=-1, keepdims=True)
    var = jnp.mean(jnp.square(xf - mu), axis=-1, keepdims=True)
    return ((xf - mu) * lax.rsqrt(var + LN_EPS)).astype(x.dtype) * g + b


def rms_norm(x, w):
    xf = x.astype(jnp.float32)
    ms = jnp.mean(jnp.square(xf), axis=-1, keepdims=True)
    return (xf * lax.rsqrt(ms + RMS_EPS)).astype(x.dtype) * w


def l2_normalize(x):
    xf = x.astype(jnp.float32)
    return (xf * lax.rsqrt(jnp.sum(jnp.square(xf), axis=-1, keepdims=True) + RMS_EPS)).astype(x.dtype)


def adaln_params(cond, w_ada, b_ada):
    ada = (jax.nn.silu(cond) @ w_ada + b_ada)[:, None, :]
    return tuple(jnp.split(ada, 6, axis=-1))


def modulate(x, shift, scale):
    return x * (1.0 + scale) + shift


def dwconv1d_centred(x, w):
    width = w.shape[0]
    pad = width // 2
    seqlen = x.shape[1]
    xp = jnp.pad(x, ((0, 0), (pad, pad), (0, 0)))
    return sum(w[t] * xp[:, t:t + seqlen] for t in range(width))


def dwconv2d_grid(x, w, rows):
    bsz, seqlen, ch = x.shape
    img = x.reshape(bsz, rows, GRID_W, ch)
    out = lax.conv_general_dilated(img, w[:, :, None, :], window_strides=(1, 1), padding='SAME',
                                   dimension_numbers=('NHWC', 'HWIO', 'NHWC'), feature_group_count=ch)
    return out.reshape(bsz, seqlen, ch)


def flip_seq(t):
    return jnp.flip(t, axis=1)


def ssd_scan(x, dt, a_neg, b_mat, c_mat, s0):
    bsz, seqlen, nh, hp = x.shape
    ng, ns = b_mat.shape[2], b_mat.shape[3]
    nr = nh // ng
    nc = seqlen // SSD_CHUNK
    xc = x.reshape(bsz, nc, SSD_CHUNK, ng, nr, hp)
    dtc = dt.reshape(bsz, nc, SSD_CHUNK, ng, nr)
    bc = b_mat.reshape(bsz, nc, SSD_CHUNK, ng, ns)
    cc = c_mat.reshape(bsz, nc, SSD_CHUNK, ng, ns)
    acum = jnp.cumsum(dtc * a_neg.reshape(ng, nr), axis=2)
    xdt = xc * dtc[..., None]
    seg = acum[:, :, :, None] - acum[:, :, None, :]
    mask = jnp.tril(jnp.ones((SSD_CHUNK, SSD_CHUNK), dtype=bool))[:, :, None, None]
    decay = jnp.exp(jnp.where(mask, seg, -jnp.inf))
    scores = jnp.einsum('bcign,bcjgn->bcijg', cc, bc)
    y_intra = jnp.einsum('bcijgr,bcjgrp->bcigrp', scores[..., None] * decay, xdt)
    decay_end = jnp.exp(acum[:, :, -1:] - acum)
    states = jnp.einsum('bcqgn,bcqgr,bcqgrp->bcgrpn', bc, decay_end, xdt)
    chunk_decay = jnp.exp(acum[:, :, -1])

    def step(s, inp):
        st, dc = inp
        s_new = dc[..., None, None] * s + st
        return s_new.astype(s.dtype), s

    s_init = s0.reshape(bsz, ng, nr, hp, ns).astype(x.dtype)
    s_final, s_prev = lax.scan(step, s_init, (jnp.moveaxis(states, 1, 0), jnp.moveaxis(chunk_decay, 1, 0)))
    s_prev = jnp.moveaxis(s_prev, 0, 1)
    y_inter = jnp.einsum('bcign,bcgrpn,bcigr->bcigrp', cc, s_prev, jnp.exp(acum))
    y = (y_intra + y_inter).reshape(bsz, seqlen, nh, hp)
    return y, s_final.reshape(bsz, nh, hp, ns)


def gdn_scan(q, k, v, g, beta, s0):
    bsz, seqlen, nh, dk = q.shape
    dv = v.shape[-1]
    nc = seqlen // GDN_CHUNK

    def chunks(t):
        return jnp.moveaxis(t.reshape(bsz, nc, GDN_CHUNK, nh, -1), 3, 2)

    qc, kc, vc = chunks(q), chunks(k), chunks(v)
    gc = jnp.cumsum(jnp.moveaxis(g.reshape(bsz, nc, GDN_CHUNK, nh), 3, 2), axis=-1)
    bc = jnp.moveaxis(beta.reshape(bsz, nc, GDN_CHUNK, nh), 3, 2)[..., None]
    seg = gc[..., :, None] - gc[..., None, :]
    incl = jnp.tril(jnp.ones((GDN_CHUNK, GDN_CHUNK), dtype=bool))
    strict = jnp.tril(jnp.ones((GDN_CHUNK, GDN_CHUNK), dtype=bool), -1)
    kb = kc * bc
    a_mat = jnp.einsum('bchid,bchjd->bchij', kb, kc) * jnp.exp(jnp.where(strict, seg, -jnp.inf))
    sys_mat = (jnp.eye(GDN_CHUNK, dtype=jnp.float32) + a_mat.astype(jnp.float32))
    rhs = jnp.concatenate([vc * bc, kb * jnp.exp(gc)[..., None]], axis=-1).astype(jnp.float32)
    sol = lax.linalg.triangular_solve(sys_mat, rhs, left_side=True, lower=True, unit_diagonal=True).astype(q.dtype)
    u, w = sol[..., :dv], sol[..., dv:]
    qk = jnp.einsum('bchid,bchjd->bchij', qc, kc) * jnp.exp(jnp.where(incl, seg, -jnp.inf))
    q_dec = qc * jnp.exp(gc)[..., None]
    k_dec = kc * jnp.exp(gc[..., -1:] - gc)[..., None]
    g_last = jnp.exp(gc[..., -1])

    def step(s, inp):
        u_c, w_c, qk_c, qd_c, kd_c, gl_c = inp
        v_new = u_c - jnp.einsum('bhid,bhde->bhie', w_c, s)
        o_c = jnp.einsum('bhid,bhde->bhie', qd_c, s) + jnp.einsum('bhij,bhje->bhie', qk_c, v_new)
        s_new = gl_c[..., None, None] * s + jnp.einsum('bhjd,bhje->bhde', kd_c, v_new)
        return s_new.astype(s.dtype), o_c

    xs = tuple(jnp.moveaxis(t, 1, 0) for t in (u, w, qk, q_dec, k_dec, g_last))
    s_final, o = lax.scan(step, s0.astype(q.dtype), xs)
    o = jnp.moveaxis(jnp.moveaxis(o, 0, 1), 2, 3).reshape(bsz, seqlen, nh, dv)
    return o, s_final


def ssd_mixer(h, s0_f, s0_b, w_in, conv_w, conv_b, dt_bias, a_log, d_skip, norm_w, w_out):
    bsz, seqlen, _ = h.shape
    proj = h @ w_in
    z = proj[..., :SSD_D_INNER]
    xbc = jax.nn.silu(dwconv1d_centred(proj[..., SSD_D_INNER:SSD_D_INNER + SSD_XBC], conv_w) + conv_b)
    dt_raw = proj[..., SSD_D_INNER + SSD_XBC:].reshape(bsz, seqlen, 2, SSD_HEADS)
    xs = xbc[..., :SSD_D_INNER].reshape(bsz, seqlen, SSD_HEADS, SSD_HEADDIM)
    gn = SSD_GROUPS * SSD_STATE
    bm = xbc[..., SSD_D_INNER:SSD_D_INNER + gn].reshape(bsz, seqlen, SSD_GROUPS, SSD_STATE)
    cm = xbc[..., SSD_D_INNER + gn:].reshape(bsz, seqlen, SSD_GROUPS, SSD_STATE)
    dt = jax.nn.softplus(dt_raw + dt_bias)
    a_neg = -jnp.exp(a_log)
    y_f, s_f = ssd_scan(xs, dt[:, :, 0], a_neg[0], bm, cm, s0_f)
    y_b, s_b = ssd_scan(flip_seq(xs), flip_seq(dt[:, :, 1]), a_neg[1], flip_seq(bm), flip_seq(cm), s0_b)
    y = y_f + flip_seq(y_b) + xs * d_skip[:, None]
    y = rms_norm(y.reshape(bsz, seqlen, SSD_D_INNER) * jax.nn.silu(z), norm_w)
    return y @ w_out, s_f, s_b


def gdn_mixer(h, s0_f, s0_b, w_in, conv_w, dt_bias, a_log, norm_w, w_out):
    bsz, seqlen, _ = h.shape
    proj = h @ w_in
    qkv = jax.nn.silu(dwconv1d_centred(proj[..., :GDN_QKV], conv_w))
    q = qkv[..., :GDN_QK].reshape(bsz, seqlen, GDN_K_HEADS, GDN_HEAD_K)
    k = qkv[..., GDN_QK:2 * GDN_QK].reshape(bsz, seqlen, GDN_K_HEADS, GDN_HEAD_K)
    v = qkv[..., 2 * GDN_QK:].reshape(bsz, seqlen, GDN_V_HEADS, GDN_HEAD_V)
    z = proj[..., GDN_QKV:GDN_QKV + GDN_V].reshape(bsz, seqlen, GDN_V_HEADS, GDN_HEAD_V)
    a_raw = proj[..., GDN_QKV + GDN_V:GDN_QKV + GDN_V + 2 * GDN_V_HEADS].reshape(bsz, seqlen, 2, GDN_V_HEADS)
    b_raw = proj[..., GDN_QKV + GDN_V + 2 * GDN_V_HEADS:].reshape(bsz, seqlen, 2, GDN_V_HEADS)
    rep = GDN_V_HEADS // GDN_K_HEADS
    q = jnp.repeat(l2_normalize(q) * (GDN_HEAD_K ** -0.5), rep, axis=2)
    k = jnp.repeat(l2_normalize(k), rep, axis=2)
    g = -jnp.exp(a_log) * jax.nn.softplus(a_raw + dt_bias)
    beta = jax.nn.sigmoid(b_raw)
    o_f, s_f = gdn_scan(q, k, v, g[:, :, 0], beta[:, :, 0], s0_f)
    o_b, s_b = gdn_scan(flip_seq(q), flip_seq(k), flip_seq(v), flip_seq(g[:, :, 1]), flip_seq(beta[:, :, 1]), s0_b)
    o = rms_norm(o_f + flip_seq(o_b), norm_w) * jax.nn.silu(z)
    return o.reshape(bsz, seqlen, GDN_V) @ w_out, s_f, s_b


def conv_glu_ffn(h, w_up, conv_w, conv_b, w_down, rows):
    up = h @ w_up
    a, v = up[..., :D_FF], up[..., D_FF:]
    if rows is None:
        a = dwconv1d_centred(a, conv_w[FFN_CONV // 2])
    else:
        a = dwconv2d_grid(a, conv_w, rows)
    return (jax.nn.silu(a + conv_b) * v) @ w_down


def _dt_bias_init(k, shape):
    dt = jnp.exp(jax.random.uniform(k, shape, jnp.float32, math.log(1e-3), math.log(1e-1)))
    return dt + jnp.log(-jnp.expm1(-dt))


def setup_inputs(seed: int = 0) -> dict:
    key = jax.random.key(seed)
    ks = jax.random.split(key, 32)

    def nrm(k, shape, scale):
        return jax.random.normal(k, shape, jnp.float32) * scale

    return {
        'x_prompt': nrm(ks[0], (BATCH, SEQ, D_MODEL), 1.0),
        'x_sample': nrm(ks[1], (DEC_BATCH, DEC_SEQ, D_MODEL), 1.0),
        'state_ssd_fwd': nrm(ks[2], (DEC_BATCH, N_SSD_LAYERS, SSD_HEADS, SSD_HEADDIM, SSD_STATE), 0.1),
        'state_ssd_bwd': nrm(ks[3], (DEC_BATCH, N_SSD_LAYERS, SSD_HEADS, SSD_HEADDIM, SSD_STATE), 0.1),
        'state_gdn_fwd': nrm(ks[4], (DEC_BATCH, N_GDN_LAYERS, GDN_V_HEADS, GDN_HEAD_K, GDN_HEAD_V), 0.1),
        'state_gdn_bwd': nrm(ks[5], (DEC_BATCH, N_GDN_LAYERS, GDN_V_HEADS, GDN_HEAD_K, GDN_HEAD_V), 0.1),
        'c': nrm(ks[6], (DEC_BATCH, D_MODEL), 1.0),
        'c_ctx': nrm(ks[7], (D_MODEL,), 1.0),
        'w_ada': nrm(ks[8], (DEPTH, D_MODEL, 6 * D_MODEL), D_MODEL ** -0.5),
        'b_ada': nrm(ks[9], (DEPTH, 6 * D_MODEL), 0.02),
        'ln_g': 1.0 + nrm(ks[10], (DEPTH, 2, D_MODEL), 0.01),
        'ln_b': nrm(ks[11], (DEPTH, 2, D_MODEL), 0.01),
        'ssd_w_in': nrm(ks[12], (N_SSD_LAYERS, D_MODEL, SSD_IN), D_MODEL ** -0.5),
        'ssd_conv_w': nrm(ks[13], (N_SSD_LAYERS, SSD_CONV, SSD_XBC), SSD_CONV ** -0.5),
        'ssd_conv_b': nrm(ks[14], (N_SSD_LAYERS, SSD_XBC), 0.02),
        'ssd_dt_bias': _dt_bias_init(ks[15], (N_SSD_LAYERS, 2, SSD_HEADS)),
        'ssd_a_log': jnp.log(jax.random.uniform(ks[16], (N_SSD_LAYERS, 2, SSD_HEADS), jnp.float32, 1.0, 16.0)),
        'ssd_d': 1.0 + nrm(ks[17], (N_SSD_LAYERS, SSD_HEADS), 0.1),
        'ssd_norm_w': 1.0 + nrm(ks[18], (N_SSD_LAYERS, SSD_D_INNER), 0.01),
        'ssd_w_out': nrm(ks[19], (N_SSD_LAYERS, SSD_D_INNER, D_MODEL), SSD_D_INNER ** -0.5 * DEEPNORM_BETA),
        'gdn_w_in': nrm(ks[20], (N_GDN_LAYERS, D_MODEL, GDN_IN), D_MODEL ** -0.5),
        'gdn_conv_w': nrm(ks[21], (N_GDN_LAYERS, GDN_CONV, GDN_QKV), GDN_CONV ** -0.5),
        'gdn_dt_bias': _dt_bias_init(ks[22], (N_GDN_LAYERS, 2, GDN_V_HEADS)),
        'gdn_a_log': jnp.log(jax.random.uniform(ks[23], (N_GDN_LAYERS, 2, GDN_V_HEADS), jnp.float32, 1.0, 16.0)),
        'gdn_norm_w': 1.0 + nrm(ks[24], (N_GDN_LAYERS, GDN_HEAD_V), 0.01),
        'gdn_w_out': nrm(ks[25], (N_GDN_LAYERS, GDN_V, D_MODEL), GDN_V ** -0.5 * DEEPNORM_BETA),
        'ffn_w_up': nrm(ks[26], (DEPTH, D_MODEL, 2 * D_FF), D_MODEL ** -0.5),
        'ffn_conv_w': nrm(ks[27], (DEPTH, FFN_CONV, FFN_CONV, D_FF), 1.0 / FFN_CONV),
        'ffn_conv_b': nrm(ks[28], (DEPTH, D_FF), 0.02),
        'ffn_w_down': nrm(ks[29], (DEPTH, D_FF, D_MODEL), D_FF ** -0.5 * DEEPNORM_BETA),
    }


def reference(x_prompt, x_sample, state_ssd_fwd, state_ssd_bwd, state_gdn_fwd, state_gdn_bwd, c, c_ctx,
              w_ada, b_ada, ln_g, ln_b,
              ssd_w_in, ssd_conv_w, ssd_conv_b, ssd_dt_bias, ssd_a_log, ssd_d, ssd_norm_w, ssd_w_out,
              gdn_w_in, gdn_conv_w, gdn_dt_bias, gdn_a_log, gdn_norm_w, gdn_w_out,
              ffn_w_up, ffn_conv_w, ffn_conv_b, ffn_w_down):
    rows = x_sample.shape[1] // GRID_W
    n_prompt = x_prompt.shape[0]
    xp, xs = x_prompt, x_sample
    cond_ctx = c_ctx[None, :]
    new_ssd_f, new_ssd_b, new_gdn_f, new_gdn_b = [], [], [], []
    for i in range(DEPTH):
        sh_mp, sc_mp, gt_mp, sh_fp, sc_fp, gt_fp = adaln_params(cond_ctx, w_ada[i], b_ada[i])
        sh_ms, sc_ms, gt_ms, sh_fs, sc_fs, gt_fs = adaln_params(c, w_ada[i], b_ada[i])
        hp = modulate(xp, sh_mp, sc_mp)
        hs = modulate(xs, sh_ms, sc_ms)
        j = i // N_MIXERS
        if i % N_MIXERS == 0:
            prm = (ssd_w_in[j], ssd_conv_w[j], ssd_conv_b[j], ssd_dt_bias[j], ssd_a_log[j], ssd_d[j],
                   ssd_norm_w[j], ssd_w_out[j])
            zeros = jnp.zeros((n_prompt, SSD_HEADS, SSD_HEADDIM, SSD_STATE), xp.dtype)
            mp, sf, sb = ssd_mixer(hp, zeros, zeros, *prm)
            ms, _, _ = ssd_mixer(hs, state_ssd_fwd[:, j], state_ssd_bwd[:, j], *prm)
            new_ssd_f.append(sf)
            new_ssd_b.append(sb)
        else:
            prm = (gdn_w_in[j], gdn_conv_w[j], gdn_dt_bias[j], gdn_a_log[j], gdn_norm_w[j], gdn_w_out[j])
            zeros = jnp.zeros((n_prompt, GDN_V_HEADS, GDN_HEAD_K, GDN_HEAD_V), xp.dtype)
            mp, sf, sb = gdn_mixer(hp, zeros, zeros, *prm)
            ms, _, _ = gdn_mixer(hs, state_gdn_fwd[:, j], state_gdn_bwd[:, j], *prm)
            new_gdn_f.append(sf)
            new_gdn_b.append(sb)
        xp = layer_norm(DEEPNORM_ALPHA * xp + gt_mp * mp, ln_g[i, 0], ln_b[i, 0])
        xs = layer_norm(DEEPNORM_ALPHA * xs + gt_ms * ms, ln_g[i, 0], ln_b[i, 0])
        fp = conv_glu_ffn(modulate(xp, sh_fp, sc_fp), ffn_w_up[i], ffn_conv_w[i], ffn_conv_b[i], ffn_w_down[i], None)
        fs = conv_glu_ffn(modulate(xs, sh_fs, sc_fs), ffn_w_up[i], ffn_conv_w[i], ffn_conv_b[i], ffn_w_down[i], rows)
        xp = layer_norm(DEEPNORM_ALPHA * xp + gt_fp * fp, ln_g[i, 1], ln_b[i, 1])
        xs = layer_norm(DEEPNORM_ALPHA * xs + gt_fs * fs, ln_g[i, 1], ln_b[i, 1])
    y_prompt = xp
    y_sample = xs
    new_state_ssd_fwd = jnp.stack(new_ssd_f, axis=1)
    new_state_ssd_bwd = jnp.stack(new_ssd_b, axis=1)
    new_state_gdn_fwd = jnp.stack(new_gdn_f, axis=1)
    new_state_gdn_bwd = jnp.stack(new_gdn_b, axis=1)
    return (y_prompt, y_sample, new_state_ssd_fwd, new_state_ssd_bwd, new_state_gdn_fwd, new_state_gdn_bwd)
```

```python
import functools
import math

import jax
import jax.numpy as jnp
from jax import lax
from jax.experimental import pallas as pl
from jax.experimental.pallas import tpu as pltpu

F32 = jnp.float32
BF16 = jnp.bfloat16
HI = lax.Precision.HIGHEST

D_MODEL = 1024
DEPTH = 2
GRID_W = 64

SSD_D_INNER = 2048
SSD_HEADDIM = 64
SSD_HEADS = 32
SSD_GROUPS = 8
SSD_STATE = 128
SSD_CHUNK = 128
SSD_XBC = 4096

GDN_K_HEADS = 8
GDN_V_HEADS = 16
GDN_HEAD = 128
GDN_QK = 1024
GDN_V = 2048
GDN_QKV = 4096
GDN_CHUNK = 64

D_FF = 2816

DEEPNORM_ALPHA = (2.0 * DEPTH) ** 0.25
LN_EPS = 1e-5
RMS_EPS = 1e-6

VMEM_LIMIT = 56 * 1024 * 1024


def _silu(x):
    return x / (1.0 + jnp.exp(-x))


def _sigmoid(x):
    return 1.0 / (1.0 + jnp.exp(-x))


def _softplus(x):
    return jnp.maximum(x, 0.0) + jnp.log(1.0 + jnp.exp(-jnp.abs(x)))


def _params(*sem):
    return pltpu.CompilerParams(dimension_semantics=sem, vmem_limit_bytes=VMEM_LIMIT)


def _ada_kernel(c_ref, w_ref, b_ref, o_ref):
    h = _silu(c_ref[...]).astype(BF16)
    o_ref[...] = jnp.dot(h, w_ref[...].astype(BF16), preferred_element_type=F32) + b_ref[...]


def _ada_call(cond, w_ada, b_ada):
    rows = cond.shape[0]
    tn = 1536
    return pl.pallas_call(
        _ada_kernel,
        grid=(DEPTH, 6 * D_MODEL // tn),
        in_specs=[
            pl.BlockSpec((rows, D_MODEL), lambda l, j: (0, 0)),
            pl.BlockSpec((None, D_MODEL, tn), lambda l, j: (l, 0, j)),
            pl.BlockSpec((None, 1, tn), lambda l, j: (l, 0, j)),
        ],
        out_specs=pl.BlockSpec((None, rows, tn), lambda l, j: (l, 0, j)),
        out_shape=jax.ShapeDtypeStruct((DEPTH, rows, 6 * D_MODEL), F32),
        compiler_params=_params("arbitrary", "arbitrary"),
        name="adaln",
    )(cond, w_ada, b_ada.reshape(DEPTH, 1, 6 * D_MODEL))


def _proj_kernel(*refs, has_small):
    if has_small:
        x_ref, sc_ref, sh_ref, w_ref, w2_ref, o_ref, o2_ref, h_ref = refs
    else:
        x_ref, sc_ref, sh_ref, w_ref, o_ref, h_ref = refs

    @pl.when(pl.program_id(1) == 0)
    def _():
        h = x_ref[...] * (1.0 + sc_ref[...]) + sh_ref[...]
        h_ref[...] = h.astype(BF16)
        if has_small:
            o2_ref[...] = jnp.dot(h_ref[...], w2_ref[...], preferred_element_type=F32)

    o_ref[...] = jnp.dot(h_ref[...], w_ref[...], preferred_element_type=F32)


def _proj_call(x, scale, shift, w, w_small, rows_per_mod, tm, tn, name):
    m = x.shape[0]
    n = w.shape[1]
    has_small = w_small is not None
    mod_spec = pl.BlockSpec((None, 1, D_MODEL), lambda i, j: ((i * tm) // rows_per_mod, 0, 0))
    in_specs = [pl.BlockSpec((tm, D_MODEL), lambda i, j: (i, 0)), mod_spec, mod_spec,
                pl.BlockSpec((D_MODEL, tn), lambda i, j: (0, j))]
    out_specs = [pl.BlockSpec((tm, tn), lambda i, j: (i, j))]
    out_shape = [jax.ShapeDtypeStruct((m, n), F32)]
    args = [x, scale, shift, w]
    if has_small:
        in_specs.append(pl.BlockSpec((D_MODEL, 128), lambda i, j: (0, 0)))
        out_specs.append(pl.BlockSpec((tm, 128), lambda i, j: (i, 0)))
        out_shape.append(jax.ShapeDtypeStruct((m, 128), F32))
        args.append(w_small)
    return pl.pallas_call(
        functools.partial(_proj_kernel, has_small=has_small),
        grid=(m // tm, n // tn),
        in_specs=in_specs,
        out_specs=out_specs,
        out_shape=out_shape,
        scratch_shapes=[pltpu.VMEM((tm, D_MODEL), BF16)],
        compiler_params=_params("arbitrary", "arbitrary"),
        name=name,
    )(*args)


def _conv_silu_rows(src_ref, r0, n_rows, seqlen, w_ref, bias):
    cur = src_ref[pl.ds(r0, n_rows), :]
    has_prev = (r0 > 0).astype(F32)
    has_next = (r0 + n_rows < seqlen).astype(F32)
    prev_row = src_ref[pl.ds(jnp.maximum(r0 - 1, 0), 1), :] * has_prev
    next_row = src_ref[pl.ds(jnp.minimum(r0 + n_rows, seqlen - 1), 1), :] * has_next
    rid = lax.broadcasted_iota(jnp.int32, cur.shape, 0)
    xm1 = jnp.where(rid == 0, prev_row, pltpu.roll(cur, 1, 0))
    xp1 = jnp.where(rid == n_rows - 1, next_row, pltpu.roll(cur, n_rows - 1, 0))
    out = w_ref[0:1, :] * xm1 + w_ref[1:2, :] * cur + w_ref[2:3, :] * xp1
    if bias is not None:
        out = out + bias
    return _silu(out)


def _transpose_rows(rows_list):
    pad = 128 - 8 * len(rows_list)
    stack = jnp.concatenate(rows_list + [jnp.zeros((pad, 128), F32)], axis=0)
    return stack.T


def _ssd_scan_kernel(*refs, seqlen, has_init, want_final):
    refs = list(refs)
    x_ref, b_ref, c_ref, cwx, cwb, cwc, cbx, cbb, cbc, dtt_ref, prm_ref, dsk_ref = refs[:12]
    pos = 12
    if has_init:
        s0_refs = refs[pos:pos + 2]
        pos += 2
    y_ref = refs[pos]
    pos += 1
    if want_final:
        sfin_refs = refs[pos:pos + 2]
        pos += 2
    xs, bs, cs, st = refs[pos:pos + 4]

    nc = seqlen // SSD_CHUNK
    q = SSD_CHUNK

    def prep(c, carry):
        r0 = pl.multiple_of(c * q, q)
        xs[pl.ds(r0, q), :] = _conv_silu_rows(x_ref, r0, q, seqlen, cwx, cbx[...])
        bs[pl.ds(r0, q), :] = _conv_silu_rows(b_ref, r0, q, seqlen, cwb, cbb[...])
        cs[pl.ds(r0, q), :] = _conv_silu_rows(c_ref, r0, q, seqlen, cwc, cbc[...])
        return carry

    lax.fori_loop(0, nc, prep, 0)

    for d in range(2):
        if has_init:
            halves = []
            for h in range(2):
                blk = jnp.concatenate([s0_refs[d][2 * h], s0_refs[d][2 * h + 1]], axis=0)
                halves.append(blk.T)
            st[d] = jnp.concatenate(halves, axis=1)
        else:
            st[d] = jnp.zeros((SSD_STATE, 256), F32)

    row = lax.broadcasted_iota(jnp.int32, (q, q), 0)
    col = lax.broadcasted_iota(jnp.int32, (q, q), 1)
    lo = col < 64
    bias = prm_ref[:, 0:1]
    a_neg = -jnp.exp(prm_ref[:, 1:2])
    dskip = dsk_ref[...]

    def chunk_step(c, d):
        r0 = pl.multiple_of(c * q, q)
        xc = xs[pl.ds(r0, q), :]
        bc = bs[pl.ds(r0, q), :]
        cc = cs[pl.ds(r0, q), :]
        dt_t = _softplus(dtt_ref[:, pl.ds(r0, q)] + bias)
        a_t = dt_t * a_neg
        if d == 0:
            cum_m = (row <= col).astype(F32)
            keep = row >= col
            last = q - 1
        else:
            cum_m = (row >= col).astype(F32)
            keep = row <= col
            last = 0
        acum_t = jnp.dot(a_t, cum_m, precision=HI, preferred_element_type=F32)
        ea_t = jnp.exp(acum_t)
        de_t = jnp.exp(acum_t[:, last:last + 1] - acum_t)
        cols = _transpose_rows([dt_t, acum_t, ea_t, de_t])

        def expand(base):
            halves = []
            for p in range(2):
                k0 = base + 4 * d + 2 * p
                halves.append(jnp.where(lo, cols[:, k0:k0 + 1], cols[:, k0 + 1:k0 + 2]))
            return jnp.concatenate(halves, axis=1)

        dt_e = expand(0)
        ea_e = expand(16)
        de_e = expand(24)
        xdt = xc * dt_e
        cc_b = cc.astype(BF16)
        bc_t = bc.T.astype(BF16)
        scores = jnp.dot(cc_b, bc_t, preferred_element_type=F32)
        s_t = st[d]
        y = jnp.dot(cc_b, s_t.astype(BF16), preferred_element_type=F32) * ea_e
        parts = []
        for r in range(4):
            k = 4 * d + r
            seg = cols[:, 8 + k:9 + k] - acum_t[k:k + 1, :]
            m = jnp.where(keep, jnp.exp(seg), 0.0) * scores
            parts.append(jnp.dot(m.astype(BF16), xdt[:, r * 64:(r + 1) * 64].astype(BF16),
                                 preferred_element_type=F32))
        y = y + jnp.concatenate(parts, axis=1)
        new_states = jnp.dot(bc_t, (xdt * de_e).astype(BF16), preferred_element_type=F32)
        st[d] = s_t * ea_e[last:last + 1, :] + new_states
        return r0, xc, y

    def fwd_body(i, carry):
        r0, xc, y = chunk_step(i, 0)
        y_ref[pl.ds(r0, q), :] = y + xc * dskip
        return carry

    def bwd_body(i, carry):
        r0, _, y = chunk_step(nc - 1 - i, 1)
        y_ref[pl.ds(r0, q), :] += y
        return carry

    lax.fori_loop(0, nc, fwd_body, 0)
    lax.fori_loop(0, nc, bwd_body, 0)

    if want_final:
        for d in range(2):
            s_t = st[d]
            for h in range(2):
                blk = s_t[:, h * 128:(h + 1) * 128].T
                sfin_refs[d][2 * h] = blk[0:64]
                sfin_refs[d][2 * h + 1] = blk[64:128]


def _ssd_scan_call(proj3, dtt, conv_w, conv_b, prm, dskip, s0, want_final, name):
    bsz, seqlen, _ = proj3.shape
    has_init = s0 is not None
    xoff = SSD_D_INNER // 256
    boff = (2 * SSD_D_INNER) // 128
    coff = boff + SSD_GROUPS
    in_specs = [
        pl.BlockSpec((None, seqlen, 256), lambda b, g: (b, 0, xoff + g)),
        pl.BlockSpec((None, seqlen, 128), lambda b, g: (b, 0, boff + g)),
        pl.BlockSpec((None, seqlen, 128), lambda b, g: (b, 0, coff + g)),
        pl.BlockSpec((3, 256), lambda b, g: (0, g)),
        pl.BlockSpec((3, 128), lambda b, g: (0, 16 + g)),
        pl.BlockSpec((3, 128), lambda b, g: (0, 24 + g)),
        pl.BlockSpec((1, 256), lambda b, g: (0, g)),
        pl.BlockSpec((1, 128), lambda b, g: (0, 16 + g)),
        pl.BlockSpec((1, 128), lambda b, g: (0, 24 + g)),
        pl.BlockSpec((None, None, 8, seqlen), lambda b, g: (b, g, 0, 0)),
        pl.BlockSpec((None, 8, 2), lambda b, g: (g, 0, 0)),
        pl.BlockSpec((None, 1, 256), lambda b, g: (g, 0, 0)),
    ]
    args = [proj3, proj3, proj3, conv_w, conv_w, conv_w, conv_b, conv_b, conv_b, dtt, prm, dskip]
    state_spec = pl.BlockSpec((None, 4, SSD_HEADDIM, SSD_STATE), lambda b, g: (b, g, 0, 0))
    if has_init:
        in_specs += [state_spec, state_spec]
        args += [s0[0], s0[1]]
    out_specs = [pl.BlockSpec((None, seqlen, 256), lambda b, g: (b, 0, g))]
    out_shape = [jax.ShapeDtypeStruct((bsz, seqlen, SSD_D_INNER), F32)]
    if want_final:
        out_specs += [state_spec, state_spec]
        out_shape += [jax.ShapeDtypeStruct((bsz, SSD_HEADS, SSD_HEADDIM, SSD_STATE), F32)] * 2
    return pl.pallas_call(
        functools.partial(_ssd_scan_kernel, seqlen=seqlen, has_init=has_init, want_final=want_final),
        grid=(bsz, SSD_GROUPS),
        in_specs=in_specs,
        out_specs=out_specs,
        out_shape=out_shape,
        scratch_shapes=[pltpu.VMEM((seqlen, 256), F32), pltpu.VMEM((seqlen, 128), F32),
                        pltpu.VMEM((seqlen, 128), F32), pltpu.VMEM((2, SSD_STATE, 256), F32)],
        compiler_params=_params("arbitrary", "arbitrary"),
        name=name,
    )(*args)


def _gdn_scan_kernel(*refs, seqlen, has_init, want_final):
    refs = list(refs)
    q_ref, k_ref, v_ref, cwq, cwk, cwv, smt_ref, prm_ref = refs[:8]
    pos = 8
    if has_init:
        s0_refs = refs[pos:pos + 2]
        pos += 2
    o_ref = refs[pos]
    pos += 1
    if want_final:
        sfin_refs = refs[pos:pos + 2]
        pos += 2
    qs, ks, vs, st = refs[pos:pos + 4]

    blk = 2 * GDN_CHUNK
    nb = seqlen // blk
    qk_scale = GDN_HEAD ** -0.5

    def prep(c, carry):
        r0 = pl.multiple_of(c * blk, blk)
        qv = _conv_silu_rows(q_ref, r0, blk, seqlen, cwq, None)
        kv = _conv_silu_rows(k_ref, r0, blk, seqlen, cwk, None)
        qs[pl.ds(r0, blk), :] = qv * (lax.rsqrt(jnp.sum(qv * qv, axis=-1, keepdims=True) + RMS_EPS) * qk_scale)
        ks[pl.ds(r0, blk), :] = kv * lax.rsqrt(jnp.sum(kv * kv, axis=-1, keepdims=True) + RMS_EPS)
        vs[pl.ds(r0, blk), :] = _conv_silu_rows(v_ref, r0, blk, seqlen, cwv, None)
        return carry

    lax.fori_loop(0, nb, prep, 0)

    for d in range(2):
        for r in range(2):
            if has_init:
                st[d, r] = s0_refs[d][r]
            else:
                st[d, r] = jnp.zeros((GDN_HEAD, GDN_HEAD), F32)

    row = lax.broadcasted_iota(jnp.int32, (blk, blk), 0)
    col = lax.broadcasted_iota(jnp.int32, (blk, blk), 1)
    same = (row < GDN_CHUNK) == (col < GDN_CHUNK)
    eye = (row == col).astype(F32)
    merge_masks = [((row >> (l + 1)) == (col >> (l + 1))) & ((row >> l) != (col >> l)) for l in range(6)]
    lane8 = lax.broadcasted_iota(jnp.int32, (8, blk), 1)
    bias = prm_ref[:, 0:1]
    a_neg = -jnp.exp(prm_ref[:, 1:2])
    zeros_half = jnp.zeros((GDN_CHUNK, GDN_HEAD), F32)

    def block_step(c, d):
        r0 = pl.multiple_of(c * blk, blk)
        qc = qs[pl.ds(r0, blk), :]
        kc = ks[pl.ds(r0, blk), :]
        vc = vs[pl.ds(r0, blk), :]
        sm = smt_ref[:, pl.ds(r0, blk)]
        g_t = a_neg * _softplus(sm + bias)
        beta_t = _sigmoid(sm)
        if d == 0:
            cum_m = jnp.where(same & (row <= col), 1.0, 0.0)
            strict = same & (row > col)
            incl = same & (row >= col)
            e0, e1 = GDN_CHUNK - 1, blk - 1
            order = (0, 1)
        else:
            cum_m = jnp.where(same & (row >= col), 1.0, 0.0)
            strict = same & (row < col)
            incl = same & (row <= col)
            e0, e1 = 0, GDN_CHUNK
            order = (1, 0)
        gc_t = jnp.dot(g_t, cum_m, precision=HI, preferred_element_type=F32)
        tot_t = jnp.where(lane8 < GDN_CHUNK, gc_t[:, e0:e0 + 1], gc_t[:, e1:e1 + 1])
        eg_t = jnp.exp(gc_t)
        kd_t = jnp.exp(tot_t - gc_t)
        gl_t = jnp.exp(tot_t)
        cols = _transpose_rows([gc_t, eg_t, kd_t, beta_t])

        qc_b = qc.astype(BF16)
        kc_b = kc.astype(BF16)
        nt = (((1,), (1,)), ((), ()))
        kk = lax.dot_general(kc_b, kc_b, nt, preferred_element_type=F32)
        qk = lax.dot_general(qc_b, kc_b, nt, preferred_element_type=F32)

        outs = []
        for r in range(2):
            k = 2 * d + r
            e = jnp.exp(cols[:, k:k + 1] - gc_t[k:k + 1, :])
            dec_s = jnp.where(strict, e, 0.0)
            dec_i = jnp.where(incl, e, 0.0)
            bcol = cols[:, 28 + k:29 + k]
            egc = cols[:, 8 + k:9 + k]
            a = bcol * kk * dec_s
            y = eye - jnp.where(merge_masks[0], a, 0.0)
            for lvl in range(1, 6):
                x = jnp.dot(jnp.where(merge_masks[lvl], a, 0.0).astype(BF16), y.astype(BF16),
                            preferred_element_type=F32)
                y = y - jnp.dot(y.astype(BF16), x.astype(BF16), preferred_element_type=F32)
            vr = vc[:, r * 128:(r + 1) * 128]
            rhs = jnp.concatenate([vr * bcol, kc * (bcol * egc)], axis=1)
            sol = jnp.dot(y.astype(BF16), rhs.astype(BF16), preferred_element_type=F32)
            u = sol[:, :GDN_HEAD]
            w = sol[:, GDN_HEAD:]
            qd = qc * egc
            kd_tr = (kc * cols[:, 16 + k:17 + k]).T.astype(BF16)
            qkm = (qk * dec_i).astype(BF16)
            s = st[d, r]
            o_parts = [None, None]
            for sub in order:
                rs = slice(sub * GDN_CHUNK, (sub + 1) * GDN_CHUNK)
                lhs = jnp.concatenate([w[rs], qd[rs]], axis=0).astype(BF16)
                ws = jnp.dot(lhs, s.astype(BF16), preferred_element_type=F32)
                vn = u[rs] - ws[:GDN_CHUNK]
                vn_full = (jnp.concatenate([vn, zeros_half], axis=0) if sub == 0
                           else jnp.concatenate([zeros_half, vn], axis=0)).astype(BF16)
                o_parts[sub] = ws[GDN_CHUNK:] + jnp.dot(qkm[rs], vn_full, preferred_element_type=F32)
                e_last = e0 if sub == 0 else e1
                s = gl_t[k:k + 1, e_last:e_last + 1] * s + jnp.dot(kd_tr, vn_full, preferred_element_type=F32)
            st[d, r] = s
            outs.append(jnp.concatenate(o_parts, axis=0))
        return r0, jnp.concatenate(outs, axis=1)

    def fwd_body(i, carry):
        r0, o = block_step(i, 0)
        o_ref[pl.ds(r0, blk), :] = o
        return carry

    def bwd_body(i, carry):
        r0, o = block_step(nb - 1 - i, 1)
        o_ref[pl.ds(r0, blk), :] += o
        return carry

    lax.fori_loop(0, nb, fwd_body, 0)
    lax.fori_loop(0, nb, bwd_body, 0)

    if want_final:
        for d in range(2):
            for r in range(2):
                sfin_refs[d][r] = st[d, r]


def _gdn_scan_call(proj3, smt, conv_w, prm, s0, want_final, name):
    bsz, seqlen, _ = proj3.shape
    has_init = s0 is not None
    in_specs = [
        pl.BlockSpec((None, seqlen, 128), lambda b, g: (b, 0, g)),
        pl.BlockSpec((None, seqlen, 128), lambda b, g: (b, 0, GDN_K_HEADS + g)),
        pl.BlockSpec((None, seqlen, 256), lambda b, g: (b, 0, GDN_K_HEADS + g)),
        pl.BlockSpec((3, 128), lambda b, g: (0, g)),
        pl.BlockSpec((3, 128), lambda b, g: (0, GDN_K_HEADS + g)),
        pl.BlockSpec((3, 256), lambda b, g: (0, GDN_K_HEADS + g)),
        pl.BlockSpec((None, None, 8, seqlen), lambda b, g: (b, g, 0, 0)),
        pl.BlockSpec((None, 8, 2), lambda b, g: (g, 0, 0)),
    ]
    args = [proj3, proj3, proj3, conv_w, conv_w, conv_w, smt, prm]
    state_spec = pl.BlockSpec((None, 2, GDN_HEAD, GDN_HEAD), lambda b, g: (b, g, 0, 0))
    if has_init:
        in_specs += [state_spec, state_spec]
        args += [s0[0], s0[1]]
    out_specs = [pl.BlockSpec((None, seqlen, 256), lambda b, g: (b, 0, g))]
    out_shape = [jax.ShapeDtypeStruct((bsz, seqlen, GDN_V), F32)]
    if want_final:
        out_specs += [state_spec, state_spec]
        out_shape += [jax.ShapeDtypeStruct((bsz, GDN_V_HEADS, GDN_HEAD, GDN_HEAD), F32)] * 2
    return pl.pallas_call(
        functools.partial(_gdn_scan_kernel, seqlen=seqlen, has_init=has_init, want_final=want_final),
        grid=(bsz, GDN_K_HEADS),
        in_specs=in_specs,
        out_specs=out_specs,
        out_shape=out_shape,
        scratch_shapes=[pltpu.VMEM((seqlen, 128), F32), pltpu.VMEM((seqlen, 128), F32),
                        pltpu.VMEM((seqlen, 256), F32), pltpu.VMEM((2, 2, GDN_HEAD, GDN_HEAD), F32)],
        compiler_params=_params("arbitrary", "arbitrary"),
        name=name,
    )(*args)


def _out_kernel(*refs, mode):
    if mode == "plain":
        a_ref, w_ref, x_ref, gt_ref, lg_ref, lb_ref, o_ref = refs
        a = a_ref[...]
    else:
        y_ref, z_ref, nw_ref, w_ref, x_ref, gt_ref, lg_ref, lb_ref, o_ref = refs
        if mode == "ssd":
            u = y_ref[...] * _silu(z_ref[...])
            ms = jnp.mean(u * u, axis=-1, keepdims=True)
            a = (u * lax.rsqrt(ms + RMS_EPS) * nw_ref[...]).astype(BF16)
        else:
            parts = []
            for h in range(GDN_V_HEADS):
                sl = slice(h * GDN_HEAD, (h + 1) * GDN_HEAD)
                oh = y_ref[:, sl]
                ms = jnp.mean(oh * oh, axis=-1, keepdims=True)
                parts.append((oh * lax.rsqrt(ms + RMS_EPS) * nw_ref[...] * _silu(z_ref[:, sl])).astype(BF16))
            a = jnp.concatenate(parts, axis=1)
    o = jnp.dot(a, w_ref[...], preferred_element_type=F32)
    r = DEEPNORM_ALPHA * x_ref[...] + gt_ref[...] * o
    mu = jnp.mean(r, axis=-1, keepdims=True)
    rc = r - mu
    var = jnp.mean(rc * rc, axis=-1, keepdims=True)
    o_ref[...] = rc * lax.rsqrt(var + LN_EPS) * lg_ref[...] + lb_ref[...]


def _out_call(mode, act, z_src, z_blk, norm_w, w, x, gate, ln_g, ln_b, rows_per_mod, tm, name):
    m, kdim = act.shape
    row_spec = pl.BlockSpec((tm, D_MODEL), lambda i: (i, 0))
    vec_spec = pl.BlockSpec((1, D_MODEL), lambda i: (0, 0))
    in_specs = [pl.BlockSpec((tm, kdim), lambda i: (i, 0))]
    args = [act]
    if mode != "plain":
        in_specs += [pl.BlockSpec((tm, kdim), lambda i: (i, z_blk)),
                     pl.BlockSpec((1, norm_w.shape[1]), lambda i: (0, 0))]
        args += [z_src, norm_w]
    in_specs += [pl.BlockSpec((kdim, D_MODEL), lambda i: (0, 0)), row_spec,
                 pl.BlockSpec((None, 1, D_MODEL), lambda i: ((i * tm) // rows_per_mod, 0, 0)),
                 vec_spec, vec_spec]
    args += [w, x, gate, ln_g, ln_b]
    return pl.pallas_call(
        functools.partial(_out_kernel, mode=mode),
        grid=(m // tm,),
        in_specs=in_specs,
        out_specs=row_spec,
        out_shape=jax.ShapeDtypeStruct((m, D_MODEL), F32),
        compiler_params=_params("arbitrary"),
        name=name,
    )(*args)


def _convglu_kernel(*refs, tm, period, row_taps, tiles_per_seq):
    if row_taps:
        a_ref, ap_ref, an_ref, v_ref, w_ref, b_ref, o_ref, ext_ref = refs
        i = pl.program_id(0)
        t = i % tiles_per_seq
        ext_ref[0:128, :] = ap_ref[...] * (t > 0).astype(F32)
        ext_ref[128:128 + tm, :] = a_ref[...]
        ext_ref[128 + tm:, :] = an_ref[...] * (t < tiles_per_seq - 1).astype(F32)
        ext = ext_ref[...]
        base = 128
        taps = (0, 1, 2)
    else:
        a_ref, v_ref, w_ref, b_ref, o_ref = refs
        ext = a_ref[...]
        base = 0
        taps = (1,)
    n = ext.shape[0]
    colpos = lax.broadcasted_iota(jnp.int32, ext.shape, 0) % period
    em1 = jnp.where(colpos == 0, 0.0, pltpu.roll(ext, 1, 0))
    ep1 = jnp.where(colpos == period - 1, 0.0, pltpu.roll(ext, n - 1, 0))
    acc = b_ref[...]
    for dr in taps:
        off = base + (dr - 1) * GRID_W
        acc = (acc + w_ref[3 * dr:3 * dr + 1, :] * em1[off:off + tm]
               + w_ref[3 * dr + 1:3 * dr + 2, :] * ext[off:off + tm]
               + w_ref[3 * dr + 2:3 * dr + 3, :] * ep1[off:off + tm])
    o_ref[...] = (_silu(acc) * v_ref[...]).astype(BF16)


def _convglu_call(up, conv_w9, conv_b, seqlen, is_grid, tm, tc, name):
    m = up.shape[0]
    nct = D_FF // tc
    if is_grid:
        hb = tm // 128
        nhb = m // 128
        tiles_per_seq = seqlen // tm
        in_specs = [
            pl.BlockSpec((tm, tc), lambda i, j: (i, j)),
            pl.BlockSpec((128, tc), lambda i, j: (jnp.maximum(i * hb - 1, 0), j)),
            pl.BlockSpec((128, tc), lambda i, j: (jnp.minimum((i + 1) * hb, nhb - 1), j)),
            pl.BlockSpec((tm, tc), lambda i, j: (i, j + nct)),
        ]
        args = [up, up, up, up]
        scratch = [pltpu.VMEM((tm + 256, tc), F32)]
        period = GRID_W
    else:
        tiles_per_seq = 1
        in_specs = [pl.BlockSpec((tm, tc), lambda i, j: (i, j)),
                    pl.BlockSpec((tm, tc), lambda i, j: (i, j + nct))]
        args = [up, up]
        scratch = []
        period = seqlen
    in_specs += [pl.BlockSpec((9, tc), lambda i, j: (0, j)), pl.BlockSpec((1, tc), lambda i, j: (0, j))]
    args += [conv_w9, conv_b]
    return pl.pallas_call(
        functools.partial(_convglu_kernel, tm=tm, period=period, row_taps=is_grid, tiles_per_seq=tiles_per_seq),
        grid=(m // tm, nct),
        in_specs=in_specs,
        out_specs=pl.BlockSpec((tm, tc), lambda i, j: (i, j)),
        out_shape=jax.ShapeDtypeStruct((m, D_FF), BF16),
        scratch_shapes=scratch,
        compiler_params=_params("arbitrary", "arbitrary"),
        name=name,
    )(*args)


def _group_rows(t, bsz, seqlen, lead):
    n_lead = int(math.prod(lead))
    t = t.reshape(bsz, seqlen, n_lead, 8, -1)
    r = t.shape[-1]
    t = jnp.transpose(t, (0, 3, 2, 4, 1))
    return t.reshape(bsz, 8, n_lead * r, seqlen)


def _group_params(p):
    r = p.shape[1] // 8
    return jnp.transpose(p.reshape(2, 8, r), (1, 0, 2)).reshape(8, 2 * r)


def _run_tokens(x, seqlen, mods, states, want_final, w, tag):
    bsz = x.shape[0]
    m = bsz * seqlen
    nb = mods.shape[1]
    rows_per_mod = m // nb
    xt = x.reshape(m, D_MODEL)
    finals = []
    is_grid = tag == "s"
    for i in range(DEPTH):
        def mod(k):
            return mods[i, :, k].reshape(nb, 1, D_MODEL)
        lw = w[i]
        proj, small = _proj_call(xt, mod(1), mod(0), lw["w_in"], lw["w_in_small"], rows_per_mod,
                                 tm=1024, tn=1536, name=f"inproj{i}{tag}")
        proj3 = proj.reshape(bsz, seqlen, 6144)
        if i % 2 == 0:
            dtt = _group_rows(small[:, :2 * SSD_HEADS], bsz, seqlen, (2,))
            res = _ssd_scan_call(proj3, dtt, lw["conv_w"], lw["conv_b"], lw["prm"], lw["dskip"],
                                 states[i], want_final, name=f"ssdscan{tag}")
            mode, z_blk = "ssd", 0
        else:
            smt = _group_rows(small[:, :4 * GDN_V_HEADS], bsz, seqlen, (2, 2))
            res = _gdn_scan_call(proj3, smt, lw["conv_w"], lw["prm"], states[i], want_final,
                                 name=f"gdnscan{tag}")
            mode, z_blk = "gdn", 2
        y = res[0].reshape(m, 2048)
        if want_final:
            finals.append((res[1], res[2]))
        xt = _out_call(mode, y, proj, z_blk, lw["norm_w"], lw["w_out"], xt, mod(2), lw["ln_g0"], lw["ln_b0"],
                       rows_per_mod, tm=256, name=f"outproj{i}{tag}")
        up = _proj_call(xt, mod(4), mod(3), lw["w_up"], None, rows_per_mod,
                        tm=1024, tn=1408, name=f"ffnup{i}{tag}")[0]
        act = _convglu_call(up, lw["ffn_conv_w"], lw["ffn_conv_b"], seqlen, is_grid,
                            tm=1024, tc=256, name=f"convglu{i}{tag}")
        xt = _out_call("plain", act, None, 0, None, lw["w_down"], xt, mod(5), lw["ln_g1"], lw["ln_b1"],
                       rows_per_mod, tm=256, name=f"ffndown{i}{tag}")
    return xt.reshape(bsz, seqlen, D_MODEL), finals


def kernel(x_prompt, x_sample, state_ssd_fwd, state_ssd_bwd, state_gdn_fwd, state_gdn_bwd, c, c_ctx, w_ada, b_ada, ln_g, ln_b, ssd_w_in, ssd_conv_w, ssd_conv_b, ssd_dt_bias, ssd_a_log, ssd_d, ssd_norm_w, ssd_w_out, gdn_w_in, gdn_conv_w, gdn_dt_bias, gdn_a_log, gdn_norm_w, gdn_w_out, ffn_w_up, ffn_conv_w, ffn_conv_b, ffn_w_down):
    n_dec = c.shape[0]
    cond = jnp.concatenate([c_ctx[None, :], c, jnp.zeros((16 - 1 - n_dec, D_MODEL), F32)], axis=0)
    ada = _ada_call(cond, w_ada, b_ada).reshape(DEPTH, 16, 6, D_MODEL)
    mods_p = ada[:, 0:1]
    mods_s = ada[:, 1:1 + n_dec]

    def small_cols(wm):
        return jnp.pad(wm, ((0, 0), (0, 128 - wm.shape[1]))).astype(BF16)

    weights = []
    for i in range(DEPTH):
        j = i // 2
        lw = {
            "ln_g0": ln_g[i, 0:1], "ln_b0": ln_b[i, 0:1], "ln_g1": ln_g[i, 1:2], "ln_b1": ln_b[i, 1:2],
            "w_up": ffn_w_up[i].astype(BF16), "w_down": ffn_w_down[i].astype(BF16),
            "ffn_conv_w": ffn_conv_w[i].reshape(9, D_FF), "ffn_conv_b": ffn_conv_b[i].reshape(1, D_FF),
        }
        if i % 2 == 0:
            lw["w_in"] = ssd_w_in[j][:, :6144].astype(BF16)
            lw["w_in_small"] = small_cols(ssd_w_in[j][:, 6144:])
            lw["conv_w"] = ssd_conv_w[j]
            lw["conv_b"] = ssd_conv_b[j].reshape(1, SSD_XBC)
            lw["prm"] = jnp.stack([_group_params(ssd_dt_bias[j]), _group_params(ssd_a_log[j])], axis=-1)
            lw["dskip"] = jnp.repeat(ssd_d[j], SSD_HEADDIM).reshape(SSD_GROUPS, 1, 256)
            lw["norm_w"] = ssd_norm_w[j].reshape(1, SSD_D_INNER)
            lw["w_out"] = ssd_w_out[j].astype(BF16)
        else:
            lw["w_in"] = gdn_w_in[j][:, :6144].astype(BF16)
            lw["w_in_small"] = small_cols(gdn_w_in[j][:, 6144:])
            lw["conv_w"] = gdn_conv_w[j]
            prm = jnp.stack([_group_params(gdn_dt_bias[j]), _group_params(gdn_a_log[j])], axis=-1)
            lw["prm"] = jnp.pad(prm, ((0, 0), (0, 4), (0, 0)))
            lw["norm_w"] = gdn_norm_w[j].reshape(1, GDN_HEAD)
            lw["w_out"] = gdn_w_out[j].astype(BF16)
        weights.append(lw)

    states_s = [(state_ssd_fwd[:, 0], state_ssd_bwd[:, 0]), (state_gdn_fwd[:, 0], state_gdn_bwd[:, 0])]
    y_prompt, fin = _run_tokens(x_prompt, x_prompt.shape[1], mods_p, [None, None], True, weights, "p")
    y_sample, _ = _run_tokens(x_sample, x_sample.shape[1], mods_s, states_s, False, weights, "s")
    return (y_prompt, y_sample,
            fin[0][0][:, None], fin[0][1][:, None], fin[1][0][:, None], fin[1][1][:, None])
```

```python
import functools
import math

import jax
import jax.numpy as jnp
from jax import lax
from jax.experimental import pallas as pl
from jax.experimental.pallas import tpu as pltpu

F32 = jnp.float32
BF16 = jnp.bfloat16
HI = lax.Precision.HIGHEST

D_MODEL = 1024
DEPTH = 2
GRID_W = 64

SSD_D_INNER = 2048
SSD_HEADDIM = 64
SSD_HEADS = 32
SSD_GROUPS = 8
SSD_STATE = 128
SSD_CHUNK = 128
SSD_XBC = 4096

GDN_K_HEADS = 8
GDN_V_HEADS = 16
GDN_HEAD = 128
GDN_QK = 1024
GDN_V = 2048
GDN_QKV = 4096
GDN_BLOCK = 128

D_FF = 2816

DEEPNORM_ALPHA = (2.0 * DEPTH) ** 0.25
LN_EPS = 1e-5
RMS_EPS = 1e-6

VMEM_LIMIT = 56 * 1024 * 1024


def _silu(x):
    return x / (1.0 + jnp.exp(-x))


def _sigmoid(x):
    return 1.0 / (1.0 + jnp.exp(-x))


def _softplus(x):
    return jnp.maximum(x, 0.0) + jnp.log(1.0 + jnp.exp(-jnp.abs(x)))


def _params(*sem):
    return pltpu.CompilerParams(dimension_semantics=sem, vmem_limit_bytes=VMEM_LIMIT)


def _ada_kernel(c_ref, w_ref, b_ref, o_ref):
    h = _silu(c_ref[...]).astype(BF16)
    o_ref[...] = jnp.dot(h, w_ref[...].astype(BF16), preferred_element_type=F32) + b_ref[...]


def _ada_call(cond, w_ada, b_ada):
    rows = cond.shape[0]
    tn = 1536
    return pl.pallas_call(
        _ada_kernel,
        grid=(DEPTH, 6 * D_MODEL // tn),
        in_specs=[
            pl.BlockSpec((rows, D_MODEL), lambda l, j: (0, 0)),
            pl.BlockSpec((None, D_MODEL, tn), lambda l, j: (l, 0, j)),
            pl.BlockSpec((None, 1, tn), lambda l, j: (l, 0, j)),
        ],
        out_specs=pl.BlockSpec((None, rows, tn), lambda l, j: (l, 0, j)),
        out_shape=jax.ShapeDtypeStruct((DEPTH, rows, 6 * D_MODEL), F32),
        compiler_params=_params("arbitrary", "arbitrary"),
        name="adaln",
    )(cond, w_ada, b_ada.reshape(DEPTH, 1, 6 * D_MODEL))


def _proj_kernel(*refs, has_small):
    if has_small:
        x_ref, sc_ref, sh_ref, w_ref, w2_ref, o_ref, o2_ref, h_ref = refs
    else:
        x_ref, sc_ref, sh_ref, w_ref, o_ref, h_ref = refs

    @pl.when(pl.program_id(1) == 0)
    def _():
        h = x_ref[...] * (1.0 + sc_ref[...]) + sh_ref[...]
        h_ref[...] = h.astype(BF16)
        if has_small:
            o2_ref[...] = jnp.dot(h_ref[...], w2_ref[...], preferred_element_type=F32)

    o_ref[...] = jnp.dot(h_ref[...], w_ref[...], preferred_element_type=F32)


def _proj_call(x, scale, shift, w, w_small, rows_per_mod, tm, tn, name):
    m = x.shape[0]
    n = w.shape[1]
    has_small = w_small is not None
    mod_spec = pl.BlockSpec((None, 1, D_MODEL), lambda i, j: ((i * tm) // rows_per_mod, 0, 0))
    in_specs = [pl.BlockSpec((tm, D_MODEL), lambda i, j: (i, 0)), mod_spec, mod_spec,
                pl.BlockSpec((D_MODEL, tn), lambda i, j: (0, j))]
    out_specs = [pl.BlockSpec((tm, tn), lambda i, j: (i, j))]
    out_shape = [jax.ShapeDtypeStruct((m, n), F32)]
    args = [x, scale, shift, w]
    if has_small:
        in_specs.append(pl.BlockSpec((D_MODEL, 128), lambda i, j: (0, 0)))
        out_specs.append(pl.BlockSpec((tm, 128), lambda i, j: (i, 0)))
        out_shape.append(jax.ShapeDtypeStruct((m, 128), F32))
        args.append(w_small)
    return pl.pallas_call(
        functools.partial(_proj_kernel, has_small=has_small),
        grid=(m // tm, n // tn),
        in_specs=in_specs,
        out_specs=out_specs,
        out_shape=out_shape,
        scratch_shapes=[pltpu.VMEM((tm, D_MODEL), BF16)],
        compiler_params=_params("arbitrary", "arbitrary"),
        name=name,
    )(*args)


def _conv_silu_rows(src_ref, r0, n_rows, seqlen, w_ref, bias):
    cur = src_ref[pl.ds(r0, n_rows), :]
    has_prev = (r0 > 0).astype(F32)
    has_next = (r0 + n_rows < seqlen).astype(F32)
    prev_row = src_ref[pl.ds(jnp.maximum(r0 - 1, 0), 1), :] * has_prev
    next_row = src_ref[pl.ds(jnp.minimum(r0 + n_rows, seqlen - 1), 1), :] * has_next
    rid = lax.broadcasted_iota(jnp.int32, cur.shape, 0)
    xm1 = jnp.where(rid == 0, prev_row, pltpu.roll(cur, 1, 0))
    xp1 = jnp.where(rid == n_rows - 1, next_row, pltpu.roll(cur, n_rows - 1, 0))
    out = w_ref[0:1, :] * xm1 + w_ref[1:2, :] * cur + w_ref[2:3, :] * xp1
    if bias is not None:
        out = out + bias
    return _silu(out)


def _transpose_rows(rows_list):
    pad = 128 - 8 * len(rows_list)
    stack = jnp.concatenate(rows_list + [jnp.zeros((pad, 128), F32)], axis=0)
    return stack.T


def _ssd_scan_kernel(*refs, seqlen, has_init, want_final):
    refs = list(refs)
    x_ref, b_ref, c_ref, cwx, cwb, cwc, cbx, cbb, cbc, dtt_ref, prm_ref, dsk_ref = refs[:12]
    pos = 12
    if has_init:
        s0_refs = refs[pos:pos + 2]
        pos += 2
    y_ref = refs[pos]
    pos += 1
    if want_final:
        sfin_refs = refs[pos:pos + 2]
        pos += 2
    cs_b, ea_s, ns_s, st = refs[pos:pos + 4]

    nc = seqlen // SSD_CHUNK
    q = SSD_CHUNK

    for d in range(2):
        if has_init:
            halves = []
            for h in range(2):
                blk = jnp.concatenate([s0_refs[d][2 * h], s0_refs[d][2 * h + 1]], axis=0)
                halves.append(blk.T)
            st[d] = jnp.concatenate(halves, axis=1)
        else:
            st[d] = jnp.zeros((SSD_STATE, 256), F32)

    row = lax.broadcasted_iota(jnp.int32, (q, q), 0)
    col = lax.broadcasted_iota(jnp.int32, (q, q), 1)
    lane_head = lax.broadcasted_iota(jnp.int32, (q, 256), 1) // SSD_HEADDIM
    bias = prm_ref[:, 0:1]
    a_neg = -jnp.exp(prm_ref[:, 1:2])
    dskip = dsk_ref[...]

    cum_ms = ((row <= col).astype(F32), (row >= col).astype(F32))
    keeps = (row >= col, row <= col)
    lasts = (q - 1, 0)

    def intra(i, carry):
        blocks = []
        for j in range(unroll_a):
            r0 = pl.multiple_of((i * unroll_a + j) * q, q)
            xc = _conv_silu_rows(x_ref, r0, q, seqlen, cwx, cbx[...])
            bc = _conv_silu_rows(b_ref, r0, q, seqlen, cwb, cbb[...])
            cc = _conv_silu_rows(c_ref, r0, q, seqlen, cwc, cbc[...])
            cc_b = cc.astype(BF16)
            cs_b[pl.ds(r0, q), :] = cc_b
            bc_t = bc.T
            dt_t = _softplus(dtt_ref[:, pl.ds(r0, q)] + bias)
            x_heads = jnp.concatenate([jnp.where(lane_head == r, xc, 0.0).astype(BF16) for r in range(4)], axis=0)
            blocks.append(dict(r0=r0, xc=xc, bc_t=bc_t, dt_t=dt_t, a_t=dt_t * a_neg, x_heads=x_heads,
                               scores=jnp.dot(cc_b, bc_t.astype(BF16), preferred_element_type=F32)))
        units = [(blk, d) for blk in blocks for d in range(2)]
        acums = [jnp.dot(blk["a_t"], cum_ms[d], precision=HI, preferred_element_type=F32) for blk, d in units]
        cols = _transpose_rows(acums)
        dtdes = [blk["dt_t"] * jnp.exp(ac[:, lasts[d]:lasts[d] + 1] - ac) for (blk, d), ac in zip(units, acums)]
        for (blk, d), ac in zip(units, acums):
            ea_t = jnp.exp(ac)
            ea_rows = jnp.concatenate([jnp.broadcast_to(ea_t[4 * d + r:4 * d + r + 1, :], (SSD_HEADDIM, q))
                                       for r in range(4)], axis=0)
            ea_s[d, pl.ds(blk["r0"], q), :] = jnp.concatenate([ea_rows[0:128].T, ea_rows[128:256].T], axis=1)
        lhss = []
        for u, ((blk, d), ac, dtde) in enumerate(zip(units, acums, dtdes)):
            parts = []
            for r in range(4):
                k = 4 * d + r
                decay = jnp.where(keeps[d], jnp.exp(cols[:, 8 * u + k:8 * u + k + 1] - ac[k:k + 1, :]), 0.0)
                m = decay * blk["scores"] * blk["dt_t"][k:k + 1, :]
                bt = blk["bc_t"] * dtde[k:k + 1, :]
                parts.append(jnp.concatenate([m, bt], axis=0).astype(BF16))
            lhss.append(jnp.concatenate(parts, axis=1))
        ress = [jnp.dot(lhs, blk["x_heads"], preferred_element_type=F32) for lhs, (blk, d) in zip(lhss, units)]
        for res, (blk, d) in zip(ress, units):
            ns_s[d, pl.ds(blk["r0"], q), :] = res[q:]
        for j, blk in enumerate(blocks):
            y_ref[pl.ds(blk["r0"], q), :] = blk["xc"] * dskip + ress[2 * j][:q] + ress[2 * j + 1][:q]
        return carry

    unroll_a = 2 if nc % 2 == 0 else 1
    lax.fori_loop(0, nc // unroll_a, intra, 0)

    def inter(i, carry):
        for d, c in ((0, i), (1, nc - 1 - i)):
            r0 = pl.multiple_of(c * q, q)
            s_t = st[d]
            y_ref[pl.ds(r0, q), :] += (jnp.dot(cs_b[pl.ds(r0, q), :], s_t.astype(BF16), preferred_element_type=F32)
                                       * ea_s[d, pl.ds(r0, q), :])
            st[d] = s_t * ea_s[d, pl.ds(r0 + lasts[d], 1), :] + ns_s[d, pl.ds(r0, q), :]
        return carry

    lax.fori_loop(0, nc, inter, 0)

    if want_final:
        for d in range(2):
            s_t = st[d]
            for h in range(2):
                blk = s_t[:, h * 128:(h + 1) * 128].T
                sfin_refs[d][2 * h] = blk[0:64]
                sfin_refs[d][2 * h + 1] = blk[64:128]


def _ssd_scan_call(proj3, dtt, conv_w, conv_b, prm, dskip, s0, want_final, name):
    bsz, seqlen, _ = proj3.shape
    has_init = s0 is not None
    xoff = SSD_D_INNER // 256
    boff = (2 * SSD_D_INNER) // 128
    coff = boff + SSD_GROUPS
    in_specs = [
        pl.BlockSpec((None, seqlen, 256), lambda b, g: (b, 0, xoff + g)),
        pl.BlockSpec((None, seqlen, 128), lambda b, g: (b, 0, boff + g)),
        pl.BlockSpec((None, seqlen, 128), lambda b, g: (b, 0, coff + g)),
        pl.BlockSpec((3, 256), lambda b, g: (0, g)),
        pl.BlockSpec((3, 128), lambda b, g: (0, 16 + g)),
        pl.BlockSpec((3, 128), lambda b, g: (0, 24 + g)),
        pl.BlockSpec((1, 256), lambda b, g: (0, g)),
        pl.BlockSpec((1, 128), lambda b, g: (0, 16 + g)),
        pl.BlockSpec((1, 128), lambda b, g: (0, 24 + g)),
        pl.BlockSpec((None, None, 8, seqlen), lambda b, g: (b, g, 0, 0)),
        pl.BlockSpec((None, 8, 2), lambda b, g: (g, 0, 0)),
        pl.BlockSpec((None, 1, 256), lambda b, g: (g, 0, 0)),
    ]
    args = [proj3, proj3, proj3, conv_w, conv_w, conv_w, conv_b, conv_b, conv_b, dtt, prm, dskip]
    state_spec = pl.BlockSpec((None, 4, SSD_HEADDIM, SSD_STATE), lambda b, g: (b, g, 0, 0))
    if has_init:
        in_specs += [state_spec, state_spec]
        args += [s0[0], s0[1]]
    out_specs = [pl.BlockSpec((None, seqlen, 256), lambda b, g: (b, 0, g))]
    out_shape = [jax.ShapeDtypeStruct((bsz, seqlen, SSD_D_INNER), F32)]
    if want_final:
        out_specs += [state_spec, state_spec]
        out_shape += [jax.ShapeDtypeStruct((bsz, SSD_HEADS, SSD_HEADDIM, SSD_STATE), F32)] * 2
    return pl.pallas_call(
        functools.partial(_ssd_scan_kernel, seqlen=seqlen, has_init=has_init, want_final=want_final),
        grid=(bsz, SSD_GROUPS),
        in_specs=in_specs,
        out_specs=out_specs,
        out_shape=out_shape,
        scratch_shapes=[pltpu.VMEM((seqlen, SSD_STATE), BF16),
                        pltpu.VMEM((2, seqlen, 256), F32),
                        pltpu.VMEM((2, seqlen, 256), F32),
                        pltpu.VMEM((2, SSD_STATE, 256), F32)],
        compiler_params=_params("arbitrary", "arbitrary"),
        name=name,
    )(*args)


def _unit_tri_inverse(a_list, eye, merge_masks):
    ys = [eye - jnp.where(merge_masks[0], a, 0.0) for a in a_list]
    for m in merge_masks[1:]:
        ys_b = [y.astype(BF16) for y in ys]
        xs = [jnp.dot(jnp.where(m, a, 0.0).astype(BF16), y_b, preferred_element_type=F32)
              for a, y_b in zip(a_list, ys_b)]
        ys = [y - jnp.dot(y_b, x.astype(BF16), preferred_element_type=F32)
              for y, y_b, x in zip(ys, ys_b, xs)]
    return ys


def _gdn_scan_kernel(*refs, seqlen, has_init, want_final):
    refs = list(refs)
    q_ref, k_ref, v_ref, cwq, cwk, cwv, smt_ref, prm_ref = refs[:8]
    pos = 8
    if has_init:
        s0_refs = refs[pos:pos + 2]
        pos += 2
    o_ref = refs[pos]
    pos += 1
    if want_final:
        sfin_refs = refs[pos:pos + 2]
        pos += 2
    u_s, wq_s, qkm_s, kdt_s, gl_s, st = refs[pos:pos + 6]

    blk = GDN_BLOCK
    nb = seqlen // blk
    qk_scale = GDN_HEAD ** -0.5

    for d in range(2):
        for r in range(2):
            if has_init:
                st[d, r] = s0_refs[d][r]
            else:
                st[d, r] = jnp.zeros((GDN_HEAD, GDN_HEAD), F32)

    row = lax.broadcasted_iota(jnp.int32, (blk, blk), 0)
    col = lax.broadcasted_iota(jnp.int32, (blk, blk), 1)
    eye = (row == col).astype(F32)
    merge_masks = [((row >> (l + 1)) == (col >> (l + 1))) & ((row >> l) != (col >> l)) for l in range(7)]
    bias = prm_ref[:, 0:1]
    a_neg = -jnp.exp(prm_ref[:, 1:2])
    nt = (((1,), (1,)), ((), ()))

    def prepare_block(c):
        r0 = pl.multiple_of(c * blk, blk)
        wq0 = pl.multiple_of(c * 2 * blk, 2 * blk)
        g0 = pl.multiple_of(c * 8, 8)
        qv = _conv_silu_rows(q_ref, r0, blk, seqlen, cwq, None)
        kv = _conv_silu_rows(k_ref, r0, blk, seqlen, cwk, None)
        qc = qv * (lax.rsqrt(jnp.sum(qv * qv, axis=-1, keepdims=True) + RMS_EPS) * qk_scale)
        kc = kv * lax.rsqrt(jnp.sum(kv * kv, axis=-1, keepdims=True) + RMS_EPS)
        vc = _conv_silu_rows(v_ref, r0, blk, seqlen, cwv, None)
        o_ref[pl.ds(r0, blk), :] = jnp.zeros((blk, 256), F32)
        sm = smt_ref[:, pl.ds(r0, blk)]
        g_t = a_neg * _softplus(sm + bias)
        beta_t = _sigmoid(sm)
        qc_b = qc.astype(BF16)
        kc_b = kc.astype(BF16)
        kk = lax.dot_general(kc_b, kc_b, nt, preferred_element_type=F32)
        qk = lax.dot_general(qc_b, kc_b, nt, preferred_element_type=F32)
        chains = []
        for d in range(2):
            if d == 0:
                cum_m = (row <= col).astype(F32)
                strict = row > col
                incl = row >= col
                last = blk - 1
            else:
                cum_m = (row >= col).astype(F32)
                strict = row < col
                incl = row <= col
                last = 0
            gc_t = jnp.dot(g_t, cum_m, precision=HI, preferred_element_type=F32)
            tot = gc_t[:, last:last + 1]
            eg_t = jnp.exp(gc_t)
            kd_t = jnp.exp(tot - gc_t)
            gl_s[d, pl.ds(g0, 8), :] = jnp.broadcast_to(jnp.exp(tot), (8, blk))
            cols = _transpose_rows([gc_t, eg_t, kd_t, beta_t])
            for r in range(2):
                k = 2 * d + r
                e = jnp.exp(cols[:, k:k + 1] - gc_t[k:k + 1, :])
                bcol = cols[:, 28 + k:29 + k]
                egc = cols[:, 8 + k:9 + k]
                vr = vc[:, r * 128:(r + 1) * 128]
                wq_s[d, r, pl.ds(wq0 + blk, blk), :] = (qc * egc).astype(BF16)
                qkm_s[d, r, pl.ds(r0, blk), :] = (qk * jnp.where(incl, e, 0.0)).astype(BF16)
                kdt_s[d, r, pl.ds(r0, blk), :] = (kc * cols[:, 16 + k:17 + k]).T.astype(BF16)
                chains.append(dict(
                    d=d, r=r, r0=r0, wq0=wq0,
                    a=bcol * kk * jnp.where(strict, e, 0.0),
                    rhs=jnp.concatenate([vr * bcol, kc * (bcol * egc)], axis=1).astype(BF16)))
        return chains

    per_iter = 2 if nb % 2 == 0 else 1

    def phase1(i, carry):
        chains = []
        for j in range(per_iter):
            chains += prepare_block(i * per_iter + j)
        ys = _unit_tri_inverse([ch["a"] for ch in chains], eye, merge_masks)
        sols = [jnp.dot(y.astype(BF16), ch["rhs"], preferred_element_type=F32) for y, ch in zip(ys, chains)]
        for sol, ch in zip(sols, chains):
            u_s[ch["d"], ch["r"], pl.ds(ch["r0"], blk), :] = sol[:, :GDN_HEAD]
            wq_s[ch["d"], ch["r"], pl.ds(ch["wq0"], blk), :] = sol[:, GDN_HEAD:].astype(BF16)
        return carry

    lax.fori_loop(0, nb // per_iter, phase1, 0)

    def phase2(i, carry):
        jobs = []
        for d, c in ((0, i), (1, nb - 1 - i)):
            r0 = pl.multiple_of(c * blk, blk)
            wq0 = pl.multiple_of(c * 2 * blk, 2 * blk)
            gl = gl_s[d, pl.ds(pl.multiple_of(c * 8, 8), 8), :]
            for r in range(2):
                jobs.append((d, r, r0, wq0, gl[2 * d + r:2 * d + r + 1, :]))
        ss = [st[d, r] for d, r, _, _, _ in jobs]
        wss = [jnp.dot(wq_s[d, r, pl.ds(wq0, 2 * blk), :], s.astype(BF16), preferred_element_type=F32)
               for (d, r, _, wq0, _), s in zip(jobs, ss)]
        vns = [(u_s[d, r, pl.ds(r0, blk), :] - ws[:blk]).astype(BF16)
               for (d, r, r0, _, _), ws in zip(jobs, wss)]
        for (d, r, r0, _, gl), s, vn in zip(jobs, ss, vns):
            st[d, r] = gl * s + jnp.dot(kdt_s[d, r, pl.ds(r0, blk), :], vn, preferred_element_type=F32)
        outs = [ws[blk:] + jnp.dot(qkm_s[d, r, pl.ds(r0, blk), :], vn, preferred_element_type=F32)
                for (d, r, r0, _, _), ws, vn in zip(jobs, wss, vns)]
        for half in range(2):
            r0 = jobs[2 * half][2]
            o_ref[pl.ds(r0, blk), :] += jnp.concatenate(outs[2 * half:2 * half + 2], axis=1)
        return carry

    lax.fori_loop(0, nb, phase2, 0)

    if want_final:
        for d in range(2):
            for r in range(2):
                sfin_refs[d][r] = st[d, r]


def _gdn_scan_call(proj3, smt, conv_w, prm, s0, want_final, name):
    bsz, seqlen, _ = proj3.shape
    has_init = s0 is not None
    in_specs = [
        pl.BlockSpec((None, seqlen, 128), lambda b, g: (b, 0, g)),
        pl.BlockSpec((None, seqlen, 128), lambda b, g: (b, 0, GDN_K_HEADS + g)),
        pl.BlockSpec((None, seqlen, 256), lambda b, g: (b, 0, GDN_K_HEADS + g)),
        pl.BlockSpec((3, 128), lambda b, g: (0, g)),
        pl.BlockSpec((3, 128), lambda b, g: (0, GDN_K_HEADS + g)),
        pl.BlockSpec((3, 256), lambda b, g: (0, GDN_K_HEADS + g)),
        pl.BlockSpec((None, None, 8, seqlen), lambda b, g: (b, g, 0, 0)),
        pl.BlockSpec((None, 8, 2), lambda b, g: (g, 0, 0)),
    ]
    args = [proj3, proj3, proj3, conv_w, conv_w, conv_w, smt, prm]
    state_spec = pl.BlockSpec((None, 2, GDN_HEAD, GDN_HEAD), lambda b, g: (b, g, 0, 0))
    if has_init:
        in_specs += [state_spec, state_spec]
        args += [s0[0], s0[1]]
    out_specs = [pl.BlockSpec((None, seqlen, 256), lambda b, g: (b, 0, g))]
    out_shape = [jax.ShapeDtypeStruct((bsz, seqlen, GDN_V), F32)]
    if want_final:
        out_specs += [state_spec, state_spec]
        out_shape += [jax.ShapeDtypeStruct((bsz, GDN_V_HEADS, GDN_HEAD, GDN_HEAD), F32)] * 2
    return pl.pallas_call(
        functools.partial(_gdn_scan_kernel, seqlen=seqlen, has_init=has_init, want_final=want_final),
        grid=(bsz, GDN_K_HEADS),
        in_specs=in_specs,
        out_specs=out_specs,
        out_shape=out_shape,
        scratch_shapes=[pltpu.VMEM((2, 2, seqlen, GDN_HEAD), F32),
                        pltpu.VMEM((2, 2, 2 * seqlen, GDN_HEAD), BF16),
                        pltpu.VMEM((2, 2, seqlen, GDN_BLOCK), BF16),
                        pltpu.VMEM((2, 2, seqlen, GDN_BLOCK), BF16),
                        pltpu.VMEM((2, 8 * (seqlen // GDN_BLOCK), GDN_BLOCK), F32),
                        pltpu.VMEM((2, 2, GDN_HEAD, GDN_HEAD), F32)],
        compiler_params=_params("arbitrary", "arbitrary"),
        name=name,
    )(*args)


def _out_kernel(*refs, mode):
    if mode == "plain":
        a_ref, w_ref, x_ref, gt_ref, lg_ref, lb_ref, o_ref = refs
        a = a_ref[...]
    else:
        y_ref, z_ref, nw_ref, w_ref, x_ref, gt_ref, lg_ref, lb_ref, o_ref = refs
        if mode == "ssd":
            u = y_ref[...] * _silu(z_ref[...])
            ms = jnp.mean(u * u, axis=-1, keepdims=True)
            a = (u * lax.rsqrt(ms + RMS_EPS) * nw_ref[...]).astype(BF16)
        else:
            parts = []
            for h in range(GDN_V_HEADS):
                sl = slice(h * GDN_HEAD, (h + 1) * GDN_HEAD)
                oh = y_ref[:, sl]
                ms = jnp.mean(oh * oh, axis=-1, keepdims=True)
                parts.append((oh * lax.rsqrt(ms + RMS_EPS) * nw_ref[...] * _silu(z_ref[:, sl])).astype(BF16))
            a = jnp.concatenate(parts, axis=1)
    o = jnp.dot(a, w_ref[...], preferred_element_type=F32)
    r = DEEPNORM_ALPHA * x_ref[...] + gt_ref[...] * o
    mu = jnp.mean(r, axis=-1, keepdims=True)
    rc = r - mu
    var = jnp.mean(rc * rc, axis=-1, keepdims=True)
    o_ref[...] = rc * lax.rsqrt(var + LN_EPS) * lg_ref[...] + lb_ref[...]


def _out_call(mode, act, z_src, z_blk, norm_w, w, x, gate, ln_g, ln_b, rows_per_mod, tm, name):
    m, kdim = act.shape
    row_spec = pl.BlockSpec((tm, D_MODEL), lambda i: (i, 0))
    vec_spec = pl.BlockSpec((1, D_MODEL), lambda i: (0, 0))
    in_specs = [pl.BlockSpec((tm, kdim), lambda i: (i, 0))]
    args = [act]
    if mode != "plain":
        in_specs += [pl.BlockSpec((tm, kdim), lambda i: (i, z_blk)),
                     pl.BlockSpec((1, norm_w.shape[1]), lambda i: (0, 0))]
        args += [z_src, norm_w]
    in_specs += [pl.BlockSpec((kdim, D_MODEL), lambda i: (0, 0)), row_spec,
                 pl.BlockSpec((None, 1, D_MODEL), lambda i: ((i * tm) // rows_per_mod, 0, 0)),
                 vec_spec, vec_spec]
    args += [w, x, gate, ln_g, ln_b]
    return pl.pallas_call(
        functools.partial(_out_kernel, mode=mode),
        grid=(m // tm,),
        in_specs=in_specs,
        out_specs=row_spec,
        out_shape=jax.ShapeDtypeStruct((m, D_MODEL), F32),
        compiler_params=_params("arbitrary"),
        name=name,
    )(*args)


def _convglu_kernel(*refs, tm, period, row_taps, tiles_per_seq):
    if row_taps:
        a_ref, ap_ref, an_ref, v_ref, w_ref, b_ref, o_ref, ext_ref = refs
        i = pl.program_id(0)
        t = i % tiles_per_seq
        ext_ref[0:128, :] = ap_ref[...] * (t > 0).astype(F32)
        ext_ref[128:128 + tm, :] = a_ref[...]
        ext_ref[128 + tm:, :] = an_ref[...] * (t < tiles_per_seq - 1).astype(F32)
        ext = ext_ref[...]
        base = 128
        taps = (0, 1, 2)
    else:
        a_ref, v_ref, w_ref, b_ref, o_ref = refs
        ext = a_ref[...]
        base = 0
        taps = (1,)
    n = ext.shape[0]
    colpos = lax.broadcasted_iota(jnp.int32, ext.shape, 0) % period
    em1 = jnp.where(colpos == 0, 0.0, pltpu.roll(ext, 1, 0))
    ep1 = jnp.where(colpos == period - 1, 0.0, pltpu.roll(ext, n - 1, 0))
    acc = b_ref[...]
    for dr in taps:
        off = base + (dr - 1) * GRID_W
        acc = (acc + w_ref[3 * dr:3 * dr + 1, :] * em1[off:off + tm]
               + w_ref[3 * dr + 1:3 * dr + 2, :] * ext[off:off + tm]
               + w_ref[3 * dr + 2:3 * dr + 3, :] * ep1[off:off + tm])
    o_ref[...] = (_silu(acc) * v_ref[...]).astype(BF16)


def _convglu_call(up, conv_w9, conv_b, seqlen, is_grid, tm, tc, name):
    m = up.shape[0]
    nct = D_FF // tc
    if is_grid:
        hb = tm // 128
        nhb = m // 128
        tiles_per_seq = seqlen // tm
        in_specs = [
            pl.BlockSpec((tm, tc), lambda i, j: (i, j)),
            pl.BlockSpec((128, tc), lambda i, j: (jnp.maximum(i * hb - 1, 0), j)),
            pl.BlockSpec((128, tc), lambda i, j: (jnp.minimum((i + 1) * hb, nhb - 1), j)),
            pl.BlockSpec((tm, tc), lambda i, j: (i, j + nct)),
        ]
        args = [up, up, up, up]
        scratch = [pltpu.VMEM((tm + 256, tc), F32)]
        period = GRID_W
    else:
        tiles_per_seq = 1
        in_specs = [pl.BlockSpec((tm, tc), lambda i, j: (i, j)),
                    pl.BlockSpec((tm, tc), lambda i, j: (i, j + nct))]
        args = [up, up]
        scratch = []
        period = seqlen
    in_specs += [pl.BlockSpec((9, tc), lambda i, j: (0, j)), pl.BlockSpec((1, tc), lambda i, j: (0, j))]
    args += [conv_w9, conv_b]
    return pl.pallas_call(
        functools.partial(_convglu_kernel, tm=tm, period=period, row_taps=is_grid, tiles_per_seq=tiles_per_seq),
        grid=(m // tm, nct),
        in_specs=in_specs,
        out_specs=pl.BlockSpec((tm, tc), lambda i, j: (i, j)),
        out_shape=jax.ShapeDtypeStruct((m, D_FF), BF16),
        scratch_shapes=scratch,
        compiler_params=_params("arbitrary", "arbitrary"),
        name=name,
    )(*args)


def _group_rows(t, bsz, seqlen, lead):
    n_lead = int(math.prod(lead))
    t = t.reshape(bsz, seqlen, n_lead, 8, -1)
    r = t.shape[-1]
    t = jnp.transpose(t, (0, 3, 2, 4, 1))
    return t.reshape(bsz, 8, n_lead * r, seqlen)


def _group_params(p):
    r = p.shape[1] // 8
    return jnp.transpose(p.reshape(2, 8, r), (1, 0, 2)).reshape(8, 2 * r)


def _run_tokens(x, seqlen, mods, states, want_final, w, tag):
    bsz = x.shape[0]
    m = bsz * seqlen
    nb = mods.shape[1]
    rows_per_mod = m // nb
    xt = x.reshape(m, D_MODEL)
    finals = []
    is_grid = tag == "s"
    for i in range(DEPTH):
        def mod(k):
            return mods[i, :, k].reshape(nb, 1, D_MODEL)
        lw = w[i]
        proj, small = _proj_call(xt, mod(1), mod(0), lw["w_in"], lw["w_in_small"], rows_per_mod,
                                 tm=1024, tn=1536, name=f"inproj{i}{tag}")
        proj3 = proj.reshape(bsz, seqlen, 6144)
        if i % 2 == 0:
            dtt = _group_rows(small[:, :2 * SSD_HEADS], bsz, seqlen, (2,))
            res = _ssd_scan_call(proj3, dtt, lw["conv_w"], lw["conv_b"], lw["prm"], lw["dskip"],
                                 states[i], want_final, name=f"ssdscan{tag}")
            mode, z_blk = "ssd", 0
        else:
            smt = _group_rows(small[:, :4 * GDN_V_HEADS], bsz, seqlen, (2, 2))
            res = _gdn_scan_call(proj3, smt, lw["conv_w"], lw["prm"], states[i], want_final,
                                 name=f"gdnscan{tag}")
            mode, z_blk = "gdn", 2
        y = res[0].reshape(m, 2048)
        if want_final:
            finals.append((res[1], res[2]))
        xt = _out_call(mode, y, proj, z_blk, lw["norm_w"], lw["w_out"], xt, mod(2), lw["ln_g0"], lw["ln_b0"],
                       rows_per_mod, tm=256, name=f"outproj{i}{tag}")
        up = _proj_call(xt, mod(4), mod(3), lw["w_up"], None, rows_per_mod,
                        tm=1024, tn=1408, name=f"ffnup{i}{tag}")[0]
        act = _convglu_call(up, lw["ffn_conv_w"], lw["ffn_conv_b"], seqlen, is_grid,
                            tm=1024, tc=256, name=f"convglu{i}{tag}")
        xt = _out_call("plain", act, None, 0, None, lw["w_down"], xt, mod(5), lw["ln_g1"], lw["ln_b1"],
                       rows_per_mod, tm=256, name=f"ffndown{i}{tag}")
    return xt.reshape(bsz, seqlen, D_MODEL), finals


def kernel(x_prompt, x_sample, state_ssd_fwd, state_ssd_bwd, state_gdn_fwd, state_gdn_bwd, c, c_ctx, w_ada, b_ada, ln_g, ln_b, ssd_w_in, ssd_conv_w, ssd_conv_b, ssd_dt_bias, ssd_a_log, ssd_d, ssd_norm_w, ssd_w_out, gdn_w_in, gdn_conv_w, gdn_dt_bias, gdn_a_log, gdn_norm_w, gdn_w_out, ffn_w_up, ffn_conv_w, ffn_conv_b, ffn_w_down):
    n_dec = c.shape[0]
    cond = jnp.concatenate([c_ctx[None, :], c, jnp.zeros((16 - 1 - n_dec, D_MODEL), F32)], axis=0)
    ada = _ada_call(cond, w_ada, b_ada).reshape(DEPTH, 16, 6, D_MODEL)
    mods_p = ada[:, 0:1]
    mods_s = ada[:, 1:1 + n_dec]

    def small_cols(wm):
        return jnp.pad(wm, ((0, 0), (0, 128 - wm.shape[1]))).astype(BF16)

    weights = []
    for i in range(DEPTH):
        j = i // 2
        lw = {
            "ln_g0": ln_g[i, 0:1], "ln_b0": ln_b[i, 0:1], "ln_g1": ln_g[i, 1:2], "ln_b1": ln_b[i, 1:2],
            "w_up": ffn_w_up[i].astype(BF16), "w_down": ffn_w_down[i].astype(BF16),
            "ffn_conv_w": ffn_conv_w[i].reshape(9, D_FF), "ffn_conv_b": ffn_conv_b[i].reshape(1, D_FF),
        }
        if i % 2 == 0:
            lw["w_in"] = ssd_w_in[j][:, :6144].astype(BF16)
            lw["w_in_small"] = small_cols(ssd_w_in[j][:, 6144:])
            lw["conv_w"] = ssd_conv_w[j]
            lw["conv_b"] = ssd_conv_b[j].reshape(1, SSD_XBC)
            lw["prm"] = jnp.stack([_group_params(ssd_dt_bias[j]), _group_params(ssd_a_log[j])], axis=-1)
            lw["dskip"] = jnp.repeat(ssd_d[j], SSD_HEADDIM).reshape(SSD_GROUPS, 1, 256)
            lw["norm_w"] = ssd_norm_w[j].reshape(1, SSD_D_INNER)
            lw["w_out"] = ssd_w_out[j].astype(BF16)
        else:
            lw["w_in"] = gdn_w_in[j][:, :6144].astype(BF16)
            lw["w_in_small"] = small_cols(gdn_w_in[j][:, 6144:])
            lw["conv_w"] = gdn_conv_w[j]
            prm = jnp.stack([_group_params(gdn_dt_bias[j]), _group_params(gdn_a_log[j])], axis=-1)
            lw["prm"] = jnp.pad(prm, ((0, 0), (0, 4), (0, 0)))
            lw["norm_w"] = gdn_norm_w[j].reshape(1, GDN_HEAD)
            lw["w_out"] = gdn_w_out[j].astype(BF16)
        weights.append(lw)

    states_s = [(state_ssd_fwd[:, 0], state_ssd_bwd[:, 0]), (state_gdn_fwd[:, 0], state_gdn_bwd[:, 0])]
    y_prompt, fin = _run_tokens(x_prompt, x_prompt.shape[1], mods_p, [None, None], True, weights, "p")
    y_sample, _ = _run_tokens(x_sample, x_sample.shape[1], mods_s, states_s, False, weights, "s")
    return (y_prompt, y_sample,
            fin[0][0][:, None], fin[0][1][:, None], fin[1][0][:, None], fin[1][1][:, None])
```

```python
import functools
import math

import jax
import jax.numpy as jnp
from jax import lax
from jax.experimental import pallas as pl
from jax.experimental.pallas import tpu as pltpu

F32 = jnp.float32
BF16 = jnp.bfloat16
HI = lax.Precision.HIGHEST

D_MODEL = 1024
DEPTH = 2
GRID_W = 64

SSD_D_INNER = 2048
SSD_HEADDIM = 64
SSD_HEADS = 32
SSD_GROUPS = 8
SSD_STATE = 128
SSD_CHUNK = 128
SSD_XBC = 4096

GDN_K_HEADS = 8
GDN_V_HEADS = 16
GDN_HEAD = 128
GDN_QK = 1024
GDN_V = 2048
GDN_QKV = 4096
GDN_BLOCK = 128

D_FF = 2816
FFN_TC = 256
HALO = 128

DEEPNORM_ALPHA = (2.0 * DEPTH) ** 0.25
LN_EPS = 1e-5
RMS_EPS = 1e-6

VMEM_LIMIT = 56 * 1024 * 1024
BF16_ROWS = 16


def _silu(x):
    h = 0.5 * x
    return h + h * jnp.tanh(h)


def _sigmoid(x):
    return 0.5 + 0.5 * jnp.tanh(0.5 * x)


def _softplus(x):
    return jnp.maximum(x, 0.0) + jnp.log(1.0 + jnp.exp(-jnp.abs(x)))


def _params(*sem):
    return pltpu.CompilerParams(dimension_semantics=sem, vmem_limit_bytes=VMEM_LIMIT)


def _ada_kernel(c_ref, w_ref, b_ref, o_ref):
    h = _silu(c_ref[...]).astype(BF16)
    o_ref[...] = jnp.dot(h, w_ref[...].astype(BF16), preferred_element_type=F32) + b_ref[...]


def _ada_call(cond, w_ada, b_ada):
    rows = cond.shape[0]
    tn = 1536
    return pl.pallas_call(
        _ada_kernel,
        grid=(DEPTH, 6 * D_MODEL // tn),
        in_specs=[
            pl.BlockSpec((rows, D_MODEL), lambda l, j: (0, 0)),
            pl.BlockSpec((None, D_MODEL, tn), lambda l, j: (l, 0, j)),
            pl.BlockSpec((None, 1, tn), lambda l, j: (l, 0, j)),
        ],
        out_specs=pl.BlockSpec((None, rows, tn), lambda l, j: (l, 0, j)),
        out_shape=jax.ShapeDtypeStruct((DEPTH, rows, 6 * D_MODEL), F32),
        compiler_params=_params("arbitrary", "arbitrary"),
        name="adaln",
    )(cond, w_ada, b_ada.reshape(DEPTH, 1, 6 * D_MODEL))


def _proj_kernel(*refs, has_small):
    if has_small:
        x_ref, sc_ref, sh_ref, w_ref, w2_ref, o_ref, o2_ref, h_ref = refs
    else:
        x_ref, sc_ref, sh_ref, w_ref, o_ref, h_ref = refs

    @pl.when(pl.program_id(1) == 0)
    def _():
        h = x_ref[...] * (1.0 + sc_ref[...]) + sh_ref[...]
        h_ref[...] = h.astype(BF16)
        if has_small:
            o2_ref[...] = jnp.dot(h_ref[...], w2_ref[...], preferred_element_type=F32)

    o_ref[...] = jnp.dot(h_ref[...], w_ref[...], preferred_element_type=F32).astype(o_ref.dtype)


def _proj_call(x, scale, shift, w, w_small, rows_per_mod, tm, tn, name):
    m = x.shape[0]
    n = w.shape[1]
    has_small = w_small is not None
    mod_spec = pl.BlockSpec((None, 1, D_MODEL), lambda i, j: ((i * tm) // rows_per_mod, 0, 0))
    in_specs = [pl.BlockSpec((tm, D_MODEL), lambda i, j: (i, 0)), mod_spec, mod_spec,
                pl.BlockSpec((D_MODEL, tn), lambda i, j: (0, j))]
    out_specs = [pl.BlockSpec((tm, tn), lambda i, j: (i, j))]
    out_shape = [jax.ShapeDtypeStruct((m, n), BF16)]
    args = [x, scale, shift, w]
    if has_small:
        in_specs.append(pl.BlockSpec((D_MODEL, 128), lambda i, j: (0, 0)))
        out_specs.append(pl.BlockSpec((tm, 128), lambda i, j: (i, 0)))
        out_shape.append(jax.ShapeDtypeStruct((m, 128), F32))
        args.append(w_small)
    return pl.pallas_call(
        functools.partial(_proj_kernel, has_small=has_small),
        grid=(m // tm, n // tn),
        in_specs=in_specs,
        out_specs=out_specs,
        out_shape=out_shape,
        scratch_shapes=[pltpu.VMEM((tm, D_MODEL), BF16)],
        compiler_params=_params("arbitrary", "arbitrary"),
        name=name,
    )(*args)


def _conv_silu_rows(src_ref, r0, n_rows, seqlen, w_ref, bias):
    cur = src_ref[pl.ds(r0, n_rows), :].astype(F32)
    has_prev = (r0 > 0).astype(F32)
    has_next = (r0 + n_rows < seqlen).astype(F32)
    g_prev = pl.multiple_of(jnp.maximum(r0 - BF16_ROWS, 0), BF16_ROWS)
    g_next = pl.multiple_of(jnp.minimum(r0 + n_rows, seqlen - BF16_ROWS), BF16_ROWS)
    prev_row = src_ref[pl.ds(g_prev, BF16_ROWS), :].astype(F32)[BF16_ROWS - 1:BF16_ROWS] * has_prev
    next_row = src_ref[pl.ds(g_next, BF16_ROWS), :].astype(F32)[0:1] * has_next
    rid = lax.broadcasted_iota(jnp.int32, cur.shape, 0)
    xm1 = jnp.where(rid == 0, prev_row, pltpu.roll(cur, 1, 0))
    xp1 = jnp.where(rid == n_rows - 1, next_row, pltpu.roll(cur, n_rows - 1, 0))
    out = w_ref[0:1, :] * xm1 + w_ref[1:2, :] * cur + w_ref[2:3, :] * xp1
    if bias is not None:
        out = out + bias
    return _silu(out)


def _transpose_rows(rows_list):
    pad = 128 - 8 * len(rows_list)
    stack = jnp.concatenate(rows_list + [jnp.zeros((pad, 128), F32)], axis=0)
    return stack.T


def _ssd_scan_kernel(*refs, seqlen, has_init, want_final):
    refs = list(refs)
    x_ref, b_ref, c_ref, cwx, cwb, cwc, cbx, cbb, cbc, dtt_ref, prm_ref, dsk_ref = refs[:12]
    pos = 12
    if has_init:
        s0_refs = refs[pos:pos + 2]
        pos += 2
    yout_ref = refs[pos]
    pos += 1
    if want_final:
        sfin_refs = refs[pos:pos + 2]
        pos += 2
    cs_b, ea_s, ns_s, st, y_ref = refs[pos:pos + 5]

    nc = seqlen // SSD_CHUNK
    q = SSD_CHUNK

    for d in range(2):
        if has_init:
            halves = []
            for h in range(2):
                blk = jnp.concatenate([s0_refs[d][2 * h], s0_refs[d][2 * h + 1]], axis=0)
                halves.append(blk.T)
            st[d] = jnp.concatenate(halves, axis=1)
        else:
            st[d] = jnp.zeros((SSD_STATE, 256), F32)

    row = lax.broadcasted_iota(jnp.int32, (q, q), 0)
    col = lax.broadcasted_iota(jnp.int32, (q, q), 1)
    lane_head = lax.broadcasted_iota(jnp.int32, (q, 256), 1) // SSD_HEADDIM
    bias = prm_ref[:, 0:1]
    a_neg = -jnp.exp(prm_ref[:, 1:2])
    dskip = dsk_ref[...]

    cum_ms = ((row <= col).astype(F32), (row >= col).astype(F32))
    keeps = (row >= col, row <= col)
    lasts = (q - 1, 0)

    def intra(i, carry):
        blocks = []
        for j in range(unroll_a):
            r0 = pl.multiple_of((i * unroll_a + j) * q, q)
            xc = _conv_silu_rows(x_ref, r0, q, seqlen, cwx, cbx[...])
            bc = _conv_silu_rows(b_ref, r0, q, seqlen, cwb, cbb[...])
            cc = _conv_silu_rows(c_ref, r0, q, seqlen, cwc, cbc[...])
            cc_b = cc.astype(BF16)
            cs_b[pl.ds(r0, q), :] = cc_b
            bc_t = bc.T
            dt_t = _softplus(dtt_ref[:, pl.ds(r0, q)] + bias)
            x_heads = jnp.concatenate([jnp.where(lane_head == r, xc, 0.0).astype(BF16) for r in range(4)], axis=0)
            blocks.append(dict(r0=r0, xc=xc, bc_t=bc_t, dt_t=dt_t, a_t=dt_t * a_neg, x_heads=x_heads,
                               scores=jnp.dot(cc_b, bc_t.astype(BF16), preferred_element_type=F32)))
        units = [(blk, d) for blk in blocks for d in range(2)]
        acums = [jnp.dot(blk["a_t"], cum_ms[d], precision=HI, preferred_element_type=F32) for blk, d in units]
        cols = _transpose_rows(acums)
        dtdes = [blk["dt_t"] * jnp.exp(ac[:, lasts[d]:lasts[d] + 1] - ac) for (blk, d), ac in zip(units, acums)]
        for (blk, d), ac in zip(units, acums):
            ea_t = jnp.exp(ac)
            ea_rows = jnp.concatenate([jnp.broadcast_to(ea_t[4 * d + r:4 * d + r + 1, :], (SSD_HEADDIM, q))
                                       for r in range(4)], axis=0)
            ea_s[d, pl.ds(blk["r0"], q), :] = jnp.concatenate([ea_rows[0:128].T, ea_rows[128:256].T], axis=1)
        lhss = []
        for u, ((blk, d), ac, dtde) in enumerate(zip(units, acums, dtdes)):
            parts = []
            for r in range(4):
                k = 4 * d + r
                decay = jnp.where(keeps[d], jnp.exp(cols[:, 8 * u + k:8 * u + k + 1] - ac[k:k + 1, :]), 0.0)
                m = decay * blk["scores"] * blk["dt_t"][k:k + 1, :]
                bt = blk["bc_t"] * dtde[k:k + 1, :]
                parts.append(jnp.concatenate([m, bt], axis=0).astype(BF16))
            lhss.append(jnp.concatenate(parts, axis=1))
        ress = [jnp.dot(lhs, blk["x_heads"], preferred_element_type=F32) for lhs, (blk, d) in zip(lhss, units)]
        for res, (blk, d) in zip(ress, units):
            ns_s[d, pl.ds(blk["r0"], q), :] = res[q:]
        for j, blk in enumerate(blocks):
            y_ref[pl.ds(blk["r0"], q), :] = blk["xc"] * dskip + ress[2 * j][:q] + ress[2 * j + 1][:q]
        return carry

    unroll_a = 2 if nc % 2 == 0 else 1
    lax.fori_loop(0, nc // unroll_a, intra, 0)

    def inter(i, carry):
        for d, c in ((0, i), (1, nc - 1 - i)):
            r0 = pl.multiple_of(c * q, q)
            s_t = st[d]
            y_ref[pl.ds(r0, q), :] += (jnp.dot(cs_b[pl.ds(r0, q), :], s_t.astype(BF16), preferred_element_type=F32)
                                       * ea_s[d, pl.ds(r0, q), :])
            st[d] = s_t * ea_s[d, pl.ds(r0 + lasts[d], 1), :] + ns_s[d, pl.ds(r0, q), :]
        return carry

    lax.fori_loop(0, nc, inter, 0)

    def emit(c, carry):
        r0 = pl.multiple_of(c * q, q)
        yout_ref[pl.ds(r0, q), :] = y_ref[pl.ds(r0, q), :].astype(yout_ref.dtype)
        return carry

    lax.fori_loop(0, nc, emit, 0)

    if want_final:
        for d in range(2):
            s_t = st[d]
            for h in range(2):
                blk = s_t[:, h * 128:(h + 1) * 128].T
                sfin_refs[d][2 * h] = blk[0:64]
                sfin_refs[d][2 * h + 1] = blk[64:128]


def _ssd_scan_call(proj3, dtt, conv_w, conv_b, prm, dskip, s0, want_final, name):
    bsz, seqlen, _ = proj3.shape
    has_init = s0 is not None
    xoff = SSD_D_INNER // 256
    boff = (2 * SSD_D_INNER) // 128
    coff = boff + SSD_GROUPS
    in_specs = [
        pl.BlockSpec((None, seqlen, 256), lambda b, g: (b, 0, xoff + g)),
        pl.BlockSpec((None, seqlen, 128), lambda b, g: (b, 0, boff + g)),
        pl.BlockSpec((None, seqlen, 128), lambda b, g: (b, 0, coff + g)),
        pl.BlockSpec((3, 256), lambda b, g: (0, g)),
        pl.BlockSpec((3, 128), lambda b, g: (0, 16 + g)),
        pl.BlockSpec((3, 128), lambda b, g: (0, 24 + g)),
        pl.BlockSpec((1, 256), lambda b, g: (0, g)),
        pl.BlockSpec((1, 128), lambda b, g: (0, 16 + g)),
        pl.BlockSpec((1, 128), lambda b, g: (0, 24 + g)),
        pl.BlockSpec((None, None, 8, seqlen), lambda b, g: (b, g, 0, 0)),
        pl.BlockSpec((None, 8, 2), lambda b, g: (g, 0, 0)),
        pl.BlockSpec((None, 1, 256), lambda b, g: (g, 0, 0)),
    ]
    args = [proj3, proj3, proj3, conv_w, conv_w, conv_w, conv_b, conv_b, conv_b, dtt, prm, dskip]
    state_spec = pl.BlockSpec((None, 4, SSD_HEADDIM, SSD_STATE), lambda b, g: (b, g, 0, 0))
    if has_init:
        in_specs += [state_spec, state_spec]
        args += [s0[0], s0[1]]
    out_specs = [pl.BlockSpec((None, seqlen, 256), lambda b, g: (b, 0, g))]
    out_shape = [jax.ShapeDtypeStruct((bsz, seqlen, SSD_D_INNER), BF16)]
    if want_final:
        out_specs += [state_spec, state_spec]
        out_shape += [jax.ShapeDtypeStruct((bsz, SSD_HEADS, SSD_HEADDIM, SSD_STATE), F32)] * 2
    return pl.pallas_call(
        functools.partial(_ssd_scan_kernel, seqlen=seqlen, has_init=has_init, want_final=want_final),
        grid=(bsz, SSD_GROUPS),
        in_specs=in_specs,
        out_specs=out_specs,
        out_shape=out_shape,
        scratch_shapes=[pltpu.VMEM((seqlen, SSD_STATE), BF16),
                        pltpu.VMEM((2, seqlen, 256), F32),
                        pltpu.VMEM((2, seqlen, 256), F32),
                        pltpu.VMEM((2, SSD_STATE, 256), F32),
                        pltpu.VMEM((seqlen, 256), F32)],
        compiler_params=_params("arbitrary", "arbitrary"),
        name=name,
    )(*args)


def _unit_tri_solve(a_list, rhs_list, merge_masks):
    xs = a_list
    for m in merge_masks:
        xs = [x - jnp.dot(x.astype(BF16), jnp.where(m, x, 0.0).astype(BF16), preferred_element_type=F32)
              for x in xs]
    return [rhs - jnp.dot(x.astype(BF16), rhs.astype(BF16), preferred_element_type=F32)
            for x, rhs in zip(xs, rhs_list)]


def _gdn_scan_kernel(*refs, seqlen, has_init, want_final):
    refs = list(refs)
    q_ref, k_ref, v_ref, cwq, cwk, cwv, smt_ref, prm_ref = refs[:8]
    pos = 8
    if has_init:
        s0_refs = refs[pos:pos + 2]
        pos += 2
    oout_ref = refs[pos]
    pos += 1
    if want_final:
        sfin_refs = refs[pos:pos + 2]
        pos += 2
    u_s, wq_s, qkm_s, kdt_s, gl_s, st, o_ref = refs[pos:pos + 7]

    blk = GDN_BLOCK
    nb = seqlen // blk
    qk_scale = GDN_HEAD ** -0.5

    for d in range(2):
        for r in range(2):
            if has_init:
                st[d, r] = s0_refs[d][r]
            else:
                st[d, r] = jnp.zeros((GDN_HEAD, GDN_HEAD), F32)

    row = lax.broadcasted_iota(jnp.int32, (blk, blk), 0)
    col = lax.broadcasted_iota(jnp.int32, (blk, blk), 1)
    merge_masks = [((row >> (l + 1)) == (col >> (l + 1))) & ((row >> l) != (col >> l)) for l in range(7)]
    bias = prm_ref[:, 0:1]
    a_neg = -jnp.exp(prm_ref[:, 1:2])
    nt = (((1,), (1,)), ((), ()))

    cum_ms = ((row <= col).astype(F32), (row >= col).astype(F32))
    stricts = (row > col, row < col)
    incls = (row >= col, row <= col)
    lasts = (blk - 1, 0)

    def load_block(c):
        r0 = pl.multiple_of(c * blk, blk)
        qv = _conv_silu_rows(q_ref, r0, blk, seqlen, cwq, None)
        kv = _conv_silu_rows(k_ref, r0, blk, seqlen, cwk, None)
        qc = qv * (lax.rsqrt(jnp.sum(qv * qv, axis=-1, keepdims=True) + RMS_EPS) * qk_scale)
        kc = kv * lax.rsqrt(jnp.sum(kv * kv, axis=-1, keepdims=True) + RMS_EPS)
        vc = _conv_silu_rows(v_ref, r0, blk, seqlen, cwv, None)
        o_ref[pl.ds(r0, blk), :] = jnp.zeros((blk, 256), F32)
        sm = smt_ref[:, pl.ds(r0, blk)]
        return dict(r0=r0, wq0=pl.multiple_of(c * 2 * blk, 2 * blk), g0=pl.multiple_of(c * 8, 8),
                    qc=qc, kc=kc, vc=vc, g_t=a_neg * _softplus(sm + bias), beta_t=_sigmoid(sm))

    def prepare(blocks):
        for b in blocks:
            kc_b = b["kc"].astype(BF16)
            b["kk"] = lax.dot_general(kc_b, kc_b, nt, preferred_element_type=F32)
            b["qk"] = lax.dot_general(b["qc"].astype(BF16), kc_b, nt, preferred_element_type=F32)
        units = [(b, d) for b in blocks for d in range(2)]
        gcs = [jnp.dot(b["g_t"], cum_ms[d], precision=HI, preferred_element_type=F32) for b, d in units]
        tots = [gc[:, lasts[d]:lasts[d] + 1] for (b, d), gc in zip(units, gcs)]
        colss = [_transpose_rows([gc, jnp.exp(gc), b["beta_t"]]) for (b, d), gc in zip(units, gcs)]
        for (b, d), tot in zip(units, tots):
            gl_s[d, pl.ds(b["g0"], 8), :] = jnp.broadcast_to(jnp.exp(tot), (8, blk))
        for b in blocks:
            b["kc_t"] = b["kc"].T
        chains = []
        for (b, d), gc, tot, cols in zip(units, gcs, tots, colss):
            kd_t = jnp.exp(tot - gc)
            for r in range(2):
                k = 2 * d + r
                e = jnp.exp(cols[:, k:k + 1] - gc[k:k + 1, :])
                beta = jnp.broadcast_to(cols[:, 20 + k:21 + k], (blk, GDN_HEAD))
                eg = jnp.broadcast_to(cols[:, 8 + k:9 + k], (blk, GDN_HEAD))
                vr = b["vc"][:, r * 128:(r + 1) * 128]
                wq_s[d, r, pl.ds(b["wq0"] + blk, blk), :] = (b["qc"] * eg).astype(BF16)
                qkm_s[d, r, pl.ds(b["r0"], blk), :] = (b["qk"] * jnp.where(incls[d], e, 0.0)).astype(BF16)
                kdt_s[d, r, pl.ds(b["r0"], blk), :] = (b["kc_t"] * kd_t[k:k + 1, :]).astype(BF16)
                chains.append(dict(
                    d=d, r=r, r0=b["r0"], wq0=b["wq0"],
                    a=beta * b["kk"] * jnp.where(stricts[d], e, 0.0),
                    rhs=jnp.concatenate([vr * beta, b["kc"] * (beta * eg)], axis=1)))
        return chains

    per_iter = 4 if nb % 4 == 0 else (2 if nb % 2 == 0 else 1)

    def phase1(i, carry):
        chains = prepare([load_block(i * per_iter + j) for j in range(per_iter)])
        sols = _unit_tri_solve([ch["a"] for ch in chains], [ch["rhs"] for ch in chains], merge_masks)
        for sol, ch in zip(sols, chains):
            u_s[ch["d"], ch["r"], pl.ds(ch["r0"], blk), :] = sol[:, :GDN_HEAD]
            wq_s[ch["d"], ch["r"], pl.ds(ch["wq0"], blk), :] = sol[:, GDN_HEAD:].astype(BF16)
        return carry

    lax.fori_loop(0, nb // per_iter, phase1, 0)

    def phase2(i, carry):
        jobs = []
        for d, c in ((0, i), (1, nb - 1 - i)):
            r0 = pl.multiple_of(c * blk, blk)
            wq0 = pl.multiple_of(c * 2 * blk, 2 * blk)
            gl = gl_s[d, pl.ds(pl.multiple_of(c * 8, 8), 8), :]
            for r in range(2):
                jobs.append((d, r, r0, wq0, gl[2 * d + r:2 * d + r + 1, :]))
        ss = [st[d, r] for d, r, _, _, _ in jobs]
        wss = [jnp.dot(wq_s[d, r, pl.ds(wq0, 2 * blk), :], s.astype(BF16), preferred_element_type=F32)
               for (d, r, _, wq0, _), s in zip(jobs, ss)]
        vns = [(u_s[d, r, pl.ds(r0, blk), :] - ws[:blk]).astype(BF16)
               for (d, r, r0, _, _), ws in zip(jobs, wss)]
        for (d, r, r0, _, gl), s, vn in zip(jobs, ss, vns):
            st[d, r] = gl * s + jnp.dot(kdt_s[d, r, pl.ds(r0, blk), :], vn, preferred_element_type=F32)
        outs = [ws[blk:] + jnp.dot(qkm_s[d, r, pl.ds(r0, blk), :], vn, preferred_element_type=F32)
                for (d, r, r0, _, _), ws, vn in zip(jobs, wss, vns)]
        for half in range(2):
            r0 = jobs[2 * half][2]
            o_ref[pl.ds(r0, blk), :] += jnp.concatenate(outs[2 * half:2 * half + 2], axis=1)
        return carry

    lax.fori_loop(0, nb, phase2, 0)

    def emit(c, carry):
        r0 = pl.multiple_of(c * blk, blk)
        oout_ref[pl.ds(r0, blk), :] = o_ref[pl.ds(r0, blk), :].astype(oout_ref.dtype)
        return carry

    lax.fori_loop(0, nb, emit, 0)

    if want_final:
        for d in range(2):
            for r in range(2):
                sfin_refs[d][r] = st[d, r]


def _gdn_scan_call(proj3, smt, conv_w, prm, s0, want_final, name):
    bsz, seqlen, _ = proj3.shape
    has_init = s0 is not None
    in_specs = [
        pl.BlockSpec((None, seqlen, 128), lambda b, g: (b, 0, g)),
        pl.BlockSpec((None, seqlen, 128), lambda b, g: (b, 0, GDN_K_HEADS + g)),
        pl.BlockSpec((None, seqlen, 256), lambda b, g: (b, 0, GDN_K_HEADS + g)),
        pl.BlockSpec((3, 128), lambda b, g: (0, g)),
        pl.BlockSpec((3, 128), lambda b, g: (0, GDN_K_HEADS + g)),
        pl.BlockSpec((3, 256), lambda b, g: (0, GDN_K_HEADS + g)),
        pl.BlockSpec((None, None, 8, seqlen), lambda b, g: (b, g, 0, 0)),
        pl.BlockSpec((None, 8, 2), lambda b, g: (g, 0, 0)),
    ]
    args = [proj3, proj3, proj3, conv_w, conv_w, conv_w, smt, prm]
    state_spec = pl.BlockSpec((None, 2, GDN_HEAD, GDN_HEAD), lambda b, g: (b, g, 0, 0))
    if has_init:
        in_specs += [state_spec, state_spec]
        args += [s0[0], s0[1]]
    out_specs = [pl.BlockSpec((None, seqlen, 256), lambda b, g: (b, 0, g))]
    out_shape = [jax.ShapeDtypeStruct((bsz, seqlen, GDN_V), BF16)]
    if want_final:
        out_specs += [state_spec, state_spec]
        out_shape += [jax.ShapeDtypeStruct((bsz, GDN_V_HEADS, GDN_HEAD, GDN_HEAD), F32)] * 2
    return pl.pallas_call(
        functools.partial(_gdn_scan_kernel, seqlen=seqlen, has_init=has_init, want_final=want_final),
        grid=(bsz, GDN_K_HEADS),
        in_specs=in_specs,
        out_specs=out_specs,
        out_shape=out_shape,
        scratch_shapes=[pltpu.VMEM((2, 2, seqlen, GDN_HEAD), F32),
                        pltpu.VMEM((2, 2, 2 * seqlen, GDN_HEAD), BF16),
                        pltpu.VMEM((2, 2, seqlen, GDN_BLOCK), BF16),
                        pltpu.VMEM((2, 2, seqlen, GDN_BLOCK), BF16),
                        pltpu.VMEM((2, 8 * (seqlen // GDN_BLOCK), GDN_BLOCK), F32),
                        pltpu.VMEM((2, 2, GDN_HEAD, GDN_HEAD), F32),
                        pltpu.VMEM((seqlen, 256), F32)],
        compiler_params=_params("arbitrary", "arbitrary"),
        name=name,
    )(*args)


def _residual_layer_norm(x, gate, o, ln_g, ln_b):
    r = DEEPNORM_ALPHA * x + gate * o
    mu = jnp.mean(r, axis=-1, keepdims=True)
    rc = r - mu
    var = jnp.mean(rc * rc, axis=-1, keepdims=True)
    return rc * lax.rsqrt(var + LN_EPS) * ln_g + ln_b


def _out_kernel(y_ref, z_ref, nw_ref, w_ref, x_ref, gt_ref, lg_ref, lb_ref, o_ref, *, mode):
    if mode == "ssd":
        u = y_ref[...].astype(F32) * _silu(z_ref[...].astype(F32))
        ms = jnp.mean(u * u, axis=-1, keepdims=True)
        a = (u * lax.rsqrt(ms + RMS_EPS) * nw_ref[...]).astype(BF16)
    else:
        parts = []
        for h in range(GDN_V_HEADS):
            sl = slice(h * GDN_HEAD, (h + 1) * GDN_HEAD)
            oh = y_ref[:, sl].astype(F32)
            ms = jnp.mean(oh * oh, axis=-1, keepdims=True)
            parts.append((oh * lax.rsqrt(ms + RMS_EPS) * nw_ref[...] * _silu(z_ref[:, sl].astype(F32))).astype(BF16))
        a = jnp.concatenate(parts, axis=1)
    o = jnp.dot(a, w_ref[...], preferred_element_type=F32)
    o_ref[...] = _residual_layer_norm(x_ref[...], gt_ref[...], o, lg_ref[...], lb_ref[...])


def _out_call(mode, act, z_src, z_blk, norm_w, w, x, gate, ln_g, ln_b, rows_per_mod, tm, name):
    m, kdim = act.shape
    row_spec = pl.BlockSpec((tm, D_MODEL), lambda i: (i, 0))
    vec_spec = pl.BlockSpec((1, D_MODEL), lambda i: (0, 0))
    in_specs = [pl.BlockSpec((tm, kdim), lambda i: (i, 0)),
                pl.BlockSpec((tm, kdim), lambda i: (i, z_blk)),
                pl.BlockSpec((1, norm_w.shape[1]), lambda i: (0, 0)),
                pl.BlockSpec((kdim, D_MODEL), lambda i: (0, 0)), row_spec,
                pl.BlockSpec((None, 1, D_MODEL), lambda i: ((i * tm) // rows_per_mod, 0, 0)),
                vec_spec, vec_spec]
    args = [act, z_src, norm_w, w, x, gate, ln_g, ln_b]
    return pl.pallas_call(
        functools.partial(_out_kernel, mode=mode),
        grid=(m // tm,),
        in_specs=in_specs,
        out_specs=row_spec,
        out_shape=jax.ShapeDtypeStruct((m, D_MODEL), F32),
        compiler_params=_params("arbitrary"),
        name=name,
    )(*args)


def _ffn_down_kernel(*refs, tm, period, row_taps, tiles_per_seq):
    if row_taps:
        (a_ref, ap_ref, an_ref, v_ref, cw_ref, cb_ref, w_ref, x_ref, gt_ref, lg_ref, lb_ref,
         o_ref, acc_ref, ext_ref) = refs
        t = pl.program_id(0) % tiles_per_seq
        has_prev = (t > 0).astype(F32)
        has_next = (t < tiles_per_seq - 1).astype(F32)
        base, taps, n = HALO, (0, 1, 2), tm + 2 * HALO
    else:
        a_ref, v_ref, cw_ref, cb_ref, w_ref, x_ref, gt_ref, lg_ref, lb_ref, o_ref, acc_ref = refs
        base, taps, n = 0, (1,), tm
    colpos = lax.broadcasted_iota(jnp.int32, (n, FFN_TC), 0) % period
    is_first = colpos == 0
    is_last = colpos == period - 1
    acc_ref[...] = jnp.zeros((tm, D_MODEL), F32)

    def channel_tile(c, carry):
        c0 = pl.multiple_of(c * FFN_TC, FFN_TC)
        cols = pl.ds(c0, FFN_TC)
        if row_taps:
            ext_ref[0:HALO, :] = ap_ref[:, cols].astype(F32) * has_prev
            ext_ref[HALO:HALO + tm, :] = a_ref[:, cols].astype(F32)
            ext_ref[HALO + tm:, :] = an_ref[:, cols].astype(F32) * has_next
            ext = ext_ref[...]
        else:
            ext = a_ref[:, cols].astype(F32)
        em1 = jnp.where(is_first, 0.0, pltpu.roll(ext, 1, 0))
        ep1 = jnp.where(is_last, 0.0, pltpu.roll(ext, n - 1, 0))
        conv = cb_ref[:, cols]
        for dr in taps:
            off = base + (dr - 1) * GRID_W
            conv = (conv + cw_ref[3 * dr:3 * dr + 1, cols] * em1[off:off + tm]
                    + cw_ref[3 * dr + 1:3 * dr + 2, cols] * ext[off:off + tm]
                    + cw_ref[3 * dr + 2:3 * dr + 3, cols] * ep1[off:off + tm])
        act = (_silu(conv) * v_ref[:, cols].astype(F32)).astype(BF16)
        acc_ref[...] += jnp.dot(act, w_ref[pl.ds(c0, FFN_TC), :], preferred_element_type=F32)
        return carry

    lax.fori_loop(0, D_FF // FFN_TC, channel_tile, 0)
    o_ref[...] = _residual_layer_norm(x_ref[...], gt_ref[...], acc_ref[...], lg_ref[...], lb_ref[...])


def _ffn_down_call(up, conv_w9, conv_b, w_down, x, gate, ln_g, ln_b, seqlen, is_grid, rows_per_mod, tm, name):
    m = up.shape[0]
    a_spec = pl.BlockSpec((tm, D_FF), lambda i: (i, 0))
    v_spec = pl.BlockSpec((tm, D_FF), lambda i: (i, 1))
    scratch = [pltpu.VMEM((tm, D_MODEL), F32)]
    if is_grid:
        hb = tm // HALO
        nhb = m // HALO
        tiles_per_seq = seqlen // tm
        in_specs = [a_spec,
                    pl.BlockSpec((HALO, D_FF), lambda i: (jnp.maximum(i * hb - 1, 0), 0)),
                    pl.BlockSpec((HALO, D_FF), lambda i: (jnp.minimum((i + 1) * hb, nhb - 1), 0)),
                    v_spec]
        args = [up, up, up, up]
        scratch.append(pltpu.VMEM((tm + 2 * HALO, FFN_TC), F32))
        period = GRID_W
    else:
        tiles_per_seq = 1
        in_specs = [a_spec, v_spec]
        args = [up, up]
        period = seqlen
    row_spec = pl.BlockSpec((tm, D_MODEL), lambda i: (i, 0))
    vec_spec = pl.BlockSpec((1, D_MODEL), lambda i: (0, 0))
    in_specs += [pl.BlockSpec((9, D_FF), lambda i: (0, 0)), pl.BlockSpec((1, D_FF), lambda i: (0, 0)),
                 pl.BlockSpec((D_FF, D_MODEL), lambda i: (0, 0)), row_spec,
                 pl.BlockSpec((None, 1, D_MODEL), lambda i: ((i * tm) // rows_per_mod, 0, 0)),
                 vec_spec, vec_spec]
    args += [conv_w9, conv_b, w_down, x, gate, ln_g, ln_b]
    return pl.pallas_call(
        functools.partial(_ffn_down_kernel, tm=tm, period=period, row_taps=is_grid, tiles_per_seq=tiles_per_seq),
        grid=(m // tm,),
        in_specs=in_specs,
        out_specs=row_spec,
        out_shape=jax.ShapeDtypeStruct((m, D_MODEL), F32),
        scratch_shapes=scratch,
        compiler_params=_params("arbitrary"),
        name=name,
    )(*args)


def _group_rows(t, bsz, seqlen, lead):
    n_lead = int(math.prod(lead))
    t = t.reshape(bsz, seqlen, n_lead, 8, -1)
    r = t.shape[-1]
    t = jnp.transpose(t, (0, 3, 2, 4, 1))
    return t.reshape(bsz, 8, n_lead * r, seqlen)


def _group_params(p):
    r = p.shape[1] // 8
    return jnp.transpose(p.reshape(2, 8, r), (1, 0, 2)).reshape(8, 2 * r)


def _run_tokens(x, seqlen, mods, states, want_final, w, tag):
    bsz = x.shape[0]
    m = bsz * seqlen
    nb = mods.shape[1]
    rows_per_mod = m // nb
    xt = x.reshape(m, D_MODEL)
    finals = []
    is_grid = tag == "s"
    for i in range(DEPTH):
        def mod(k):
            return mods[i, :, k].reshape(nb, 1, D_MODEL)
        lw = w[i]
        proj, small = _proj_call(xt, mod(1), mod(0), lw["w_in"], lw["w_in_small"], rows_per_mod,
                                 tm=1024, tn=1536, name=f"inproj{i}{tag}")
        proj3 = proj.reshape(bsz, seqlen, 6144)
        if i % 2 == 0:
            dtt = _group_rows(small[:, :2 * SSD_HEADS], bsz, seqlen, (2,))
            res = _ssd_scan_call(proj3, dtt, lw["conv_w"], lw["conv_b"], lw["prm"], lw["dskip"],
                                 states[i], want_final, name=f"ssdscan{tag}")
            mode, z_blk = "ssd", 0
        else:
            smt = _group_rows(small[:, :4 * GDN_V_HEADS], bsz, seqlen, (2, 2))
            res = _gdn_scan_call(proj3, smt, lw["conv_w"], lw["prm"], states[i], want_final,
                                 name=f"gdnscan{tag}")
            mode, z_blk = "gdn", 2
        y = res[0].reshape(m, 2048)
        if want_final:
            finals.append((res[1], res[2]))
        xt = _out_call(mode, y, proj, z_blk, lw["norm_w"], lw["w_out"], xt, mod(2), lw["ln_g0"], lw["ln_b0"],
                       rows_per_mod, tm=256, name=f"outproj{i}{tag}")
        up = _proj_call(xt, mod(4), mod(3), lw["w_up"], None, rows_per_mod,
                        tm=1024, tn=1408, name=f"ffnup{i}{tag}")[0]
        xt = _ffn_down_call(up, lw["ffn_conv_w"], lw["ffn_conv_b"], lw["w_down"], xt, mod(5),
                            lw["ln_g1"], lw["ln_b1"], seqlen, is_grid, rows_per_mod, tm=512,
                            name=f"ffndown{i}{tag}")
    return xt.reshape(bsz, seqlen, D_MODEL), finals


def kernel(x_prompt, x_sample, state_ssd_fwd, state_ssd_bwd, state_gdn_fwd, state_gdn_bwd, c, c_ctx, w_ada, b_ada, ln_g, ln_b, ssd_w_in, ssd_conv_w, ssd_conv_b, ssd_dt_bias, ssd_a_log, ssd_d, ssd_norm_w, ssd_w_out, gdn_w_in, gdn_conv_w, gdn_dt_bias, gdn_a_log, gdn_norm_w, gdn_w_out, ffn_w_up, ffn_conv_w, ffn_conv_b, ffn_w_down):
    n_dec = c.shape[0]
    cond = jnp.concatenate([c_ctx[None, :], c, jnp.zeros((16 - 1 - n_dec, D_MODEL), F32)], axis=0)
    ada = _ada_call(cond, w_ada, b_ada).reshape(DEPTH, 16, 6, D_MODEL)
    mods_p = ada[:, 0:1]
    mods_s = ada[:, 1:1 + n_dec]

    def small_cols(wm):
        return jnp.pad(wm, ((0, 0), (0, 128 - wm.shape[1]))).astype(BF16)

    weights = []
    for i in range(DEPTH):
        j = i // 2
        lw = {
            "ln_g0": ln_g[i, 0:1], "ln_b0": ln_b[i, 0:1], "ln_g1": ln_g[i, 1:2], "ln_b1": ln_b[i, 1:2],
            "w_up": ffn_w_up[i].astype(BF16), "w_down": ffn_w_down[i].astype(BF16),
            "ffn_conv_w": ffn_conv_w[i].reshape(9, D_FF), "ffn_conv_b": ffn_conv_b[i].reshape(1, D_FF),
        }
        if i % 2 == 0:
            lw["w_in"] = ssd_w_in[j][:, :6144].astype(BF16)
            lw["w_in_small"] = small_cols(ssd_w_in[j][:, 6144:])
            lw["conv_w"] = ssd_conv_w[j]
            lw["conv_b"] = ssd_conv_b[j].reshape(1, SSD_XBC)
            lw["prm"] = jnp.stack([_group_params(ssd_dt_bias[j]), _group_params(ssd_a_log[j])], axis=-1)
            lw["dskip"] = jnp.repeat(ssd_d[j], SSD_HEADDIM).reshape(SSD_GROUPS, 1, 256)
            lw["norm_w"] = ssd_norm_w[j].reshape(1, SSD_D_INNER)
            lw["w_out"] = ssd_w_out[j].astype(BF16)
        else:
            lw["w_in"] = gdn_w_in[j][:, :6144].astype(BF16)
            lw["w_in_small"] = small_cols(gdn_w_in[j][:, 6144:])
            lw["conv_w"] = gdn_conv_w[j]
            prm = jnp.stack([_group_params(gdn_dt_bias[j]), _group_params(gdn_a_log[j])], axis=-1)
            lw["prm"] = jnp.pad(prm, ((0, 0), (0, 4), (0, 0)))
            lw["norm_w"] = gdn_norm_w[j].reshape(1, GDN_HEAD)
            lw["w_out"] = gdn_w_out[j].astype(BF16)
        weights.append(lw)

    states_s = [(state_ssd_fwd[:, 0], state_ssd_bwd[:, 0]), (state_gdn_fwd[:, 0], state_gdn_bwd[:, 0])]
    y_prompt, fin = _run_tokens(x_prompt, x_prompt.shape[1], mods_p, [None, None], True, weights, "p")
    y_sample, _ = _run_tokens(x_sample, x_sample.shape[1], mods_s, states_s, False, weights, "s")
    return (y_prompt, y_sample,
            fin[0][0][:, None], fin[0][1][:, None], fin[1][0][:, None], fin[1][1][:, None])
```

```python
import functools
import math

import jax
import jax.numpy as jnp
from jax import lax
from jax.experimental import pallas as pl
from jax.experimental.pallas import tpu as pltpu

F32 = jnp.float32
BF16 = jnp.bfloat16
HI = lax.Precision.HIGHEST

D_MODEL = 1024
DEPTH = 2
GRID_W = 64

SSD_D_INNER = 2048
SSD_HEADDIM = 64
SSD_HEADS = 32
SSD_GROUPS = 8
SSD_STATE = 128
SSD_CHUNK = 128
SSD_XBC = 4096

GDN_K_HEADS = 8
GDN_V_HEADS = 16
GDN_HEAD = 128
GDN_QK = 1024
GDN_V = 2048
GDN_QKV = 4096
GDN_BLOCK = 128

D_FF = 2816
FFN_TC = 256
HALO = 128
CONV_STRIP = 64

DEEPNORM_ALPHA = (2.0 * DEPTH) ** 0.25
LN_EPS = 1e-5
RMS_EPS = 1e-6

VMEM_LIMIT = 56 * 1024 * 1024
BF16_ROWS = 16


def _silu(x):
    h = 0.5 * x
    return h + h * jnp.tanh(h)


def _sigmoid(x):
    return 0.5 + 0.5 * jnp.tanh(0.5 * x)


def _softplus(x):
    return jnp.maximum(x, 0.0) + jnp.log(1.0 + jnp.exp(-jnp.abs(x)))


def _params(*sem):
    return pltpu.CompilerParams(dimension_semantics=sem, vmem_limit_bytes=VMEM_LIMIT)


def _ada_kernel(c_ref, w_ref, b_ref, o_ref):
    h = _silu(c_ref[...]).astype(BF16)
    o_ref[...] = jnp.dot(h, w_ref[...].astype(BF16), preferred_element_type=F32) + b_ref[...]


def _ada_call(cond, w_ada, b_ada):
    rows = cond.shape[0]
    tn = 1536
    return pl.pallas_call(
        _ada_kernel,
        grid=(DEPTH, 6 * D_MODEL // tn),
        in_specs=[
            pl.BlockSpec((rows, D_MODEL), lambda l, j: (0, 0)),
            pl.BlockSpec((None, D_MODEL, tn), lambda l, j: (l, 0, j)),
            pl.BlockSpec((None, 1, tn), lambda l, j: (l, 0, j)),
        ],
        out_specs=pl.BlockSpec((None, rows, tn), lambda l, j: (l, 0, j)),
        out_shape=jax.ShapeDtypeStruct((DEPTH, rows, 6 * D_MODEL), F32),
        compiler_params=_params("arbitrary", "arbitrary"),
        name="adaln",
    )(cond, w_ada, b_ada.reshape(DEPTH, 1, 6 * D_MODEL))


def _proj_kernel(*refs, has_small):
    if has_small:
        x_ref, sc_ref, sh_ref, w_ref, w2_ref, o_ref, o2_ref, h_ref = refs
    else:
        x_ref, sc_ref, sh_ref, w_ref, o_ref, h_ref = refs

    @pl.when(pl.program_id(1) == 0)
    def _():
        h = x_ref[...] * (1.0 + sc_ref[...]) + sh_ref[...]
        h_ref[...] = h.astype(BF16)
        if has_small:
            o2_ref[...] = jnp.dot(h_ref[...], w2_ref[...], preferred_element_type=F32)

    o_ref[...] = jnp.dot(h_ref[...], w_ref[...], preferred_element_type=F32).astype(o_ref.dtype)


def _proj_call(x, scale, shift, w, w_small, rows_per_mod, tm, tn, name):
    m = x.shape[0]
    n = w.shape[1]
    has_small = w_small is not None
    mod_spec = pl.BlockSpec((None, 1, D_MODEL), lambda i, j: ((i * tm) // rows_per_mod, 0, 0))
    in_specs = [pl.BlockSpec((tm, D_MODEL), lambda i, j: (i, 0)), mod_spec, mod_spec,
                pl.BlockSpec((D_MODEL, tn), lambda i, j: (0, j))]
    out_specs = [pl.BlockSpec((tm, tn), lambda i, j: (i, j))]
    out_shape = [jax.ShapeDtypeStruct((m, n), BF16)]
    args = [x, scale, shift, w]
    if has_small:
        in_specs.append(pl.BlockSpec((D_MODEL, 128), lambda i, j: (0, 0)))
        out_specs.append(pl.BlockSpec((tm, 128), lambda i, j: (i, 0)))
        out_shape.append(jax.ShapeDtypeStruct((m, 128), F32))
        args.append(w_small)
    return pl.pallas_call(
        functools.partial(_proj_kernel, has_small=has_small),
        grid=(m // tm, n // tn),
        in_specs=in_specs,
        out_specs=out_specs,
        out_shape=out_shape,
        scratch_shapes=[pltpu.VMEM((tm, D_MODEL), BF16)],
        compiler_params=_params("arbitrary", "arbitrary"),
        name=name,
    )(*args)


def _conv_silu_rows(src_ref, r0, n_rows, seqlen, w_ref, bias):
    cur = src_ref[pl.ds(r0, n_rows), :].astype(F32)
    has_prev = (r0 > 0).astype(F32)
    has_next = (r0 + n_rows < seqlen).astype(F32)
    g_prev = pl.multiple_of(jnp.maximum(r0 - BF16_ROWS, 0), BF16_ROWS)
    g_next = pl.multiple_of(jnp.minimum(r0 + n_rows, seqlen - BF16_ROWS), BF16_ROWS)
    prev_row = src_ref[pl.ds(g_prev, BF16_ROWS), :].astype(F32)[BF16_ROWS - 1:BF16_ROWS] * has_prev
    next_row = src_ref[pl.ds(g_next, BF16_ROWS), :].astype(F32)[0:1] * has_next
    rid = lax.broadcasted_iota(jnp.int32, cur.shape, 0)
    xm1 = jnp.where(rid == 0, prev_row, pltpu.roll(cur, 1, 0))
    xp1 = jnp.where(rid == n_rows - 1, next_row, pltpu.roll(cur, n_rows - 1, 0))
    out = w_ref[0:1, :] * xm1 + w_ref[1:2, :] * cur + w_ref[2:3, :] * xp1
    if bias is not None:
        out = out + bias
    return _silu(out)


def _transpose_rows(rows_list):
    pad = 128 - 8 * len(rows_list)
    stack = jnp.concatenate(rows_list + [jnp.zeros((pad, 128), F32)], axis=0)
    return stack.T


def _ssd_scan_kernel(*refs, seqlen, has_init, want_final):
    refs = list(refs)
    x_ref, b_ref, c_ref, cwx, cwb, cwc, cbx, cbb, cbc, dtt_ref, prm_ref, dsk_ref = refs[:12]
    pos = 12
    if has_init:
        s0_refs = refs[pos:pos + 2]
        pos += 2
    yout_ref = refs[pos]
    pos += 1
    if want_final:
        sfin_refs = refs[pos:pos + 2]
        pos += 2
    cs_b, ea_s, ns_s, st, y_ref = refs[pos:pos + 5]

    nc = seqlen // SSD_CHUNK
    q = SSD_CHUNK

    for d in range(2):
        if has_init:
            halves = []
            for h in range(2):
                blk = jnp.concatenate([s0_refs[d][2 * h], s0_refs[d][2 * h + 1]], axis=0)
                halves.append(blk.T)
            st[d] = jnp.concatenate(halves, axis=1)
        else:
            st[d] = jnp.zeros((SSD_STATE, 256), F32)

    row = lax.broadcasted_iota(jnp.int32, (q, q), 0)
    col = lax.broadcasted_iota(jnp.int32, (q, q), 1)
    lane_head = lax.broadcasted_iota(jnp.int32, (q, 256), 1) // SSD_HEADDIM
    bias = prm_ref[:, 0:1]
    a_neg = -jnp.exp(prm_ref[:, 1:2])
    dskip = dsk_ref[...]

    cum_ms = ((row <= col).astype(F32), (row >= col).astype(F32))
    keeps = (row >= col, row <= col)
    lasts = (q - 1, 0)

    def intra(i, carry):
        blocks = []
        for j in range(unroll_a):
            r0 = pl.multiple_of((i * unroll_a + j) * q, q)
            xc = _conv_silu_rows(x_ref, r0, q, seqlen, cwx, cbx[...])
            bc = _conv_silu_rows(b_ref, r0, q, seqlen, cwb, cbb[...])
            cc = _conv_silu_rows(c_ref, r0, q, seqlen, cwc, cbc[...])
            cc_b = cc.astype(BF16)
            cs_b[pl.ds(r0, q), :] = cc_b
            bc_t = bc.T
            dt_t = _softplus(dtt_ref[:, pl.ds(r0, q)] + bias)
            x_heads = jnp.concatenate([jnp.where(lane_head == r, xc, 0.0).astype(BF16) for r in range(4)], axis=0)
            blocks.append(dict(r0=r0, xc=xc, bc_t=bc_t, dt_t=dt_t, a_t=dt_t * a_neg, x_heads=x_heads,
                               scores=jnp.dot(cc_b, bc_t.astype(BF16), preferred_element_type=F32)))
        units = [(blk, d) for blk in blocks for d in range(2)]
        acums = [jnp.dot(blk["a_t"], cum_ms[d], precision=HI, preferred_element_type=F32) for blk, d in units]
        cols = _transpose_rows(acums)
        dtdes = [blk["dt_t"] * jnp.exp(ac[:, lasts[d]:lasts[d] + 1] - ac) for (blk, d), ac in zip(units, acums)]
        for (blk, d), ac in zip(units, acums):
            ea_t = jnp.exp(ac)
            ea_rows = jnp.concatenate([jnp.broadcast_to(ea_t[4 * d + r:4 * d + r + 1, :], (SSD_HEADDIM, q))
                                       for r in range(4)], axis=0)
            ea_s[d, pl.ds(blk["r0"], q), :] = jnp.concatenate([ea_rows[0:128].T, ea_rows[128:256].T], axis=1)
        lhss = []
        for u, ((blk, d), ac, dtde) in enumerate(zip(units, acums, dtdes)):
            parts = []
            for r in range(4):
                k = 4 * d + r
                decay = jnp.where(keeps[d], jnp.exp(cols[:, 8 * u + k:8 * u + k + 1] - ac[k:k + 1, :]), 0.0)
                m = decay * blk["scores"] * blk["dt_t"][k:k + 1, :]
                bt = blk["bc_t"] * dtde[k:k + 1, :]
                parts.append(jnp.concatenate([m, bt], axis=0).astype(BF16))
            lhss.append(jnp.concatenate(parts, axis=1))
        ress = [jnp.dot(lhs, blk["x_heads"], preferred_element_type=F32) for lhs, (blk, d) in zip(lhss, units)]
        for res, (blk, d) in zip(ress, units):
            ns_s[d, pl.ds(blk["r0"], q), :] = res[q:]
        for j, blk in enumerate(blocks):
            y_ref[pl.ds(blk["r0"], q), :] = blk["xc"] * dskip + ress[2 * j][:q] + ress[2 * j + 1][:q]
        return carry

    unroll_a = 4 if nc % 4 == 0 else (2 if nc % 2 == 0 else 1)
    lax.fori_loop(0, nc // unroll_a, intra, 0)

    def inter(i, carry):
        jobs = []
        for d in range(2):
            s_t = st[d]
            for j in range(unroll_a):
                c = i * unroll_a + j
                r0 = pl.multiple_of((c if d == 0 else nc - 1 - c) * q, q)
                jobs.append((d, r0, s_t.astype(BF16)))
                s_t = s_t * ea_s[d, pl.ds(r0 + lasts[d], 1), :] + ns_s[d, pl.ds(r0, q), :]
            st[d] = s_t
        y_inters = [jnp.dot(cs_b[pl.ds(r0, q), :], s_b, preferred_element_type=F32) * ea_s[d, pl.ds(r0, q), :]
                    for d, r0, s_b in jobs]
        for (d, r0, _), y_inter in zip(jobs, y_inters):
            y_ref[pl.ds(r0, q), :] += y_inter
        return carry

    lax.fori_loop(0, nc // unroll_a, inter, 0)

    def emit(c, carry):
        r0 = pl.multiple_of(c * q, q)
        yout_ref[pl.ds(r0, q), :] = y_ref[pl.ds(r0, q), :].astype(yout_ref.dtype)
        return carry

    lax.fori_loop(0, nc, emit, 0)

    if want_final:
        for d in range(2):
            s_t = st[d]
            for h in range(2):
                blk = s_t[:, h * 128:(h + 1) * 128].T
                sfin_refs[d][2 * h] = blk[0:64]
                sfin_refs[d][2 * h + 1] = blk[64:128]


def _ssd_scan_call(proj3, dtt, conv_w, conv_b, prm, dskip, s0, want_final, name):
    bsz, seqlen, _ = proj3.shape
    has_init = s0 is not None
    xoff = SSD_D_INNER // 256
    boff = (2 * SSD_D_INNER) // 128
    coff = boff + SSD_GROUPS
    in_specs = [
        pl.BlockSpec((None, seqlen, 256), lambda b, g: (b, 0, xoff + g)),
        pl.BlockSpec((None, seqlen, 128), lambda b, g: (b, 0, boff + g)),
        pl.BlockSpec((None, seqlen, 128), lambda b, g: (b, 0, coff + g)),
        pl.BlockSpec((3, 256), lambda b, g: (0, g)),
        pl.BlockSpec((3, 128), lambda b, g: (0, 16 + g)),
        pl.BlockSpec((3, 128), lambda b, g: (0, 24 + g)),
        pl.BlockSpec((1, 256), lambda b, g: (0, g)),
        pl.BlockSpec((1, 128), lambda b, g: (0, 16 + g)),
        pl.BlockSpec((1, 128), lambda b, g: (0, 24 + g)),
        pl.BlockSpec((None, None, 8, seqlen), lambda b, g: (b, g, 0, 0)),
        pl.BlockSpec((None, 8, 2), lambda b, g: (g, 0, 0)),
        pl.BlockSpec((None, 1, 256), lambda b, g: (g, 0, 0)),
    ]
    args = [proj3, proj3, proj3, conv_w, conv_w, conv_w, conv_b, conv_b, conv_b, dtt, prm, dskip]
    state_spec = pl.BlockSpec((None, 4, SSD_HEADDIM, SSD_STATE), lambda b, g: (b, g, 0, 0))
    if has_init:
        in_specs += [state_spec, state_spec]
        args += [s0[0], s0[1]]
    out_specs = [pl.BlockSpec((None, seqlen, 256), lambda b, g: (b, 0, g))]
    out_shape = [jax.ShapeDtypeStruct((bsz, seqlen, SSD_D_INNER), BF16)]
    if want_final:
        out_specs += [state_spec, state_spec]
        out_shape += [jax.ShapeDtypeStruct((bsz, SSD_HEADS, SSD_HEADDIM, SSD_STATE), F32)] * 2
    return pl.pallas_call(
        functools.partial(_ssd_scan_kernel, seqlen=seqlen, has_init=has_init, want_final=want_final),
        grid=(bsz, SSD_GROUPS),
        in_specs=in_specs,
        out_specs=out_specs,
        out_shape=out_shape,
        scratch_shapes=[pltpu.VMEM((seqlen, SSD_STATE), BF16),
                        pltpu.VMEM((2, seqlen, 256), F32),
                        pltpu.VMEM((2, seqlen, 256), F32),
                        pltpu.VMEM((2, SSD_STATE, 256), F32),
                        pltpu.VMEM((seqlen, 256), F32)],
        compiler_params=_params("arbitrary", "arbitrary"),
        name=name,
    )(*args)


def _unit_tri_solve(a_list, rhs_list, merge_masks):
    xs = a_list
    for m in merge_masks:
        xs_b = [x.astype(BF16) for x in xs]
        xs = [x - jnp.dot(x_b, x_b * m, preferred_element_type=F32) for x, x_b in zip(xs, xs_b)]
    return [rhs - jnp.dot(x.astype(BF16), rhs.astype(BF16), preferred_element_type=F32)
            for x, rhs in zip(xs, rhs_list)]


def _gdn_scan_kernel(*refs, seqlen, has_init, want_final):
    refs = list(refs)
    q_ref, k_ref, v_ref, cwq, cwk, cwv, smt_ref, prm_ref = refs[:8]
    pos = 8
    if has_init:
        s0_refs = refs[pos:pos + 2]
        pos += 2
    oout_ref = refs[pos]
    pos += 1
    if want_final:
        sfin_refs = refs[pos:pos + 2]
        pos += 2
    u_s, wq_s, qkm_s, kdt_s, gl_s, st, o_ref = refs[pos:pos + 7]

    blk = GDN_BLOCK
    nb = seqlen // blk
    qk_scale = GDN_HEAD ** -0.5

    for d in range(2):
        for r in range(2):
            if has_init:
                st[d, r] = s0_refs[d][r]
            else:
                st[d, r] = jnp.zeros((GDN_HEAD, GDN_HEAD), F32)

    row = lax.broadcasted_iota(jnp.int32, (blk, blk), 0)
    col = lax.broadcasted_iota(jnp.int32, (blk, blk), 1)
    merge_masks = [jnp.where(((row >> (l + 1)) == (col >> (l + 1))) & ((row >> l) != (col >> l)), 1.0, 0.0
                             ).astype(BF16) for l in range(7)]
    bias = prm_ref[:, 0:1]
    a_neg = -jnp.exp(prm_ref[:, 1:2])
    nt = (((1,), (1,)), ((), ()))

    cum_ms = ((row <= col).astype(F32), (row >= col).astype(F32))
    stricts = (row > col, row < col)
    incls = (row >= col, row <= col)
    lasts = (blk - 1, 0)

    def load_block(c):
        r0 = pl.multiple_of(c * blk, blk)
        qv = _conv_silu_rows(q_ref, r0, blk, seqlen, cwq, None)
        kv = _conv_silu_rows(k_ref, r0, blk, seqlen, cwk, None)
        qc = qv * (lax.rsqrt(jnp.sum(qv * qv, axis=-1, keepdims=True) + RMS_EPS) * qk_scale)
        kc = kv * lax.rsqrt(jnp.sum(kv * kv, axis=-1, keepdims=True) + RMS_EPS)
        vc = _conv_silu_rows(v_ref, r0, blk, seqlen, cwv, None)
        o_ref[pl.ds(r0, blk), :] = jnp.zeros((blk, 256), F32)
        sm = smt_ref[:, pl.ds(r0, blk)]
        return dict(r0=r0, wq0=pl.multiple_of(c * 2 * blk, 2 * blk), g0=pl.multiple_of(c * 8, 8),
                    qc=qc, kc=kc, vc=vc, g_t=a_neg * _softplus(sm + bias), beta_t=_sigmoid(sm))

    def prepare(blocks):
        for b in blocks:
            kc_b = b["kc"].astype(BF16)
            b["kk"] = lax.dot_general(kc_b, kc_b, nt, preferred_element_type=F32)
            b["qk"] = lax.dot_general(b["qc"].astype(BF16), kc_b, nt, preferred_element_type=F32)
        units = [(b, d) for b in blocks for d in range(2)]
        gcs = [jnp.dot(b["g_t"], cum_ms[d], precision=HI, preferred_element_type=F32) for b, d in units]
        tots = [gc[:, lasts[d]:lasts[d] + 1] for (b, d), gc in zip(units, gcs)]
        colss = [_transpose_rows([gc, jnp.exp(gc), b["beta_t"]]) for (b, d), gc in zip(units, gcs)]
        for (b, d), tot in zip(units, tots):
            gl_s[d, pl.ds(b["g0"], 8), :] = jnp.broadcast_to(jnp.exp(tot), (8, blk))
        for b in blocks:
            b["kc_t"] = b["kc"].T
        chains = []
        for (b, d), gc, tot, cols in zip(units, gcs, tots, colss):
            kd_t = jnp.exp(tot - gc)
            for r in range(2):
                k = 2 * d + r
                e = jnp.exp(cols[:, k:k + 1] - gc[k:k + 1, :])
                beta = jnp.broadcast_to(cols[:, 20 + k:21 + k], (blk, GDN_HEAD))
                eg = jnp.broadcast_to(cols[:, 8 + k:9 + k], (blk, GDN_HEAD))
                vr = b["vc"][:, r * 128:(r + 1) * 128]
                wq_s[d, r, pl.ds(b["wq0"] + blk, blk), :] = (b["qc"] * eg).astype(BF16)
                qkm_s[d, r, pl.ds(b["r0"], blk), :] = (b["qk"] * jnp.where(incls[d], e, 0.0)).astype(BF16)
                kdt_s[d, r, pl.ds(b["r0"], blk), :] = (b["kc_t"] * kd_t[k:k + 1, :]).astype(BF16)
                chains.append(dict(
                    d=d, r=r, r0=b["r0"], wq0=b["wq0"],
                    a=beta * b["kk"] * jnp.where(stricts[d], e, 0.0),
                    rhs=jnp.concatenate([vr * beta, b["kc"] * (beta * eg)], axis=1)))
        return chains

    per_iter = 4 if nb % 4 == 0 else (2 if nb % 2 == 0 else 1)

    def phase1(i, carry):
        chains = prepare([load_block(i * per_iter + j) for j in range(per_iter)])
        sols = _unit_tri_solve([ch["a"] for ch in chains], [ch["rhs"] for ch in chains], merge_masks)
        for sol, ch in zip(sols, chains):
            u_s[ch["d"], ch["r"], pl.ds(ch["r0"], blk), :] = sol[:, :GDN_HEAD]
            wq_s[ch["d"], ch["r"], pl.ds(ch["wq0"], blk), :] = sol[:, GDN_HEAD:].astype(BF16)
        return carry

    lax.fori_loop(0, nb // per_iter, phase1, 0)

    def phase2(i, carry):
        jobs = []
        for d, c in ((0, i), (1, nb - 1 - i)):
            r0 = pl.multiple_of(c * blk, blk)
            wq0 = pl.multiple_of(c * 2 * blk, 2 * blk)
            gl = gl_s[d, pl.ds(pl.multiple_of(c * 8, 8), 8), :]
            for r in range(2):
                jobs.append((d, r, r0, wq0, gl[2 * d + r:2 * d + r + 1, :]))
        ss = [st[d, r] for d, r, _, _, _ in jobs]
        wss = [jnp.dot(wq_s[d, r, pl.ds(wq0, 2 * blk), :], s.astype(BF16), preferred_element_type=F32)
               for (d, r, _, wq0, _), s in zip(jobs, ss)]
        vns = [(u_s[d, r, pl.ds(r0, blk), :] - ws[:blk]).astype(BF16)
               for (d, r, r0, _, _), ws in zip(jobs, wss)]
        for (d, r, r0, _, gl), s, vn in zip(jobs, ss, vns):
            st[d, r] = gl * s + jnp.dot(kdt_s[d, r, pl.ds(r0, blk), :], vn, preferred_element_type=F32)
        outs = [ws[blk:] + jnp.dot(qkm_s[d, r, pl.ds(r0, blk), :], vn, preferred_element_type=F32)
                for (d, r, r0, _, _), ws, vn in zip(jobs, wss, vns)]
        for half in range(2):
            r0 = jobs[2 * half][2]
            o_ref[pl.ds(r0, blk), :] += jnp.concatenate(outs[2 * half:2 * half + 2], axis=1)
        return carry

    lax.fori_loop(0, nb, phase2, 0)

    def emit(c, carry):
        r0 = pl.multiple_of(c * blk, blk)
        oout_ref[pl.ds(r0, blk), :] = o_ref[pl.ds(r0, blk), :].astype(oout_ref.dtype)
        return carry

    lax.fori_loop(0, nb, emit, 0)

    if want_final:
        for d in range(2):
            for r in range(2):
                sfin_refs[d][r] = st[d, r]


def _gdn_scan_call(proj3, smt, conv_w, prm, s0, want_final, name):
    bsz, seqlen, _ = proj3.shape
    has_init = s0 is not None
    in_specs = [
        pl.BlockSpec((None, seqlen, 128), lambda b, g: (b, 0, g)),
        pl.BlockSpec((None, seqlen, 128), lambda b, g: (b, 0, GDN_K_HEADS + g)),
        pl.BlockSpec((None, seqlen, 256), lambda b, g: (b, 0, GDN_K_HEADS + g)),
        pl.BlockSpec((3, 128), lambda b, g: (0, g)),
        pl.BlockSpec((3, 128), lambda b, g: (0, GDN_K_HEADS + g)),
        pl.BlockSpec((3, 256), lambda b, g: (0, GDN_K_HEADS + g)),
        pl.BlockSpec((None, None, 8, seqlen), lambda b, g: (b, g, 0, 0)),
        pl.BlockSpec((None, 8, 2), lambda b, g: (g, 0, 0)),
    ]
    args = [proj3, proj3, proj3, conv_w, conv_w, conv_w, smt, prm]
    state_spec = pl.BlockSpec((None, 2, GDN_HEAD, GDN_HEAD), lambda b, g: (b, g, 0, 0))
    if has_init:
        in_specs += [state_spec, state_spec]
        args += [s0[0], s0[1]]
    out_specs = [pl.BlockSpec((None, seqlen, 256), lambda b, g: (b, 0, g))]
    out_shape = [jax.ShapeDtypeStruct((bsz, seqlen, GDN_V), BF16)]
    if want_final:
        out_specs += [state_spec, state_spec]
        out_shape += [jax.ShapeDtypeStruct((bsz, GDN_V_HEADS, GDN_HEAD, GDN_HEAD), F32)] * 2
    return pl.pallas_call(
        functools.partial(_gdn_scan_kernel, seqlen=seqlen, has_init=has_init, want_final=want_final),
        grid=(bsz, GDN_K_HEADS),
        in_specs=in_specs,
        out_specs=out_specs,
        out_shape=out_shape,
        scratch_shapes=[pltpu.VMEM((2, 2, seqlen, GDN_HEAD), F32),
                        pltpu.VMEM((2, 2, 2 * seqlen, GDN_HEAD), BF16),
                        pltpu.VMEM((2, 2, seqlen, GDN_BLOCK), BF16),
                        pltpu.VMEM((2, 2, seqlen, GDN_BLOCK), BF16),
                        pltpu.VMEM((2, 8 * (seqlen // GDN_BLOCK), GDN_BLOCK), F32),
                        pltpu.VMEM((2, 2, GDN_HEAD, GDN_HEAD), F32),
                        pltpu.VMEM((seqlen, 256), F32)],
        compiler_params=_params("arbitrary", "arbitrary"),
        name=name,
    )(*args)


def _residual_layer_norm(x, gate, o, ln_g, ln_b):
    r = DEEPNORM_ALPHA * x + gate * o
    mu = jnp.mean(r, axis=-1, keepdims=True)
    rc = r - mu
    var = jnp.mean(rc * rc, axis=-1, keepdims=True)
    return rc * lax.rsqrt(var + LN_EPS) * ln_g + ln_b


def _out_kernel(y_ref, z_ref, nw_ref, w_ref, x_ref, gt_ref, lg_ref, lb_ref, o_ref, *, mode):
    if mode == "ssd":
        u = y_ref[...].astype(F32) * _silu(z_ref[...].astype(F32))
        ms = jnp.mean(u * u, axis=-1, keepdims=True)
        a = (u * lax.rsqrt(ms + RMS_EPS) * nw_ref[...]).astype(BF16)
    else:
        parts = []
        for h in range(GDN_V_HEADS):
            sl = slice(h * GDN_HEAD, (h + 1) * GDN_HEAD)
            oh = y_ref[:, sl].astype(F32)
            ms = jnp.mean(oh * oh, axis=-1, keepdims=True)
            parts.append((oh * lax.rsqrt(ms + RMS_EPS) * nw_ref[...] * _silu(z_ref[:, sl].astype(F32))).astype(BF16))
        a = jnp.concatenate(parts, axis=1)
    o = jnp.dot(a, w_ref[...], preferred_element_type=F32)
    o_ref[...] = _residual_layer_norm(x_ref[...], gt_ref[...], o, lg_ref[...], lb_ref[...])


def _out_call(mode, act, z_src, z_blk, norm_w, w, x, gate, ln_g, ln_b, rows_per_mod, tm, name):
    m, kdim = act.shape
    row_spec = pl.BlockSpec((tm, D_MODEL), lambda i: (i, 0))
    vec_spec = pl.BlockSpec((1, D_MODEL), lambda i: (0, 0))
    in_specs = [pl.BlockSpec((tm, kdim), lambda i: (i, 0)),
                pl.BlockSpec((tm, kdim), lambda i: (i, z_blk)),
                pl.BlockSpec((1, norm_w.shape[1]), lambda i: (0, 0)),
                pl.BlockSpec((kdim, D_MODEL), lambda i: (0, 0)), row_spec,
                pl.BlockSpec((None, 1, D_MODEL), lambda i: ((i * tm) // rows_per_mod, 0, 0)),
                vec_spec, vec_spec]
    args = [act, z_src, norm_w, w, x, gate, ln_g, ln_b]
    return pl.pallas_call(
        functools.partial(_out_kernel, mode=mode),
        grid=(m // tm,),
        in_specs=in_specs,
        out_specs=row_spec,
        out_shape=jax.ShapeDtypeStruct((m, D_MODEL), F32),
        compiler_params=_params("arbitrary"),
        name=name,
    )(*args)


def _ffn_up_conv_kernel(*refs, tm, n_tiles, period, row_taps, tiles_per_seq):
    x_ref, sc_ref, sh_ref, w_ref, cw_ref, cb_ref, o_ref, h_ref, a_ring, v_ring, ext_ref, em1_ref, ep1_ref = refs
    i = pl.program_id(0)
    slot, slot_next, slot_cur, slot_prev = i % 4, (i + 3) % 4, (i + 2) % 4, (i + 1) % 4
    vslot, vslot_cur = i % 3, (i + 1) % 3
    n_ct = D_FF // FFN_TC

    @pl.when(i == 0)
    def _():
        a_ring[...] = jnp.zeros(a_ring.shape, BF16)
        v_ring[...] = jnp.zeros(v_ring.shape, BF16)

    h_ref[...] = (x_ref[...] * (1.0 + sc_ref[...]) + sh_ref[...]).astype(BF16)

    if row_taps:
        t = (i + 2 * tiles_per_seq - 2) % tiles_per_seq
        has_prev = (t > 0).astype(F32)
        has_next = (t < tiles_per_seq - 1).astype(F32)
        base, taps, n = HALO, (0, 1, 2), tm + 2 * HALO
    else:
        base, taps, n = 0, (1,), tm
    colpos = lax.broadcasted_iota(jnp.int32, (n, FFN_TC), 0) % period
    is_first = colpos == 0
    is_last = colpos == period - 1

    def up_tile(c):
        c0 = pl.multiple_of(c * FFN_TC, FFN_TC)
        h = h_ref[...]
        a_ring[slot, :, pl.ds(c0, FFN_TC)] = jnp.dot(
            h, w_ref[:, pl.ds(c0, FFN_TC)], preferred_element_type=F32).astype(BF16)
        v_ring[vslot, :, pl.ds(c0, FFN_TC)] = jnp.dot(
            h, w_ref[:, pl.ds(D_FF + c0, FFN_TC)], preferred_element_type=F32).astype(BF16)

    def gated_tile(c):
        cols = pl.ds(pl.multiple_of(c * FFN_TC, FFN_TC), FFN_TC)
        if row_taps:
            ext_ref[0:HALO, :] = a_ring[slot_prev, tm - HALO:tm, cols].astype(F32) * has_prev
            ext_ref[HALO + tm:, :] = a_ring[slot_next, 0:HALO, cols].astype(F32) * has_next
        ext_ref[base:base + tm, :] = a_ring[slot_cur, :, cols].astype(F32)
        ext = ext_ref[...]
        em1_ref[...] = jnp.where(is_first, 0.0, pltpu.roll(ext, 1, 0))
        ep1_ref[...] = jnp.where(is_last, 0.0, pltpu.roll(ext, n - 1, 0))
        weights = [cw_ref[k:k + 1, cols] for k in range(9)]
        bias = cb_ref[:, cols]
        for s in range(0, tm, CONV_STRIP):
            conv = bias
            for dr in taps:
                rows = slice(base + (dr - 1) * GRID_W + s, base + (dr - 1) * GRID_W + s + CONV_STRIP)
                conv = (conv + weights[3 * dr] * em1_ref[rows, :] + weights[3 * dr + 1] * ext_ref[rows, :]
                        + weights[3 * dr + 2] * ep1_ref[rows, :])
            v = v_ring[vslot_cur, s:s + CONV_STRIP, cols].astype(F32)
            o_ref[s:s + CONV_STRIP, cols] = (_silu(conv) * v).astype(BF16)

    def channel_tile(c, carry):
        gated_tile(c)
        up_tile(c)
        return carry

    lax.fori_loop(0, n_ct, channel_tile, 0)


def _ffn_up_conv_call(x, scale, shift, w_up, conv_w9, conv_b, seqlen, is_grid, rows_per_mod, tm, name):
    m = x.shape[0]
    n_tiles = m // tm
    tiles_per_seq = seqlen // tm if is_grid else 1
    period = GRID_W if is_grid else seqlen

    def tile(i):
        return jnp.minimum(i, n_tiles - 1)

    mod_spec = pl.BlockSpec((None, 1, D_MODEL), lambda i: ((tile(i) * tm) // rows_per_mod, 0, 0))
    n_ext = tm + 2 * HALO if is_grid else tm
    scratch = [pltpu.VMEM((tm, D_MODEL), BF16), pltpu.VMEM((4, tm, D_FF), BF16), pltpu.VMEM((3, tm, D_FF), BF16)]
    scratch += [pltpu.VMEM((n_ext, FFN_TC), F32)] * 3
    return pl.pallas_call(
        functools.partial(_ffn_up_conv_kernel, tm=tm, n_tiles=n_tiles, period=period, row_taps=is_grid,
                          tiles_per_seq=tiles_per_seq),
        grid=(n_tiles + 2,),
        in_specs=[pl.BlockSpec((tm, D_MODEL), lambda i: (tile(i), 0)), mod_spec, mod_spec,
                  pl.BlockSpec((D_MODEL, 2 * D_FF), lambda i: (0, 0), pipeline_mode=pl.Buffered(1)),
                  pl.BlockSpec((9, D_FF), lambda i: (0, 0)), pl.BlockSpec((1, D_FF), lambda i: (0, 0))],
        out_specs=pl.BlockSpec((tm, D_FF), lambda i: (jnp.maximum(i - 2, 0), 0)),
        out_shape=jax.ShapeDtypeStruct((m, D_FF), BF16),
        scratch_shapes=scratch,
        compiler_params=_params("arbitrary"),
        name=name,
    )(x, scale, shift, w_up, conv_w9, conv_b)


def _down_kernel(a_ref, w_ref, x_ref, gt_ref, lg_ref, lb_ref, o_ref):
    o = jnp.dot(a_ref[...], w_ref[...], preferred_element_type=F32)
    o_ref[...] = _residual_layer_norm(x_ref[...], gt_ref[...], o, lg_ref[...], lb_ref[...])


def _down_call(act, w, x, gate, ln_g, ln_b, rows_per_mod, tm, name):
    m, kdim = act.shape
    row_spec = pl.BlockSpec((tm, D_MODEL), lambda i: (i, 0))
    vec_spec = pl.BlockSpec((1, D_MODEL), lambda i: (0, 0))
    return pl.pallas_call(
        _down_kernel,
        grid=(m // tm,),
        in_specs=[pl.BlockSpec((tm, kdim), lambda i: (i, 0)), pl.BlockSpec((kdim, D_MODEL), lambda i: (0, 0)),
                  row_spec, pl.BlockSpec((None, 1, D_MODEL), lambda i: ((i * tm) // rows_per_mod, 0, 0)),
                  vec_spec, vec_spec],
        out_specs=row_spec,
        out_shape=jax.ShapeDtypeStruct((m, D_MODEL), F32),
        compiler_params=_params("arbitrary"),
        name=name,
    )(act, w, x, gate, ln_g, ln_b)


def _group_rows(t, bsz, seqlen, lead):
    n_lead = int(math.prod(lead))
    t = t.reshape(bsz, seqlen, n_lead, 8, -1)
    r = t.shape[-1]
    t = jnp.transpose(t, (0, 3, 2, 4, 1))
    return t.reshape(bsz, 8, n_lead * r, seqlen)


def _group_params(p):
    r = p.shape[1] // 8
    return jnp.transpose(p.reshape(2, 8, r), (1, 0, 2)).reshape(8, 2 * r)


def _run_tokens(x, seqlen, mods, states, want_final, w, tag):
    bsz = x.shape[0]
    m = bsz * seqlen
    nb = mods.shape[1]
    rows_per_mod = m // nb
    xt = x.reshape(m, D_MODEL)
    finals = []
    is_grid = tag == "s"
    for i in range(DEPTH):
        def mod(k):
            return mods[i, :, k].reshape(nb, 1, D_MODEL)
        lw = w[i]
        proj, small = _proj_call(xt, mod(1), mod(0), lw["w_in"], lw["w_in_small"], rows_per_mod,
                                 tm=1024, tn=1536, name=f"inproj{i}{tag}")
        proj3 = proj.reshape(bsz, seqlen, 6144)
        if i % 2 == 0:
            dtt = _group_rows(small[:, :2 * SSD_HEADS], bsz, seqlen, (2,))
            res = _ssd_scan_call(proj3, dtt, lw["conv_w"], lw["conv_b"], lw["prm"], lw["dskip"],
                                 states[i], want_final, name=f"ssdscan{tag}")
            mode, z_blk = "ssd", 0
        else:
            smt = _group_rows(small[:, :4 * GDN_V_HEADS], bsz, seqlen, (2, 2))
            res = _gdn_scan_call(proj3, smt, lw["conv_w"], lw["prm"], states[i], want_final,
                                 name=f"gdnscan{tag}")
            mode, z_blk = "gdn", 2
        y = res[0].reshape(m, 2048)
        if want_final:
            finals.append((res[1], res[2]))
        xt = _out_call(mode, y, proj, z_blk, lw["norm_w"], lw["w_out"], xt, mod(2), lw["ln_g0"], lw["ln_b0"],
                       rows_per_mod, tm=256, name=f"outproj{i}{tag}")
        act = _ffn_up_conv_call(xt, mod(4), mod(3), lw["w_up"], lw["ffn_conv_w"], lw["ffn_conv_b"],
                                seqlen, is_grid, rows_per_mod, tm=512, name=f"ffnup{i}{tag}")
        xt = _down_call(act, lw["w_down"], xt, mod(5), lw["ln_g1"], lw["ln_b1"], rows_per_mod, tm=512,
                        name=f"ffndown{i}{tag}")
    return xt.reshape(bsz, seqlen, D_MODEL), finals


def kernel(x_prompt, x_sample, state_ssd_fwd, state_ssd_bwd, state_gdn_fwd, state_gdn_bwd, c, c_ctx, w_ada, b_ada, ln_g, ln_b, ssd_w_in, ssd_conv_w, ssd_conv_b, ssd_dt_bias, ssd_a_log, ssd_d, ssd_norm_w, ssd_w_out, gdn_w_in, gdn_conv_w, gdn_dt_bias, gdn_a_log, gdn_norm_w, gdn_w_out, ffn_w_up, ffn_conv_w, ffn_conv_b, ffn_w_down):
    n_dec = c.shape[0]
    cond = jnp.concatenate([c_ctx[None, :], c, jnp.zeros((16 - 1 - n_dec, D_MODEL), F32)], axis=0)
    ada = _ada_call(cond, w_ada, b_ada).reshape(DEPTH, 16, 6, D_MODEL)
    mods_p = ada[:, 0:1]
    mods_s = ada[:, 1:1 + n_dec]

    def small_cols(wm):
        return jnp.pad(wm, ((0, 0), (0, 128 - wm.shape[1]))).astype(BF16)

    weights = []
    for i in range(DEPTH):
        j = i // 2
        lw = {
            "ln_g0": ln_g[i, 0:1], "ln_b0": ln_b[i, 0:1], "ln_g1": ln_g[i, 1:2], "ln_b1": ln_b[i, 1:2],
            "w_up": ffn_w_up[i].astype(BF16), "w_down": ffn_w_down[i].astype(BF16),
            "ffn_conv_w": ffn_conv_w[i].reshape(9, D_FF), "ffn_conv_b": ffn_conv_b[i].reshape(1, D_FF),
        }
        if i % 2 == 0:
            lw["w_in"] = ssd_w_in[j][:, :6144].astype(BF16)
            lw["w_in_small"] = small_cols(ssd_w_in[j][:, 6144:])
            lw["conv_w"] = ssd_conv_w[j]
            lw["conv_b"] = ssd_conv_b[j].reshape(1, SSD_XBC)
            lw["prm"] = jnp.stack([_group_params(ssd_dt_bias[j]), _group_params(ssd_a_log[j])], axis=-1)
            lw["dskip"] = jnp.repeat(ssd_d[j], SSD_HEADDIM).reshape(SSD_GROUPS, 1, 256)
            lw["norm_w"] = ssd_norm_w[j].reshape(1, SSD_D_INNER)
            lw["w_out"] = ssd_w_out[j].astype(BF16)
        else:
            lw["w_in"] = gdn_w_in[j][:, :6144].astype(BF16)
            lw["w_in_small"] = small_cols(gdn_w_in[j][:, 6144:])
            lw["conv_w"] = gdn_conv_w[j]
            prm = jnp.stack([_group_params(gdn_dt_bias[j]), _group_params(gdn_a_log[j])], axis=-1)
            lw["prm"] = jnp.pad(prm, ((0, 0), (0, 4), (0, 0)))
            lw["norm_w"] = gdn_norm_w[j].reshape(1, GDN_HEAD)
            lw["w_out"] = gdn_w_out[j].astype(BF16)
        weights.append(lw)

    states_s = [(state_ssd_fwd[:, 0], state_ssd_bwd[:, 0]), (state_gdn_fwd[:, 0], state_gdn_bwd[:, 0])]
    y_prompt, fin = _run_tokens(x_prompt, x_prompt.shape[1], mods_p, [None, None], True, weights, "p")
    y_sample, _ = _run_tokens(x_sample, x_sample.shape[1], mods_s, states_s, False, weights, "s")
    return (y_prompt, y_sample,
            fin[0][0][:, None], fin[0][1][:, None], fin[1][0][:, None], fin[1][1][:, None])
```

```python
import functools
import math

import jax
import jax.numpy as jnp
from jax import lax
from jax.experimental import pallas as pl
from jax.experimental.pallas import tpu as pltpu

F32 = jnp.float32
BF16 = jnp.bfloat16
HI = lax.Precision.HIGHEST

D_MODEL = 1024
DEPTH = 2
GRID_W = 64

SSD_D_INNER = 2048
SSD_HEADDIM = 64
SSD_HEADS = 32
SSD_GROUPS = 8
SSD_STATE = 128
SSD_CHUNK = 128
SSD_BLOCKS = 4
SSD_XBC = 4096

GDN_K_HEADS = 8
GDN_V_HEADS = 16
GDN_HEAD = 128
GDN_QK = 1024
GDN_V = 2048
GDN_QKV = 4096
GDN_BLOCK = 128
GDN_CHAINS = 16

D_FF = 2816
FFN_TC = 256
HALO = 128
CONV_STRIP = 64

DEEPNORM_ALPHA = (2.0 * DEPTH) ** 0.25
LN_EPS = 1e-5
RMS_EPS = 1e-6

VMEM_LIMIT = 56 * 1024 * 1024
BF16_ROWS = 16


def _silu(x):
    h = 0.5 * x
    return h + h * jnp.tanh(h)


def _sigmoid(x):
    return 0.5 + 0.5 * jnp.tanh(0.5 * x)


def _softplus(x):
    return jnp.maximum(x, 0.0) + jnp.log(1.0 + jnp.exp(-jnp.abs(x)))


def _params(*sem):
    return pltpu.CompilerParams(dimension_semantics=sem, vmem_limit_bytes=VMEM_LIMIT)


def _ada_kernel(c_ref, w_ref, b_ref, o_ref):
    h = _silu(c_ref[...]).astype(BF16)
    o_ref[...] = jnp.dot(h, w_ref[...].astype(BF16), preferred_element_type=F32) + b_ref[...]


def _ada_call(cond, w_ada, b_ada):
    rows = cond.shape[0]
    tn = 1536
    return pl.pallas_call(
        _ada_kernel,
        grid=(DEPTH, 6 * D_MODEL // tn),
        in_specs=[
            pl.BlockSpec((rows, D_MODEL), lambda l, j: (0, 0)),
            pl.BlockSpec((None, D_MODEL, tn), lambda l, j: (l, 0, j)),
            pl.BlockSpec((None, 1, tn), lambda l, j: (l, 0, j)),
        ],
        out_specs=pl.BlockSpec((None, rows, tn), lambda l, j: (l, 0, j)),
        out_shape=jax.ShapeDtypeStruct((DEPTH, rows, 6 * D_MODEL), F32),
        compiler_params=_params("arbitrary", "arbitrary"),
        name="adaln",
    )(cond, w_ada, b_ada.reshape(DEPTH, 1, 6 * D_MODEL))


def _proj_kernel(*refs, has_small):
    if has_small:
        x_ref, sc_ref, sh_ref, w_ref, w2_ref, o_ref, o2_ref, h_ref = refs
    else:
        x_ref, sc_ref, sh_ref, w_ref, o_ref, h_ref = refs

    @pl.when(pl.program_id(1) == 0)
    def _():
        h = x_ref[...] * (1.0 + sc_ref[...]) + sh_ref[...]
        h_ref[...] = h.astype(BF16)
        if has_small:
            o2_ref[...] = jnp.dot(h_ref[...], w2_ref[...], preferred_element_type=F32)

    o_ref[...] = jnp.dot(h_ref[...], w_ref[...], preferred_element_type=F32).astype(o_ref.dtype)


def _proj_call(x, scale, shift, w, w_small, rows_per_mod, tm, tn, name):
    m = x.shape[0]
    n = w.shape[1]
    has_small = w_small is not None
    mod_spec = pl.BlockSpec((None, 1, D_MODEL), lambda i, j: ((i * tm) // rows_per_mod, 0, 0))
    in_specs = [pl.BlockSpec((tm, D_MODEL), lambda i, j: (i, 0)), mod_spec, mod_spec,
                pl.BlockSpec((D_MODEL, tn), lambda i, j: (0, j))]
    out_specs = [pl.BlockSpec((tm, tn), lambda i, j: (i, j))]
    out_shape = [jax.ShapeDtypeStruct((m, n), BF16)]
    args = [x, scale, shift, w]
    if has_small:
        in_specs.append(pl.BlockSpec((D_MODEL, 128), lambda i, j: (0, 0)))
        out_specs.append(pl.BlockSpec((tm, 128), lambda i, j: (i, 0)))
        out_shape.append(jax.ShapeDtypeStruct((m, 128), F32))
        args.append(w_small)
    return pl.pallas_call(
        functools.partial(_proj_kernel, has_small=has_small),
        grid=(m // tm, n // tn),
        in_specs=in_specs,
        out_specs=out_specs,
        out_shape=out_shape,
        scratch_shapes=[pltpu.VMEM((tm, D_MODEL), BF16)],
        compiler_params=_params("arbitrary", "arbitrary"),
        name=name,
    )(*args)


def _conv_silu_rows(src_ref, r0, n_rows, seqlen, w_ref, bias):
    cur = src_ref[pl.ds(r0, n_rows), :].astype(F32)
    has_prev = (r0 > 0).astype(F32)
    has_next = (r0 + n_rows < seqlen).astype(F32)
    g_prev = pl.multiple_of(jnp.maximum(r0 - BF16_ROWS, 0), BF16_ROWS)
    g_next = pl.multiple_of(jnp.minimum(r0 + n_rows, seqlen - BF16_ROWS), BF16_ROWS)
    prev_row = src_ref[pl.ds(g_prev, BF16_ROWS), :].astype(F32)[BF16_ROWS - 1:BF16_ROWS] * has_prev
    next_row = src_ref[pl.ds(g_next, BF16_ROWS), :].astype(F32)[0:1] * has_next
    rid = lax.broadcasted_iota(jnp.int32, cur.shape, 0)
    xm1 = jnp.where(rid == 0, prev_row, pltpu.roll(cur, 1, 0))
    xp1 = jnp.where(rid == n_rows - 1, next_row, pltpu.roll(cur, n_rows - 1, 0))
    out = w_ref[0:1, :] * xm1 + w_ref[1:2, :] * cur + w_ref[2:3, :] * xp1
    if bias is not None:
        out = out + bias
    return _silu(out)


def _transpose_rows(rows_list):
    pad = 128 - 8 * len(rows_list)
    stack = jnp.concatenate(rows_list + [jnp.zeros((pad, 128), F32)], axis=0)
    return stack.T


def _ssd_scan_kernel(*refs, seqlen, nseq, has_init, want_final):
    refs = list(refs)
    x_ref, b_ref, c_ref, cwx, cwb, cwc, cbx, cbb, cbc, dtt_ref, prm_ref, dsk_ref = refs[:12]
    pos = 12
    if has_init:
        s0_refs = refs[pos:pos + 2]
        pos += 2
    yout_ref = refs[pos]
    pos += 1
    if want_final:
        sfin_refs = refs[pos:pos + 2]
        pos += 2
    cs_b, ea_s, ns_s, st, y_ref = refs[pos:pos + 5]

    nc = seqlen // SSD_CHUNK
    q = SSD_CHUNK

    for g in range(nseq):
        for d in range(2):
            if has_init:
                halves = []
                for h in range(2):
                    blk = jnp.concatenate([s0_refs[d][g, 2 * h], s0_refs[d][g, 2 * h + 1]], axis=0)
                    halves.append(blk.T)
                st[g, d] = jnp.concatenate(halves, axis=1)
            else:
                st[g, d] = jnp.zeros((SSD_STATE, 256), F32)

    row = lax.broadcasted_iota(jnp.int32, (q, q), 0)
    col = lax.broadcasted_iota(jnp.int32, (q, q), 1)
    lane_head = lax.broadcasted_iota(jnp.int32, (q, 256), 1) // SSD_HEADDIM
    bias = prm_ref[:, 0:1]
    a_neg = -jnp.exp(prm_ref[:, 1:2])
    dskip = dsk_ref[...]

    cum_ms = ((row <= col).astype(F32), (row >= col).astype(F32))
    keeps = (row >= col, row <= col)
    lasts = (q - 1, 0)

    def intra(i, carry):
        blocks = []
        for g, j in ((g, j) for g in range(nseq) for j in range(unroll_a)):
            r0 = pl.multiple_of((i * unroll_a + j) * q, q)
            xc = _conv_silu_rows(x_ref.at[g], r0, q, seqlen, cwx, cbx[...])
            bc = _conv_silu_rows(b_ref.at[g], r0, q, seqlen, cwb, cbb[...])
            cc = _conv_silu_rows(c_ref.at[g], r0, q, seqlen, cwc, cbc[...])
            cc_b = cc.astype(BF16)
            cs_b[g, pl.ds(r0, q), :] = cc_b
            bc_t = bc.T
            dt_t = _softplus(dtt_ref[g, :, pl.ds(r0, q)] + bias)
            x_heads = jnp.concatenate([jnp.where(lane_head == r, xc, 0.0).astype(BF16) for r in range(4)], axis=0)
            blocks.append(dict(g=g, r0=r0, xc=xc, bc_t=bc_t, dt_t=dt_t, a_t=dt_t * a_neg, x_heads=x_heads,
                               scores=jnp.dot(cc_b, bc_t.astype(BF16), preferred_element_type=F32)))
        units = [(blk, d) for blk in blocks for d in range(2)]
        acums = [jnp.dot(blk["a_t"], cum_ms[d], precision=HI, preferred_element_type=F32) for blk, d in units]
        cols = _transpose_rows(acums)
        dtdes = [blk["dt_t"] * jnp.exp(ac[:, lasts[d]:lasts[d] + 1] - ac) for (blk, d), ac in zip(units, acums)]
        for (blk, d), ac in zip(units, acums):
            ea_t = jnp.exp(ac)
            ea_rows = jnp.concatenate([jnp.broadcast_to(ea_t[4 * d + r:4 * d + r + 1, :], (SSD_HEADDIM, q))
                                       for r in range(4)], axis=0)
            ea_s[blk["g"], d, pl.ds(blk["r0"], q), :] = jnp.concatenate([ea_rows[0:128].T, ea_rows[128:256].T], axis=1)
        lhss = []
        for u, ((blk, d), ac, dtde) in enumerate(zip(units, acums, dtdes)):
            parts = []
            for r in range(4):
                k = 4 * d + r
                decay = jnp.where(keeps[d], jnp.exp(cols[:, 8 * u + k:8 * u + k + 1] - ac[k:k + 1, :]), 0.0)
                m = decay * blk["scores"] * blk["dt_t"][k:k + 1, :]
                bt = blk["bc_t"] * dtde[k:k + 1, :]
                parts.append(jnp.concatenate([m, bt], axis=0).astype(BF16))
            lhss.append(jnp.concatenate(parts, axis=1))
        ress = [jnp.dot(lhs, blk["x_heads"], preferred_element_type=F32) for lhs, (blk, d) in zip(lhss, units)]
        for res, (blk, d) in zip(ress, units):
            ns_s[blk["g"], d, pl.ds(blk["r0"], q), :] = res[q:]
        for j, blk in enumerate(blocks):
            y_ref[blk["g"], pl.ds(blk["r0"], q), :] = blk["xc"] * dskip + ress[2 * j][:q] + ress[2 * j + 1][:q]
        return carry

    unroll_a = max(1, SSD_BLOCKS // nseq)
    while nc % unroll_a:
        unroll_a //= 2
    lax.fori_loop(0, nc // unroll_a, intra, 0)

    def inter(i, carry):
        jobs = []
        for g, d in ((g, d) for g in range(nseq) for d in range(2)):
            s_t = st[g, d]
            for j in range(unroll_a):
                c = i * unroll_a + j
                r0 = pl.multiple_of((c if d == 0 else nc - 1 - c) * q, q)
                jobs.append((g, d, r0, s_t.astype(BF16)))
                s_t = s_t * ea_s[g, d, pl.ds(r0 + lasts[d], 1), :] + ns_s[g, d, pl.ds(r0, q), :]
            st[g, d] = s_t
        y_inters = [jnp.dot(cs_b[g, pl.ds(r0, q), :], s_b, preferred_element_type=F32) * ea_s[g, d, pl.ds(r0, q), :]
                    for g, d, r0, s_b in jobs]
        for (g, d, r0, _), y_inter in zip(jobs, y_inters):
            y_ref[g, pl.ds(r0, q), :] += y_inter
        return carry

    lax.fori_loop(0, nc // unroll_a, inter, 0)

    def emit(c, carry):
        r0 = pl.multiple_of(c * q, q)
        for g in range(nseq):
            yout_ref[g, pl.ds(r0, q), :] = y_ref[g, pl.ds(r0, q), :].astype(yout_ref.dtype)
        return carry

    lax.fori_loop(0, nc, emit, 0)

    if want_final:
        for g, d in ((g, d) for g in range(nseq) for d in range(2)):
            s_t = st[g, d]
            for h in range(2):
                blk = s_t[:, h * 128:(h + 1) * 128].T
                sfin_refs[d][g, 2 * h] = blk[0:64]
                sfin_refs[d][g, 2 * h + 1] = blk[64:128]


def _ssd_scan_call(proj3, dtt, conv_w, conv_b, prm, dskip, s0, want_final, name):
    bsz, seqlen, _ = proj3.shape
    has_init = s0 is not None
    nseq = max(1, min(bsz, SSD_BLOCKS * SSD_CHUNK // seqlen))
    while bsz % nseq:
        nseq -= 1
    xoff = SSD_D_INNER // 256
    boff = (2 * SSD_D_INNER) // 128
    coff = boff + SSD_GROUPS
    in_specs = [
        pl.BlockSpec((nseq, seqlen, 256), lambda b, g: (b, 0, xoff + g)),
        pl.BlockSpec((nseq, seqlen, 128), lambda b, g: (b, 0, boff + g)),
        pl.BlockSpec((nseq, seqlen, 128), lambda b, g: (b, 0, coff + g)),
        pl.BlockSpec((3, 256), lambda b, g: (0, g)),
        pl.BlockSpec((3, 128), lambda b, g: (0, 16 + g)),
        pl.BlockSpec((3, 128), lambda b, g: (0, 24 + g)),
        pl.BlockSpec((1, 256), lambda b, g: (0, g)),
        pl.BlockSpec((1, 128), lambda b, g: (0, 16 + g)),
        pl.BlockSpec((1, 128), lambda b, g: (0, 24 + g)),
        pl.BlockSpec((nseq, None, 8, seqlen), lambda b, g: (b, g, 0, 0)),
        pl.BlockSpec((None, 8, 2), lambda b, g: (g, 0, 0)),
        pl.BlockSpec((None, 1, 256), lambda b, g: (g, 0, 0)),
    ]
    args = [proj3, proj3, proj3, conv_w, conv_w, conv_w, conv_b, conv_b, conv_b, dtt, prm, dskip]
    state_spec = pl.BlockSpec((nseq, 4, SSD_HEADDIM, SSD_STATE), lambda b, g: (b, g, 0, 0))
    if has_init:
        in_specs += [state_spec, state_spec]
        args += [s0[0], s0[1]]
    out_specs = [pl.BlockSpec((nseq, seqlen, 256), lambda b, g: (b, 0, g))]
    out_shape = [jax.ShapeDtypeStruct((bsz, seqlen, SSD_D_INNER), BF16)]
    if want_final:
        out_specs += [state_spec, state_spec]
        out_shape += [jax.ShapeDtypeStruct((bsz, SSD_HEADS, SSD_HEADDIM, SSD_STATE), F32)] * 2
    return pl.pallas_call(
        functools.partial(_ssd_scan_kernel, seqlen=seqlen, nseq=nseq, has_init=has_init, want_final=want_final),
        grid=(bsz // nseq, SSD_GROUPS),
        in_specs=in_specs,
        out_specs=out_specs,
        out_shape=out_shape,
        scratch_shapes=[pltpu.VMEM((nseq, seqlen, SSD_STATE), BF16),
                        pltpu.VMEM((nseq, 2, seqlen, 256), F32),
                        pltpu.VMEM((nseq, 2, seqlen, 256), F32),
                        pltpu.VMEM((nseq, 2, SSD_STATE, 256), F32),
                        pltpu.VMEM((nseq, seqlen, 256), F32)],
        compiler_params=_params("arbitrary", "arbitrary"),
        name=name,
    )(*args)


def _unit_tri_solve(a_list, rhs_list, merge_masks):
    xs = a_list
    for m in merge_masks:
        xs_b = [x.astype(BF16) for x in xs]
        xs = [x - jnp.dot(x_b, x_b * m, preferred_element_type=F32) for x, x_b in zip(xs, xs_b)]
    return [rhs - jnp.dot(x.astype(BF16), rhs.astype(BF16), preferred_element_type=F32)
            for x, rhs in zip(xs, rhs_list)]


def _gdn_scan_kernel(*refs, seqlen, nseq, has_init, want_final):
    refs = list(refs)
    q_ref, k_ref, v_ref, cwq, cwk, cwv, smt_ref, prm_ref = refs[:8]
    pos = 8
    if has_init:
        s0_refs = refs[pos:pos + 2]
        pos += 2
    oout_ref = refs[pos]
    pos += 1
    if want_final:
        sfin_refs = refs[pos:pos + 2]
        pos += 2
    u_s, wq_s, qkm_s, kdt_s, gl_s, st, o_ref = refs[pos:pos + 7]

    blk = GDN_BLOCK
    nb = seqlen // blk
    qk_scale = GDN_HEAD ** -0.5

    for g in range(nseq):
        for d in range(2):
            for r in range(2):
                if has_init:
                    st[g, d, r] = s0_refs[d][g, r]
                else:
                    st[g, d, r] = jnp.zeros((GDN_HEAD, GDN_HEAD), F32)

    row = lax.broadcasted_iota(jnp.int32, (blk, blk), 0)
    col = lax.broadcasted_iota(jnp.int32, (blk, blk), 1)
    merge_masks = [jnp.where(((row >> (l + 1)) == (col >> (l + 1))) & ((row >> l) != (col >> l)), 1.0, 0.0
                             ).astype(BF16) for l in range(7)]
    bias = prm_ref[:, 0:1]
    a_neg = -jnp.exp(prm_ref[:, 1:2])
    nt = (((1,), (1,)), ((), ()))

    cum_ms = ((row <= col).astype(F32), (row >= col).astype(F32))
    stricts = (row > col, row < col)
    incls = (row >= col, row <= col)
    lasts = (blk - 1, 0)

    def load_block(g, c):
        r0 = pl.multiple_of(c * blk, blk)
        qv = _conv_silu_rows(q_ref.at[g], r0, blk, seqlen, cwq, None)
        kv = _conv_silu_rows(k_ref.at[g], r0, blk, seqlen, cwk, None)
        qc = qv * (lax.rsqrt(jnp.sum(qv * qv, axis=-1, keepdims=True) + RMS_EPS) * qk_scale)
        kc = kv * lax.rsqrt(jnp.sum(kv * kv, axis=-1, keepdims=True) + RMS_EPS)
        vc = _conv_silu_rows(v_ref.at[g], r0, blk, seqlen, cwv, None)
        o_ref[g, pl.ds(r0, blk), :] = jnp.zeros((blk, 256), F32)
        sm = smt_ref[g, :, pl.ds(r0, blk)]
        return dict(g=g, r0=r0, wq0=pl.multiple_of(c * 2 * blk, 2 * blk), g0=pl.multiple_of(c * 8, 8),
                    qc=qc, kc=kc, vc=vc, g_t=a_neg * _softplus(sm + bias), beta_t=_sigmoid(sm))

    def prepare(blocks):
        for b in blocks:
            kc_b = b["kc"].astype(BF16)
            b["kk"] = lax.dot_general(kc_b, kc_b, nt, preferred_element_type=F32)
            b["qk"] = lax.dot_general(b["qc"].astype(BF16), kc_b, nt, preferred_element_type=F32)
        units = [(b, d) for b in blocks for d in range(2)]
        gcs = [jnp.dot(b["g_t"], cum_ms[d], precision=HI, preferred_element_type=F32) for b, d in units]
        tots = [gc[:, lasts[d]:lasts[d] + 1] for (b, d), gc in zip(units, gcs)]
        colss = [_transpose_rows([gc, jnp.exp(gc), b["beta_t"]]) for (b, d), gc in zip(units, gcs)]
        for (b, d), tot in zip(units, tots):
            gl_s[b["g"], d, pl.ds(b["g0"], 8), :] = jnp.broadcast_to(jnp.exp(tot), (8, blk))
        for b in blocks:
            b["kc_t"] = b["kc"].T
        chains = []
        for (b, d), gc, tot, cols in zip(units, gcs, tots, colss):
            kd_t = jnp.exp(tot - gc)
            for r in range(2):
                k = 2 * d + r
                e = jnp.exp(cols[:, k:k + 1] - gc[k:k + 1, :])
                beta = jnp.broadcast_to(cols[:, 20 + k:21 + k], (blk, GDN_HEAD))
                eg = jnp.broadcast_to(cols[:, 8 + k:9 + k], (blk, GDN_HEAD))
                vr = b["vc"][:, r * 128:(r + 1) * 128]
                g = b["g"]
                wq_s[g, d, r, pl.ds(b["wq0"] + blk, blk), :] = (b["qc"] * eg).astype(BF16)
                qkm_s[g, d, r, pl.ds(b["r0"], blk), :] = (b["qk"] * jnp.where(incls[d], e, 0.0)).astype(BF16)
                kdt_s[g, d, r, pl.ds(b["r0"], blk), :] = (b["kc_t"] * kd_t[k:k + 1, :]).astype(BF16)
                chains.append(dict(
                    g=g, d=d, r=r, r0=b["r0"], wq0=b["wq0"],
                    a=beta * b["kk"] * jnp.where(stricts[d], e, 0.0),
                    rhs=jnp.concatenate([vr * beta, b["kc"] * (beta * eg)], axis=1)))
        return chains

    chains_per_block = 4
    per_iter = max(1, GDN_CHAINS // (chains_per_block * nseq))
    while nb % per_iter:
        per_iter //= 2

    def phase1(i, carry):
        chains = prepare([load_block(g, i * per_iter + j) for g in range(nseq) for j in range(per_iter)])
        sols = _unit_tri_solve([ch["a"] for ch in chains], [ch["rhs"] for ch in chains], merge_masks)
        for sol, ch in zip(sols, chains):
            u_s[ch["g"], ch["d"], ch["r"], pl.ds(ch["r0"], blk), :] = sol[:, :GDN_HEAD]
            wq_s[ch["g"], ch["d"], ch["r"], pl.ds(ch["wq0"], blk), :] = sol[:, GDN_HEAD:].astype(BF16)
        return carry

    lax.fori_loop(0, nb // per_iter, phase1, 0)

    def phase2(i, carry):
        jobs = []
        for g in range(nseq):
            for d, c in ((0, i), (1, nb - 1 - i)):
                r0 = pl.multiple_of(c * blk, blk)
                wq0 = pl.multiple_of(c * 2 * blk, 2 * blk)
                gl = gl_s[g, d, pl.ds(pl.multiple_of(c * 8, 8), 8), :]
                for r in range(2):
                    jobs.append(((g, d, r), r0, wq0, gl[2 * d + r:2 * d + r + 1, :]))
        ss = [st[idx] for idx, _, _, _ in jobs]
        wss = [jnp.dot(wq_s[idx + (pl.ds(wq0, 2 * blk), slice(None))], s.astype(BF16), preferred_element_type=F32)
               for (idx, _, wq0, _), s in zip(jobs, ss)]
        vns = [(u_s[idx + (pl.ds(r0, blk), slice(None))] - ws[:blk]).astype(BF16)
               for (idx, r0, _, _), ws in zip(jobs, wss)]
        for (idx, r0, _, gl), s, vn in zip(jobs, ss, vns):
            st[idx] = gl * s + jnp.dot(kdt_s[idx + (pl.ds(r0, blk), slice(None))], vn, preferred_element_type=F32)
        outs = [ws[blk:] + jnp.dot(qkm_s[idx + (pl.ds(r0, blk), slice(None))], vn, preferred_element_type=F32)
                for (idx, r0, _, _), ws, vn in zip(jobs, wss, vns)]
        for pair in range(2 * nseq):
            (g, _, _), r0 = jobs[2 * pair][0], jobs[2 * pair][1]
            o_ref[g, pl.ds(r0, blk), :] += jnp.concatenate(outs[2 * pair:2 * pair + 2], axis=1)
        return carry

    lax.fori_loop(0, nb, phase2, 0)

    def emit(c, carry):
        r0 = pl.multiple_of(c * blk, blk)
        for g in range(nseq):
            oout_ref[g, pl.ds(r0, blk), :] = o_ref[g, pl.ds(r0, blk), :].astype(oout_ref.dtype)
        return carry

    lax.fori_loop(0, nb, emit, 0)

    if want_final:
        for g in range(nseq):
            for d in range(2):
                for r in range(2):
                    sfin_refs[d][g, r] = st[g, d, r]


def _gdn_scan_call(proj3, smt, conv_w, prm, s0, want_final, name):
    bsz, seqlen, _ = proj3.shape
    has_init = s0 is not None
    nseq = max(1, min(bsz, GDN_CHAINS * GDN_BLOCK // (4 * seqlen)))
    while bsz % nseq:
        nseq -= 1
    in_specs = [
        pl.BlockSpec((nseq, seqlen, 128), lambda b, g: (b, 0, g)),
        pl.BlockSpec((nseq, seqlen, 128), lambda b, g: (b, 0, GDN_K_HEADS + g)),
        pl.BlockSpec((nseq, seqlen, 256), lambda b, g: (b, 0, GDN_K_HEADS + g)),
        pl.BlockSpec((3, 128), lambda b, g: (0, g)),
        pl.BlockSpec((3, 128), lambda b, g: (0, GDN_K_HEADS + g)),
        pl.BlockSpec((3, 256), lambda b, g: (0, GDN_K_HEADS + g)),
        pl.BlockSpec((nseq, None, 8, seqlen), lambda b, g: (b, g, 0, 0)),
        pl.BlockSpec((None, 8, 2), lambda b, g: (g, 0, 0)),
    ]
    args = [proj3, proj3, proj3, conv_w, conv_w, conv_w, smt, prm]
    state_spec = pl.BlockSpec((nseq, 2, GDN_HEAD, GDN_HEAD), lambda b, g: (b, g, 0, 0))
    if has_init:
        in_specs += [state_spec, state_spec]
        args += [s0[0], s0[1]]
    out_specs = [pl.BlockSpec((nseq, seqlen, 256), lambda b, g: (b, 0, g))]
    out_shape = [jax.ShapeDtypeStruct((bsz, seqlen, GDN_V), BF16)]
    if want_final:
        out_specs += [state_spec, state_spec]
        out_shape += [jax.ShapeDtypeStruct((bsz, GDN_V_HEADS, GDN_HEAD, GDN_HEAD), F32)] * 2
    return pl.pallas_call(
        functools.partial(_gdn_scan_kernel, seqlen=seqlen, nseq=nseq, has_init=has_init, want_final=want_final),
        grid=(bsz // nseq, GDN_K_HEADS),
        in_specs=in_specs,
        out_specs=out_specs,
        out_shape=out_shape,
        scratch_shapes=[pltpu.VMEM((nseq, 2, 2, seqlen, GDN_HEAD), F32),
                        pltpu.VMEM((nseq, 2, 2, 2 * seqlen, GDN_HEAD), BF16),
                        pltpu.VMEM((nseq, 2, 2, seqlen, GDN_BLOCK), BF16),
                        pltpu.VMEM((nseq, 2, 2, seqlen, GDN_BLOCK), BF16),
                        pltpu.VMEM((nseq, 2, 8 * (seqlen // GDN_BLOCK), GDN_BLOCK), F32),
                        pltpu.VMEM((nseq, 2, 2, GDN_HEAD, GDN_HEAD), F32),
                        pltpu.VMEM((nseq, seqlen, 256), F32)],
        compiler_params=_params("arbitrary", "arbitrary"),
        name=name,
    )(*args)


def _residual_layer_norm(x, gate, o, ln_g, ln_b):
    r = DEEPNORM_ALPHA * x + gate * o
    mu = jnp.mean(r, axis=-1, keepdims=True)
    rc = r - mu
    var = jnp.mean(rc * rc, axis=-1, keepdims=True)
    return rc * lax.rsqrt(var + LN_EPS) * ln_g + ln_b


def _out_kernel(y_ref, z_ref, nw_ref, w_ref, x_ref, gt_ref, lg_ref, lb_ref, o_ref, *, mode):
    if mode == "ssd":
        u = y_ref[...].astype(F32) * _silu(z_ref[...].astype(F32))
        ms = jnp.mean(u * u, axis=-1, keepdims=True)
        a = (u * lax.rsqrt(ms + RMS_EPS) * nw_ref[...]).astype(BF16)
    else:
        parts = []
        for h in range(GDN_V_HEADS):
            sl = slice(h * GDN_HEAD, (h + 1) * GDN_HEAD)
            oh = y_ref[:, sl].astype(F32)
            ms = jnp.mean(oh * oh, axis=-1, keepdims=True)
            parts.append((oh * lax.rsqrt(ms + RMS_EPS) * nw_ref[...] * _silu(z_ref[:, sl].astype(F32))).astype(BF16))
        a = jnp.concatenate(parts, axis=1)
    o = jnp.dot(a, w_ref[...], preferred_element_type=F32)
    o_ref[...] = _residual_layer_norm(x_ref[...], gt_ref[...], o, lg_ref[...], lb_ref[...])


def _out_call(mode, act, z_src, z_blk, norm_w, w, x, gate, ln_g, ln_b, rows_per_mod, tm, name):
    m, kdim = act.shape
    row_spec = pl.BlockSpec((tm, D_MODEL), lambda i: (i, 0))
    vec_spec = pl.BlockSpec((1, D_MODEL), lambda i: (0, 0))
    in_specs = [pl.BlockSpec((tm, kdim), lambda i: (i, 0)),
                pl.BlockSpec((tm, kdim), lambda i: (i, z_blk)),
                pl.BlockSpec((1, norm_w.shape[1]), lambda i: (0, 0)),
                pl.BlockSpec((kdim, D_MODEL), lambda i: (0, 0)), row_spec,
                pl.BlockSpec((None, 1, D_MODEL), lambda i: ((i * tm) // rows_per_mod, 0, 0)),
                vec_spec, vec_spec]
    args = [act, z_src, norm_w, w, x, gate, ln_g, ln_b]
    return pl.pallas_call(
        functools.partial(_out_kernel, mode=mode),
        grid=(m // tm,),
        in_specs=in_specs,
        out_specs=row_spec,
        out_shape=jax.ShapeDtypeStruct((m, D_MODEL), F32),
        compiler_params=_params("arbitrary"),
        name=name,
    )(*args)


def _ffn_down_kernel(*refs, tm, period, row_taps, tiles_per_seq):
    if row_taps:
        (a_ref, ap_ref, an_ref, v_ref, cw_ref, cb_ref, w_ref, x_ref, gt_ref, lg_ref, lb_ref,
         o_ref, acc_ref, act_ref, ext_ref, em1_ref, ep1_ref) = refs
        t = pl.program_id(0) % tiles_per_seq
        has_prev = (t > 0).astype(F32)
        has_next = (t < tiles_per_seq - 1).astype(F32)
        base, taps, n = HALO, (0, 1, 2), tm + 2 * HALO
    else:
        (a_ref, v_ref, cw_ref, cb_ref, w_ref, x_ref, gt_ref, lg_ref, lb_ref,
         o_ref, acc_ref, act_ref, ext_ref, em1_ref, ep1_ref) = refs
        base, taps, n = 0, (1,), tm
    colpos = lax.broadcasted_iota(jnp.int32, (n, FFN_TC), 0) % period
    is_first = colpos == 0
    is_last = colpos == period - 1
    n_ct = D_FF // FFN_TC

    def gated_tile(c):
        cols = pl.ds(pl.multiple_of(c * FFN_TC, FFN_TC), FFN_TC)
        if row_taps:
            ext_ref[0:HALO, :] = ap_ref[:, cols].astype(F32) * has_prev
            ext_ref[HALO + tm:, :] = an_ref[:, cols].astype(F32) * has_next
        ext_ref[base:base + tm, :] = a_ref[:, cols].astype(F32)
        ext = ext_ref[...]
        em1_ref[...] = jnp.where(is_first, 0.0, pltpu.roll(ext, 1, 0))
        ep1_ref[...] = jnp.where(is_last, 0.0, pltpu.roll(ext, n - 1, 0))
        weights = [cw_ref[k:k + 1, cols] for k in range(9)]
        bias = cb_ref[:, cols]
        for s in range(0, tm, CONV_STRIP):
            conv = bias
            for dr in taps:
                rows = slice(base + (dr - 1) * GRID_W + s, base + (dr - 1) * GRID_W + s + CONV_STRIP)
                conv = (conv + weights[3 * dr] * em1_ref[rows, :] + weights[3 * dr + 1] * ext_ref[rows, :]
                        + weights[3 * dr + 2] * ep1_ref[rows, :])
            v = v_ref[s:s + CONV_STRIP, cols].astype(F32)
            act_ref[c % 2, s:s + CONV_STRIP, :] = (_silu(conv) * v).astype(BF16)

    def project(c):
        rows = pl.ds(pl.multiple_of(c * FFN_TC, FFN_TC), FFN_TC)
        return jnp.dot(act_ref[c % 2], w_ref[rows, :], preferred_element_type=F32)

    gated_tile(0)
    acc_ref[...] = jnp.zeros((tm, D_MODEL), F32)

    def channel_tile(c, carry):
        acc_ref[...] += project(c - 1)
        gated_tile(c)
        return carry

    lax.fori_loop(1, n_ct, channel_tile, 0)
    o = acc_ref[...] + project(n_ct - 1)
    o_ref[...] = _residual_layer_norm(x_ref[...], gt_ref[...], o, lg_ref[...], lb_ref[...])


def _ffn_down_call(up, conv_w9, conv_b, w_down, x, gate, ln_g, ln_b, seqlen, is_grid, rows_per_mod, tm, name):
    m = up.shape[0]
    a_spec = pl.BlockSpec((tm, D_FF), lambda i: (i, 0))
    v_spec = pl.BlockSpec((tm, D_FF), lambda i: (i, 1))
    n_ext = tm + 2 * HALO if is_grid else tm
    scratch = [pltpu.VMEM((tm, D_MODEL), F32), pltpu.VMEM((2, tm, FFN_TC), BF16)]
    scratch += [pltpu.VMEM((n_ext, FFN_TC), F32)] * 3
    if is_grid:
        hb = tm // HALO
        nhb = m // HALO
        tiles_per_seq = seqlen // tm
        in_specs = [a_spec,
                    pl.BlockSpec((HALO, D_FF), lambda i: (jnp.maximum(i * hb - 1, 0), 0)),
                    pl.BlockSpec((HALO, D_FF), lambda i: (jnp.minimum((i + 1) * hb, nhb - 1), 0)),
                    v_spec]
        args = [up, up, up, up]
        period = GRID_W
    else:
        tiles_per_seq = 1
        in_specs = [a_spec, v_spec]
        args = [up, up]
        period = seqlen
    row_spec = pl.BlockSpec((tm, D_MODEL), lambda i: (i, 0))
    vec_spec = pl.BlockSpec((1, D_MODEL), lambda i: (0, 0))
    in_specs += [pl.BlockSpec((9, D_FF), lambda i: (0, 0)), pl.BlockSpec((1, D_FF), lambda i: (0, 0)),
                 pl.BlockSpec((D_FF, D_MODEL), lambda i: (0, 0)), row_spec,
                 pl.BlockSpec((None, 1, D_MODEL), lambda i: ((i * tm) // rows_per_mod, 0, 0)),
                 vec_spec, vec_spec]
    args += [conv_w9, conv_b, w_down, x, gate, ln_g, ln_b]
    return pl.pallas_call(
        functools.partial(_ffn_down_kernel, tm=tm, period=period, row_taps=is_grid, tiles_per_seq=tiles_per_seq),
        grid=(m // tm,),
        in_specs=in_specs,
        out_specs=row_spec,
        out_shape=jax.ShapeDtypeStruct((m, D_MODEL), F32),
        scratch_shapes=scratch,
        compiler_params=_params("arbitrary"),
        name=name,
    )(*args)


def _group_rows(t, bsz, seqlen, lead):
    n_lead = int(math.prod(lead))
    t = t.reshape(bsz, seqlen, n_lead, 8, -1)
    r = t.shape[-1]
    t = jnp.transpose(t, (0, 3, 2, 4, 1))
    return t.reshape(bsz, 8, n_lead * r, seqlen)


def _group_params(p):
    r = p.shape[1] // 8
    return jnp.transpose(p.reshape(2, 8, r), (1, 0, 2)).reshape(8, 2 * r)


def _run_tokens(x, seqlen, mods, states, want_final, w, tag):
    bsz = x.shape[0]
    m = bsz * seqlen
    nb = mods.shape[1]
    rows_per_mod = m // nb
    xt = x.reshape(m, D_MODEL)
    finals = []
    is_grid = tag == "s"
    for i in range(DEPTH):
        def mod(k):
            return mods[i, :, k].reshape(nb, 1, D_MODEL)
        lw = w[i]
        proj, small = _proj_call(xt, mod(1), mod(0), lw["w_in"], lw["w_in_small"], rows_per_mod,
                                 tm=1024, tn=1536, name=f"inproj{i}{tag}")
        proj3 = proj.reshape(bsz, seqlen, 6144)
        if i % 2 == 0:
            dtt = _group_rows(small[:, :2 * SSD_HEADS], bsz, seqlen, (2,))
            res = _ssd_scan_call(proj3, dtt, lw["conv_w"], lw["conv_b"], lw["prm"], lw["dskip"],
                                 states[i], want_final, name=f"ssdscan{tag}")
            mode, z_blk = "ssd", 0
        else:
            smt = _group_rows(small[:, :4 * GDN_V_HEADS], bsz, seqlen, (2, 2))
            res = _gdn_scan_call(proj3, smt, lw["conv_w"], lw["prm"], states[i], want_final,
                                 name=f"gdnscan{tag}")
            mode, z_blk = "gdn", 2
        y = res[0].reshape(m, 2048)
        if want_final:
            finals.append((res[1], res[2]))
        xt = _out_call(mode, y, proj, z_blk, lw["norm_w"], lw["w_out"], xt, mod(2), lw["ln_g0"], lw["ln_b0"],
                       rows_per_mod, tm=256, name=f"outproj{i}{tag}")
        up = _proj_call(xt, mod(4), mod(3), lw["w_up"], None, rows_per_mod,
                        tm=1024, tn=1408, name=f"ffnup{i}{tag}")[0]
        xt = _ffn_down_call(up, lw["ffn_conv_w"], lw["ffn_conv_b"], lw["w_down"], xt, mod(5),
                            lw["ln_g1"], lw["ln_b1"], seqlen, is_grid, rows_per_mod, tm=512,
                            name=f"ffndown{i}{tag}")
    return xt.reshape(bsz, seqlen, D_MODEL), finals


def kernel(x_prompt, x_sample, state_ssd_fwd, state_ssd_bwd, state_gdn_fwd, state_gdn_bwd, c, c_ctx, w_ada, b_ada, ln_g, ln_b, ssd_w_in, ssd_conv_w, ssd_conv_b, ssd_dt_bias, ssd_a_log, ssd_d, ssd_norm_w, ssd_w_out, gdn_w_in, gdn_conv_w, gdn_dt_bias, gdn_a_log, gdn_norm_w, gdn_w_out, ffn_w_up, ffn_conv_w, ffn_conv_b, ffn_w_down):
    n_dec = c.shape[0]
    cond = jnp.concatenate([c_ctx[None, :], c, jnp.zeros((16 - 1 - n_dec, D_MODEL), F32)], axis=0)
    ada = _ada_call(cond, w_ada, b_ada).reshape(DEPTH, 16, 6, D_MODEL)
    mods_p = ada[:, 0:1]
    mods_s = ada[:, 1:1 + n_dec]

    def small_cols(wm):
        return jnp.pad(wm, ((0, 0), (0, 128 - wm.shape[1]))).astype(BF16)

    weights = []
    for i in range(DEPTH):
        j = i // 2
        lw = {
            "ln_g0": ln_g[i, 0:1], "ln_b0": ln_b[i, 0:1], "ln_g1": ln_g[i, 1:2], "ln_b1": ln_b[i, 1:2],
            "w_up": ffn_w_up[i].astype(BF16), "w_down": ffn_w_down[i].astype(BF16),
            "ffn_conv_w": ffn_conv_w[i].reshape(9, D_FF), "ffn_conv_b": ffn_conv_b[i].reshape(1, D_FF),
        }
        if i % 2 == 0:
            lw["w_in"] = ssd_w_in[j][:, :6144].astype(BF16)
            lw["w_in_small"] = small_cols(ssd_w_in[j][:, 6144:])
            lw["conv_w"] = ssd_conv_w[j]
            lw["conv_b"] = ssd_conv_b[j].reshape(1, SSD_XBC)
            lw["prm"] = jnp.stack([_group_params(ssd_dt_bias[j]), _group_params(ssd_a_log[j])], axis=-1)
            lw["dskip"] = jnp.repeat(ssd_d[j], SSD_HEADDIM).reshape(SSD_GROUPS, 1, 256)
            lw["norm_w"] = ssd_norm_w[j].reshape(1, SSD_D_INNER)
            lw["w_out"] = ssd_w_out[j].astype(BF16)
        else:
            lw["w_in"] = gdn_w_in[j][:, :6144].astype(BF16)
            lw["w_in_small"] = small_cols(gdn_w_in[j][:, 6144:])
            lw["conv_w"] = gdn_conv_w[j]
            prm = jnp.stack([_group_params(gdn_dt_bias[j]), _group_params(gdn_a_log[j])], axis=-1)
            lw["prm"] = jnp.pad(prm, ((0, 0), (0, 4), (0, 0)))
            lw["norm_w"] = gdn_norm_w[j].reshape(1, GDN_HEAD)
            lw["w_out"] = gdn_w_out[j].astype(BF16)
        weights.append(lw)

    states_s = [(state_ssd_fwd[:, 0], state_ssd_bwd[:, 0]), (state_gdn_fwd[:, 0], state_gdn_bwd[:, 0])]
    y_prompt, fin = _run_tokens(x_prompt, x_prompt.shape[1], mods_p, [None, None], True, weights, "p")
    y_sample, _ = _run_tokens(x_sample, x_sample.shape[1], mods_s, states_s, False, weights, "s")
    return (y_prompt, y_sample,
            fin[0][0][:, None], fin[0][1][:, None], fin[1][0][:, None], fin[1][1][:, None])
```

```python
import functools
import math

import jax
import jax.numpy as jnp
from jax import lax
from jax.experimental import pallas as pl
from jax.experimental.pallas import tpu as pltpu

F32 = jnp.float32
BF16 = jnp.bfloat16
HI = lax.Precision.HIGHEST

D_MODEL = 1024
DEPTH = 2
GRID_W = 64

SSD_D_INNER = 2048
SSD_HEADDIM = 64
SSD_HEADS = 32
SSD_GROUPS = 8
SSD_STATE = 128
SSD_CHUNK = 128
SSD_BLOCKS = 4
SSD_XBC = 4096

GDN_K_HEADS = 8
GDN_V_HEADS = 16
GDN_HEAD = 128
GDN_QK = 1024
GDN_V = 2048
GDN_QKV = 4096
GDN_BLOCK = 128
GDN_CHAINS = 16

D_FF = 2816
FFN_TC = 256
HALO = 128

DEEPNORM_ALPHA = (2.0 * DEPTH) ** 0.25
LN_EPS = 1e-5
RMS_EPS = 1e-6

VMEM_LIMIT = 56 * 1024 * 1024
BF16_ROWS = 16


def _silu(x):
    h = 0.5 * x
    return h + h * jnp.tanh(h)


def _sigmoid(x):
    return 0.5 + 0.5 * jnp.tanh(0.5 * x)


def _softplus(x):
    return jnp.maximum(x, 0.0) + jnp.log(1.0 + jnp.exp(-jnp.abs(x)))


def _params(*sem):
    return pltpu.CompilerParams(dimension_semantics=sem, vmem_limit_bytes=VMEM_LIMIT)


def _ada_kernel(c_ref, w_ref, b_ref, o_ref):
    h = _silu(c_ref[...]).astype(BF16)
    o_ref[...] = jnp.dot(h, w_ref[...].astype(BF16), preferred_element_type=F32) + b_ref[...]


def _ada_call(cond, w_ada, b_ada):
    rows = cond.shape[0]
    tn = 1536
    return pl.pallas_call(
        _ada_kernel,
        grid=(DEPTH, 6 * D_MODEL // tn),
        in_specs=[
            pl.BlockSpec((rows, D_MODEL), lambda l, j: (0, 0)),
            pl.BlockSpec((None, D_MODEL, tn), lambda l, j: (l, 0, j)),
            pl.BlockSpec((None, 1, tn), lambda l, j: (l, 0, j)),
        ],
        out_specs=pl.BlockSpec((None, rows, tn), lambda l, j: (l, 0, j)),
        out_shape=jax.ShapeDtypeStruct((DEPTH, rows, 6 * D_MODEL), F32),
        compiler_params=_params("arbitrary", "arbitrary"),
        name="adaln",
    )(cond, w_ada, b_ada.reshape(DEPTH, 1, 6 * D_MODEL))


def _proj_kernel(*refs, has_small):
    if has_small:
        x_ref, sc_ref, sh_ref, w_ref, w2_ref, o_ref, o2_ref, h_ref = refs
    else:
        x_ref, sc_ref, sh_ref, w_ref, o_ref, h_ref = refs

    @pl.when(pl.program_id(1) == 0)
    def _():
        h = x_ref[...] * (1.0 + sc_ref[...]) + sh_ref[...]
        h_ref[...] = h.astype(BF16)
        if has_small:
            o2_ref[...] = jnp.dot(h_ref[...], w2_ref[...], preferred_element_type=F32)

    o_ref[...] = jnp.dot(h_ref[...], w_ref[...], preferred_element_type=F32).astype(o_ref.dtype)


def _proj_call(x, scale, shift, w, w_small, rows_per_mod, tm, tn, name):
    m = x.shape[0]
    n = w.shape[1]
    has_small = w_small is not None
    mod_spec = pl.BlockSpec((None, 1, D_MODEL), lambda i, j: ((i * tm) // rows_per_mod, 0, 0))
    in_specs = [pl.BlockSpec((tm, D_MODEL), lambda i, j: (i, 0)), mod_spec, mod_spec,
                pl.BlockSpec((D_MODEL, tn), lambda i, j: (0, j))]
    out_specs = [pl.BlockSpec((tm, tn), lambda i, j: (i, j))]
    out_shape = [jax.ShapeDtypeStruct((m, n), BF16)]
    args = [x, scale, shift, w]
    if has_small:
        in_specs.append(pl.BlockSpec((D_MODEL, 128), lambda i, j: (0, 0)))
        out_specs.append(pl.BlockSpec((tm, 128), lambda i, j: (i, 0)))
        out_shape.append(jax.ShapeDtypeStruct((m, 128), F32))
        args.append(w_small)
    return pl.pallas_call(
        functools.partial(_proj_kernel, has_small=has_small),
        grid=(m // tm, n // tn),
        in_specs=in_specs,
        out_specs=out_specs,
        out_shape=out_shape,
        scratch_shapes=[pltpu.VMEM((tm, D_MODEL), BF16)],
        compiler_params=_params("arbitrary", "arbitrary"),
        name=name,
    )(*args)


def _conv_silu_rows(src_ref, r0, n_rows, seqlen, w_ref, bias):
    cur = src_ref[pl.ds(r0, n_rows), :].astype(F32)
    has_prev = (r0 > 0).astype(F32)
    has_next = (r0 + n_rows < seqlen).astype(F32)
    g_prev = pl.multiple_of(jnp.maximum(r0 - BF16_ROWS, 0), BF16_ROWS)
    g_next = pl.multiple_of(jnp.minimum(r0 + n_rows, seqlen - BF16_ROWS), BF16_ROWS)
    prev_row = src_ref[pl.ds(g_prev, BF16_ROWS), :].astype(F32)[BF16_ROWS - 1:BF16_ROWS] * has_prev
    next_row = src_ref[pl.ds(g_next, BF16_ROWS), :].astype(F32)[0:1] * has_next
    rid = lax.broadcasted_iota(jnp.int32, cur.shape, 0)
    xm1 = jnp.where(rid == 0, prev_row, pltpu.roll(cur, 1, 0))
    xp1 = jnp.where(rid == n_rows - 1, next_row, pltpu.roll(cur, n_rows - 1, 0))
    out = w_ref[0:1, :] * xm1 + w_ref[1:2, :] * cur + w_ref[2:3, :] * xp1
    if bias is not None:
        out = out + bias
    return _silu(out)


def _transpose_rows(rows_list):
    pad = 128 - 8 * len(rows_list)
    stack = jnp.concatenate(rows_list + [jnp.zeros((pad, 128), F32)], axis=0)
    return stack.T


def _ssd_scan_kernel(*refs, seqlen, nseq, has_init, want_final):
    refs = list(refs)
    x_ref, b_ref, c_ref, cwx, cwb, cwc, cbx, cbb, cbc, dtt_ref, prm_ref, dsk_ref = refs[:12]
    pos = 12
    if has_init:
        s0_refs = refs[pos:pos + 2]
        pos += 2
    yout_ref = refs[pos]
    pos += 1
    if want_final:
        sfin_refs = refs[pos:pos + 2]
        pos += 2
    cs_b, ea_s, ns_s, st, y_ref = refs[pos:pos + 5]

    nc = seqlen // SSD_CHUNK
    q = SSD_CHUNK

    for g in range(nseq):
        for d in range(2):
            if has_init:
                halves = []
                for h in range(2):
                    blk = jnp.concatenate([s0_refs[d][g, 2 * h], s0_refs[d][g, 2 * h + 1]], axis=0)
                    halves.append(blk.T)
                st[g, d] = jnp.concatenate(halves, axis=1)
            else:
                st[g, d] = jnp.zeros((SSD_STATE, 256), F32)

    row = lax.broadcasted_iota(jnp.int32, (q, q), 0)
    col = lax.broadcasted_iota(jnp.int32, (q, q), 1)
    lane_head = lax.broadcasted_iota(jnp.int32, (q, 256), 1) // SSD_HEADDIM
    bias = prm_ref[:, 0:1]
    a_neg = -jnp.exp(prm_ref[:, 1:2])
    dskip = dsk_ref[...]

    cum_ms = ((row <= col).astype(F32), (row >= col).astype(F32))
    keeps = (row >= col, row <= col)
    lasts = (q - 1, 0)

    def intra(i, carry):
        blocks = []
        for g, j in ((g, j) for g in range(nseq) for j in range(unroll_a)):
            r0 = pl.multiple_of((i * unroll_a + j) * q, q)
            xc = _conv_silu_rows(x_ref.at[g], r0, q, seqlen, cwx, cbx[...])
            bc = _conv_silu_rows(b_ref.at[g], r0, q, seqlen, cwb, cbb[...])
            cc = _conv_silu_rows(c_ref.at[g], r0, q, seqlen, cwc, cbc[...])
            cc_b = cc.astype(BF16)
            cs_b[g, pl.ds(r0, q), :] = cc_b
            bc_t = bc.T
            dt_t = _softplus(dtt_ref[g, :, pl.ds(r0, q)] + bias)
            x_heads = jnp.concatenate([jnp.where(lane_head == r, xc, 0.0).astype(BF16) for r in range(4)], axis=0)
            blocks.append(dict(g=g, r0=r0, xc=xc, bc_t=bc_t, dt_t=dt_t, a_t=dt_t * a_neg, x_heads=x_heads,
                               scores=jnp.dot(cc_b, bc_t.astype(BF16), preferred_element_type=F32)))
        units = [(blk, d) for blk in blocks for d in range(2)]
        acums = [jnp.dot(blk["a_t"], cum_ms[d], precision=HI, preferred_element_type=F32) for blk, d in units]
        cols = _transpose_rows(acums)
        dtdes = [blk["dt_t"] * jnp.exp(ac[:, lasts[d]:lasts[d] + 1] - ac) for (blk, d), ac in zip(units, acums)]
        for (blk, d), ac in zip(units, acums):
            ea_t = jnp.exp(ac)
            ea_rows = jnp.concatenate([jnp.broadcast_to(ea_t[4 * d + r:4 * d + r + 1, :], (SSD_HEADDIM, q))
                                       for r in range(4)], axis=0)
            ea_s[blk["g"], d, pl.ds(blk["r0"], q), :] = jnp.concatenate([ea_rows[0:128].T, ea_rows[128:256].T], axis=1)
        lhss = []
        for u, ((blk, d), ac, dtde) in enumerate(zip(units, acums, dtdes)):
            parts = []
            for r in range(4):
                k = 4 * d + r
                decay = jnp.where(keeps[d], jnp.exp(cols[:, 8 * u + k:8 * u + k + 1] - ac[k:k + 1, :]), 0.0)
                m = decay * blk["scores"] * blk["dt_t"][k:k + 1, :]
                bt = blk["bc_t"] * dtde[k:k + 1, :]
                parts.append(jnp.concatenate([m, bt], axis=0).astype(BF16))
            lhss.append(jnp.concatenate(parts, axis=1))
        ress = [jnp.dot(lhs, blk["x_heads"], preferred_element_type=F32) for lhs, (blk, d) in zip(lhss, units)]
        for res, (blk, d) in zip(ress, units):
            ns_s[blk["g"], d, pl.ds(blk["r0"], q), :] = res[q:]
        for j, blk in enumerate(blocks):
            y_ref[blk["g"], pl.ds(blk["r0"], q), :] = blk["xc"] * dskip + ress[2 * j][:q] + ress[2 * j + 1][:q]
        return carry

    unroll_a = max(1, SSD_BLOCKS // nseq)
    while nc % unroll_a:
        unroll_a //= 2
    lax.fori_loop(0, nc // unroll_a, intra, 0)

    def inter(i, carry):
        jobs = []
        for g, d in ((g, d) for g in range(nseq) for d in range(2)):
            s_t = st[g, d]
            for j in range(unroll_a):
                c = i * unroll_a + j
                r0 = pl.multiple_of((c if d == 0 else nc - 1 - c) * q, q)
                jobs.append((g, d, r0, s_t.astype(BF16)))
                s_t = s_t * ea_s[g, d, pl.ds(r0 + lasts[d], 1), :] + ns_s[g, d, pl.ds(r0, q), :]
            st[g, d] = s_t
        y_inters = [jnp.dot(cs_b[g, pl.ds(r0, q), :], s_b, preferred_element_type=F32) * ea_s[g, d, pl.ds(r0, q), :]
                    for g, d, r0, s_b in jobs]
        for (g, d, r0, _), y_inter in zip(jobs, y_inters):
            y_ref[g, pl.ds(r0, q), :] += y_inter
        return carry

    lax.fori_loop(0, nc // unroll_a, inter, 0)

    def emit(c, carry):
        r0 = pl.multiple_of(c * q, q)
        for g in range(nseq):
            yout_ref[g, pl.ds(r0, q), :] = y_ref[g, pl.ds(r0, q), :].astype(yout_ref.dtype)
        return carry

    lax.fori_loop(0, nc, emit, 0)

    if want_final:
        for g, d in ((g, d) for g in range(nseq) for d in range(2)):
            s_t = st[g, d]
            for h in range(2):
                blk = s_t[:, h * 128:(h + 1) * 128].T
                sfin_refs[d][g, 2 * h] = blk[0:64]
                sfin_refs[d][g, 2 * h + 1] = blk[64:128]


def _ssd_scan_call(proj3, dtt, conv_w, conv_b, prm, dskip, s0, want_final, name):
    bsz, seqlen, _ = proj3.shape
    has_init = s0 is not None
    nseq = max(1, min(bsz, SSD_BLOCKS * SSD_CHUNK // seqlen))
    while bsz % nseq:
        nseq -= 1
    xoff = SSD_D_INNER // 256
    boff = (2 * SSD_D_INNER) // 128
    coff = boff + SSD_GROUPS
    in_specs = [
        pl.BlockSpec((nseq, seqlen, 256), lambda b, g: (b, 0, xoff + g)),
        pl.BlockSpec((nseq, seqlen, 128), lambda b, g: (b, 0, boff + g)),
        pl.BlockSpec((nseq, seqlen, 128), lambda b, g: (b, 0, coff + g)),
        pl.BlockSpec((3, 256), lambda b, g: (0, g)),
        pl.BlockSpec((3, 128), lambda b, g: (0, 16 + g)),
        pl.BlockSpec((3, 128), lambda b, g: (0, 24 + g)),
        pl.BlockSpec((1, 256), lambda b, g: (0, g)),
        pl.BlockSpec((1, 128), lambda b, g: (0, 16 + g)),
        pl.BlockSpec((1, 128), lambda b, g: (0, 24 + g)),
        pl.BlockSpec((nseq, None, 8, seqlen), lambda b, g: (b, g, 0, 0)),
        pl.BlockSpec((None, 8, 2), lambda b, g: (g, 0, 0)),
        pl.BlockSpec((None, 1, 256), lambda b, g: (g, 0, 0)),
    ]
    args = [proj3, proj3, proj3, conv_w, conv_w, conv_w, conv_b, conv_b, conv_b, dtt, prm, dskip]
    state_spec = pl.BlockSpec((nseq, 4, SSD_HEADDIM, SSD_STATE), lambda b, g: (b, g, 0, 0))
    if has_init:
        in_specs += [state_spec, state_spec]
        args += [s0[0], s0[1]]
    out_specs = [pl.BlockSpec((nseq, seqlen, 256), lambda b, g: (b, 0, g))]
    out_shape = [jax.ShapeDtypeStruct((bsz, seqlen, SSD_D_INNER), BF16)]
    if want_final:
        out_specs += [state_spec, state_spec]
        out_shape += [jax.ShapeDtypeStruct((bsz, SSD_HEADS, SSD_HEADDIM, SSD_STATE), F32)] * 2
    return pl.pallas_call(
        functools.partial(_ssd_scan_kernel, seqlen=seqlen, nseq=nseq, has_init=has_init, want_final=want_final),
        grid=(bsz // nseq, SSD_GROUPS),
        in_specs=in_specs,
        out_specs=out_specs,
        out_shape=out_shape,
        scratch_shapes=[pltpu.VMEM((nseq, seqlen, SSD_STATE), BF16),
                        pltpu.VMEM((nseq, 2, seqlen, 256), F32),
                        pltpu.VMEM((nseq, 2, seqlen, 256), F32),
                        pltpu.VMEM((nseq, 2, SSD_STATE, 256), F32),
                        pltpu.VMEM((nseq, seqlen, 256), F32)],
        compiler_params=_params("arbitrary", "arbitrary"),
        name=name,
    )(*args)


def _unit_tri_solve(a_list, rhs_list, merge_masks):
    xs = a_list
    for m in merge_masks:
        xs_b = [x.astype(BF16) for x in xs]
        xs = [x - jnp.dot(x_b, x_b * m, preferred_element_type=F32) for x, x_b in zip(xs, xs_b)]
    return [rhs - jnp.dot(x.astype(BF16), rhs.astype(BF16), preferred_element_type=F32)
            for x, rhs in zip(xs, rhs_list)]


def _gdn_scan_kernel(*refs, seqlen, nseq, has_init, want_final):
    refs = list(refs)
    q_ref, k_ref, v_ref, cwq, cwk, cwv, smt_ref, prm_ref = refs[:8]
    pos = 8
    if has_init:
        s0_refs = refs[pos:pos + 2]
        pos += 2
    oout_ref = refs[pos]
    pos += 1
    if want_final:
        sfin_refs = refs[pos:pos + 2]
        pos += 2
    u_s, wq_s, qkm_s, kdt_s, gl_s, st, o_ref = refs[pos:pos + 7]

    blk = GDN_BLOCK
    nb = seqlen // blk
    qk_scale = GDN_HEAD ** -0.5

    for g in range(nseq):
        for d in range(2):
            for r in range(2):
                if has_init:
                    st[g, d, r] = s0_refs[d][g, r]
                else:
                    st[g, d, r] = jnp.zeros((GDN_HEAD, GDN_HEAD), F32)

    row = lax.broadcasted_iota(jnp.int32, (blk, blk), 0)
    col = lax.broadcasted_iota(jnp.int32, (blk, blk), 1)
    merge_masks = [jnp.where(((row >> (l + 1)) == (col >> (l + 1))) & ((row >> l) != (col >> l)), 1.0, 0.0
                             ).astype(BF16) for l in range(7)]
    bias = prm_ref[:, 0:1]
    a_neg = -jnp.exp(prm_ref[:, 1:2])
    nt = (((1,), (1,)), ((), ()))

    cum_ms = ((row <= col).astype(F32), (row >= col).astype(F32))
    stricts = (row > col, row < col)
    incls = (row >= col, row <= col)
    lasts = (blk - 1, 0)

    def load_block(g, c):
        r0 = pl.multiple_of(c * blk, blk)
        qv = _conv_silu_rows(q_ref.at[g], r0, blk, seqlen, cwq, None)
        kv = _conv_silu_rows(k_ref.at[g], r0, blk, seqlen, cwk, None)
        qc = qv * (lax.rsqrt(jnp.sum(qv * qv, axis=-1, keepdims=True) + RMS_EPS) * qk_scale)
        kc = kv * lax.rsqrt(jnp.sum(kv * kv, axis=-1, keepdims=True) + RMS_EPS)
        vc = _conv_silu_rows(v_ref.at[g], r0, blk, seqlen, cwv, None)
        o_ref[g, pl.ds(r0, blk), :] = jnp.zeros((blk, 256), F32)
        sm = smt_ref[g, :, pl.ds(r0, blk)]
        return dict(g=g, r0=r0, wq0=pl.multiple_of(c * 2 * blk, 2 * blk), g0=pl.multiple_of(c * 8, 8),
                    qc=qc, kc=kc, vc=vc, g_t=a_neg * _softplus(sm + bias), beta_t=_sigmoid(sm))

    def prepare(blocks):
        for b in blocks:
            kc_b = b["kc"].astype(BF16)
            b["kk"] = lax.dot_general(kc_b, kc_b, nt, preferred_element_type=F32)
            b["qk"] = lax.dot_general(b["qc"].astype(BF16), kc_b, nt, preferred_element_type=F32)
        units = [(b, d) for b in blocks for d in range(2)]
        gcs = [jnp.dot(b["g_t"], cum_ms[d], precision=HI, preferred_element_type=F32) for b, d in units]
        tots = [gc[:, lasts[d]:lasts[d] + 1] for (b, d), gc in zip(units, gcs)]
        colss = [_transpose_rows([gc, jnp.exp(gc), b["beta_t"]]) for (b, d), gc in zip(units, gcs)]
        for (b, d), tot in zip(units, tots):
            gl_s[b["g"], d, pl.ds(b["g0"], 8), :] = jnp.broadcast_to(jnp.exp(tot), (8, blk))
        for b in blocks:
            b["kc_t"] = b["kc"].T
        chains = []
        for (b, d), gc, tot, cols in zip(units, gcs, tots, colss):
            kd_t = jnp.exp(tot - gc)
            for r in range(2):
                k = 2 * d + r
                e = jnp.exp(cols[:, k:k + 1] - gc[k:k + 1, :])
                beta = jnp.broadcast_to(cols[:, 20 + k:21 + k], (blk, GDN_HEAD))
                eg = jnp.broadcast_to(cols[:, 8 + k:9 + k], (blk, GDN_HEAD))
                vr = b["vc"][:, r * 128:(r + 1) * 128]
                g = b["g"]
                wq_s[g, d, r, pl.ds(b["wq0"] + blk, blk), :] = (b["qc"] * eg).astype(BF16)
                qkm_s[g, d, r, pl.ds(b["r0"], blk), :] = (b["qk"] * jnp.where(incls[d], e, 0.0)).astype(BF16)
                kdt_s[g, d, r, pl.ds(b["r0"], blk), :] = (b["kc_t"] * kd_t[k:k + 1, :]).astype(BF16)
                chains.append(dict(
                    g=g, d=d, r=r, r0=b["r0"], wq0=b["wq0"],
                    a=beta * b["kk"] * jnp.where(stricts[d], e, 0.0),
                    rhs=jnp.concatenate([vr * beta, b["kc"] * (beta * eg)], axis=1)))
        return chains

    chains_per_block = 4
    per_iter = max(1, GDN_CHAINS // (chains_per_block * nseq))
    while nb % per_iter:
        per_iter //= 2

    def phase1(i, carry):
        chains = prepare([load_block(g, i * per_iter + j) for g in range(nseq) for j in range(per_iter)])
        sols = _unit_tri_solve([ch["a"] for ch in chains], [ch["rhs"] for ch in chains], merge_masks)
        for sol, ch in zip(sols, chains):
            u_s[ch["g"], ch["d"], ch["r"], pl.ds(ch["r0"], blk), :] = sol[:, :GDN_HEAD]
            wq_s[ch["g"], ch["d"], ch["r"], pl.ds(ch["wq0"], blk), :] = sol[:, GDN_HEAD:].astype(BF16)
        return carry

    lax.fori_loop(0, nb // per_iter, phase1, 0)

    def phase2(i, carry):
        jobs = []
        for g in range(nseq):
            for d, c in ((0, i), (1, nb - 1 - i)):
                r0 = pl.multiple_of(c * blk, blk)
                wq0 = pl.multiple_of(c * 2 * blk, 2 * blk)
                gl = gl_s[g, d, pl.ds(pl.multiple_of(c * 8, 8), 8), :]
                for r in range(2):
                    jobs.append(((g, d, r), r0, wq0, gl[2 * d + r:2 * d + r + 1, :]))
        ss = [st[idx] for idx, _, _, _ in jobs]
        wss = [jnp.dot(wq_s[idx + (pl.ds(wq0, 2 * blk), slice(None))], s.astype(BF16), preferred_element_type=F32)
               for (idx, _, wq0, _), s in zip(jobs, ss)]
        vns = [(u_s[idx + (pl.ds(r0, blk), slice(None))] - ws[:blk]).astype(BF16)
               for (idx, r0, _, _), ws in zip(jobs, wss)]
        for (idx, r0, _, gl), s, vn in zip(jobs, ss, vns):
            st[idx] = gl * s + jnp.dot(kdt_s[idx + (pl.ds(r0, blk), slice(None))], vn, preferred_element_type=F32)
        outs = [ws[blk:] + jnp.dot(qkm_s[idx + (pl.ds(r0, blk), slice(None))], vn, preferred_element_type=F32)
                for (idx, r0, _, _), ws, vn in zip(jobs, wss, vns)]
        for pair in range(2 * nseq):
            (g, _, _), r0 = jobs[2 * pair][0], jobs[2 * pair][1]
            o_ref[g, pl.ds(r0, blk), :] += jnp.concatenate(outs[2 * pair:2 * pair + 2], axis=1)
        return carry

    lax.fori_loop(0, nb, phase2, 0)

    def emit(c, carry):
        r0 = pl.multiple_of(c * blk, blk)
        for g in range(nseq):
            oout_ref[g, pl.ds(r0, blk), :] = o_ref[g, pl.ds(r0, blk), :].astype(oout_ref.dtype)
        return carry

    lax.fori_loop(0, nb, emit, 0)

    if want_final:
        for g in range(nseq):
            for d in range(2):
                for r in range(2):
                    sfin_refs[d][g, r] = st[g, d, r]


def _gdn_scan_call(proj3, smt, conv_w, prm, s0, want_final, name):
    bsz, seqlen, _ = proj3.shape
    has_init = s0 is not None
    nseq = max(2, min(bsz, GDN_CHAINS * GDN_BLOCK // (4 * seqlen)))
    while bsz % nseq:
        nseq -= 1
    in_specs = [
        pl.BlockSpec((nseq, seqlen, 128), lambda b, g: (b, 0, g)),
        pl.BlockSpec((nseq, seqlen, 128), lambda b, g: (b, 0, GDN_K_HEADS + g)),
        pl.BlockSpec((nseq, seqlen, 256), lambda b, g: (b, 0, GDN_K_HEADS + g)),
        pl.BlockSpec((3, 128), lambda b, g: (0, g)),
        pl.BlockSpec((3, 128), lambda b, g: (0, GDN_K_HEADS + g)),
        pl.BlockSpec((3, 256), lambda b, g: (0, GDN_K_HEADS + g)),
        pl.BlockSpec((nseq, None, 8, seqlen), lambda b, g: (b, g, 0, 0)),
        pl.BlockSpec((None, 8, 2), lambda b, g: (g, 0, 0)),
    ]
    args = [proj3, proj3, proj3, conv_w, conv_w, conv_w, smt, prm]
    state_spec = pl.BlockSpec((nseq, 2, GDN_HEAD, GDN_HEAD), lambda b, g: (b, g, 0, 0))
    if has_init:
        in_specs += [state_spec, state_spec]
        args += [s0[0], s0[1]]
    out_specs = [pl.BlockSpec((nseq, seqlen, 256), lambda b, g: (b, 0, g))]
    out_shape = [jax.ShapeDtypeStruct((bsz, seqlen, GDN_V), BF16)]
    if want_final:
        out_specs += [state_spec, state_spec]
        out_shape += [jax.ShapeDtypeStruct((bsz, GDN_V_HEADS, GDN_HEAD, GDN_HEAD), F32)] * 2
    return pl.pallas_call(
        functools.partial(_gdn_scan_kernel, seqlen=seqlen, nseq=nseq, has_init=has_init, want_final=want_final),
        grid=(bsz // nseq, GDN_K_HEADS),
        in_specs=in_specs,
        out_specs=out_specs,
        out_shape=out_shape,
        scratch_shapes=[pltpu.VMEM((nseq, 2, 2, seqlen, GDN_HEAD), F32),
                        pltpu.VMEM((nseq, 2, 2, 2 * seqlen, GDN_HEAD), BF16),
                        pltpu.VMEM((nseq, 2, 2, seqlen, GDN_BLOCK), BF16),
                        pltpu.VMEM((nseq, 2, 2, seqlen, GDN_BLOCK), BF16),
                        pltpu.VMEM((nseq, 2, 8 * (seqlen // GDN_BLOCK), GDN_BLOCK), F32),
                        pltpu.VMEM((nseq, 2, 2, GDN_HEAD, GDN_HEAD), F32),
                        pltpu.VMEM((nseq, seqlen, 256), F32)],
        compiler_params=_params("arbitrary", "arbitrary"),
        name=name,
    )(*args)


def _residual_layer_norm(x, gate, o, ln_g, ln_b):
    r = DEEPNORM_ALPHA * x + gate * o
    mu = jnp.mean(r, axis=-1, keepdims=True)
    rc = r - mu
    var = jnp.mean(rc * rc, axis=-1, keepdims=True)
    return rc * lax.rsqrt(var + LN_EPS) * ln_g + ln_b


def _out_kernel(y_ref, z_ref, nw_ref, w_ref, x_ref, gt_ref, lg_ref, lb_ref, o_ref, *, mode):
    if mode == "ssd":
        u = y_ref[...].astype(F32) * _silu(z_ref[...].astype(F32))
        ms = jnp.mean(u * u, axis=-1, keepdims=True)
        a = (u * lax.rsqrt(ms + RMS_EPS) * nw_ref[...]).astype(BF16)
    else:
        parts = []
        for h in range(GDN_V_HEADS):
            sl = slice(h * GDN_HEAD, (h + 1) * GDN_HEAD)
            oh = y_ref[:, sl].astype(F32)
            ms = jnp.mean(oh * oh, axis=-1, keepdims=True)
            parts.append((oh * lax.rsqrt(ms + RMS_EPS) * nw_ref[...] * _silu(z_ref[:, sl].astype(F32))).astype(BF16))
        a = jnp.concatenate(parts, axis=1)
    o = jnp.dot(a, w_ref[...], preferred_element_type=F32)
    o_ref[...] = _residual_layer_norm(x_ref[...], gt_ref[...], o, lg_ref[...], lb_ref[...])


def _out_call(mode, act, z_src, z_blk, norm_w, w, x, gate, ln_g, ln_b, rows_per_mod, tm, name):
    m, kdim = act.shape
    row_spec = pl.BlockSpec((tm, D_MODEL), lambda i: (i, 0))
    vec_spec = pl.BlockSpec((1, D_MODEL), lambda i: (0, 0))
    in_specs = [pl.BlockSpec((tm, kdim), lambda i: (i, 0)),
                pl.BlockSpec((tm, kdim), lambda i: (i, z_blk)),
                pl.BlockSpec((1, norm_w.shape[1]), lambda i: (0, 0)),
                pl.BlockSpec((kdim, D_MODEL), lambda i: (0, 0)), row_spec,
                pl.BlockSpec((None, 1, D_MODEL), lambda i: ((i * tm) // rows_per_mod, 0, 0)),
                vec_spec, vec_spec]
    args = [act, z_src, norm_w, w, x, gate, ln_g, ln_b]
    return pl.pallas_call(
        functools.partial(_out_kernel, mode=mode),
        grid=(m // tm,),
        in_specs=in_specs,
        out_specs=row_spec,
        out_shape=jax.ShapeDtypeStruct((m, D_MODEL), F32),
        compiler_params=_params("arbitrary"),
        name=name,
    )(*args)


def _ffn_down_kernel(*refs, tm, period, row_taps, tiles_per_seq):
    if row_taps:
        (a_ref, ap_ref, an_ref, v_ref, cw_ref, cb_ref, w_ref, x_ref, gt_ref, lg_ref, lb_ref,
         o_ref, acc_ref, act_ref, ext_ref) = refs
        t = pl.program_id(0) % tiles_per_seq
        has_prev = (t > 0).astype(F32)
        has_next = (t < tiles_per_seq - 1).astype(F32)
        base, taps, n = HALO, (0, 1, 2), tm + 2 * HALO
    else:
        a_ref, v_ref, cw_ref, cb_ref, w_ref, x_ref, gt_ref, lg_ref, lb_ref, o_ref, acc_ref, act_ref = refs
        base, taps, n = 0, (1,), tm
    colpos = lax.broadcasted_iota(jnp.int32, (n, FFN_TC), 0) % period
    is_first = colpos == 0
    is_last = colpos == period - 1
    n_ct = D_FF // FFN_TC

    def gated_tile(c):
        cols = pl.ds(pl.multiple_of(c * FFN_TC, FFN_TC), FFN_TC)
        if row_taps:
            ext_ref[0:HALO, :] = ap_ref[:, cols].astype(F32) * has_prev
            ext_ref[HALO:HALO + tm, :] = a_ref[:, cols].astype(F32)
            ext_ref[HALO + tm:, :] = an_ref[:, cols].astype(F32) * has_next
            ext = ext_ref[...]
        else:
            ext = a_ref[:, cols].astype(F32)
        em1 = jnp.where(is_first, 0.0, pltpu.roll(ext, 1, 0))
        ep1 = jnp.where(is_last, 0.0, pltpu.roll(ext, n - 1, 0))
        conv = cb_ref[:, cols]
        for dr in taps:
            off = base + (dr - 1) * GRID_W
            conv = (conv + cw_ref[3 * dr:3 * dr + 1, cols] * em1[off:off + tm]
                    + cw_ref[3 * dr + 1:3 * dr + 2, cols] * ext[off:off + tm]
                    + cw_ref[3 * dr + 2:3 * dr + 3, cols] * ep1[off:off + tm])
        act_ref[c % 2] = (_silu(conv) * v_ref[:, cols].astype(F32)).astype(BF16)

    def project(c):
        rows = pl.ds(pl.multiple_of(c * FFN_TC, FFN_TC), FFN_TC)
        return jnp.dot(act_ref[c % 2], w_ref[rows, :], preferred_element_type=F32)

    gated_tile(0)
    acc_ref[...] = jnp.zeros((tm, D_MODEL), F32)

    def channel_tile(c, carry):
        acc_ref[...] += project(c - 1)
        gated_tile(c)
        return carry

    lax.fori_loop(1, n_ct, channel_tile, 0)
    o = acc_ref[...] + project(n_ct - 1)
    o_ref[...] = _residual_layer_norm(x_ref[...], gt_ref[...], o, lg_ref[...], lb_ref[...])


def _ffn_down_call(up, conv_w9, conv_b, w_down, x, gate, ln_g, ln_b, seqlen, is_grid, rows_per_mod, tm, name):
    m = up.shape[0]
    a_spec = pl.BlockSpec((tm, D_FF), lambda i: (i, 0))
    v_spec = pl.BlockSpec((tm, D_FF), lambda i: (i, 1))
    scratch = [pltpu.VMEM((tm, D_MODEL), F32), pltpu.VMEM((2, tm, FFN_TC), BF16)]
    if is_grid:
        scratch.append(pltpu.VMEM((tm + 2 * HALO, FFN_TC), F32))
        hb = tm // HALO
        nhb = m // HALO
        tiles_per_seq = seqlen // tm
        in_specs = [a_spec,
                    pl.BlockSpec((HALO, D_FF), lambda i: (jnp.maximum(i * hb - 1, 0), 0)),
                    pl.BlockSpec((HALO, D_FF), lambda i: (jnp.minimum((i + 1) * hb, nhb - 1), 0)),
                    v_spec]
        args = [up, up, up, up]
        period = GRID_W
    else:
        tiles_per_seq = 1
        in_specs = [a_spec, v_spec]
        args = [up, up]
        period = seqlen
    row_spec = pl.BlockSpec((tm, D_MODEL), lambda i: (i, 0))
    vec_spec = pl.BlockSpec((1, D_MODEL), lambda i: (0, 0))
    in_specs += [pl.BlockSpec((9, D_FF), lambda i: (0, 0)), pl.BlockSpec((1, D_FF), lambda i: (0, 0)),
                 pl.BlockSpec((D_FF, D_MODEL), lambda i: (0, 0)), row_spec,
                 pl.BlockSpec((None, 1, D_MODEL), lambda i: ((i * tm) // rows_per_mod, 0, 0)),
                 vec_spec, vec_spec]
    args += [conv_w9, conv_b, w_down, x, gate, ln_g, ln_b]
    return pl.pallas_call(
        functools.partial(_ffn_down_kernel, tm=tm, period=period, row_taps=is_grid, tiles_per_seq=tiles_per_seq),
        grid=(m // tm,),
        in_specs=in_specs,
        out_specs=row_spec,
        out_shape=jax.ShapeDtypeStruct((m, D_MODEL), F32),
        scratch_shapes=scratch,
        compiler_params=_params("arbitrary"),
        name=name,
    )(*args)


def _group_rows(t, bsz, seqlen, lead):
    n_lead = int(math.prod(lead))
    t = t.reshape(bsz, seqlen, n_lead, 8, -1)
    r = t.shape[-1]
    t = jnp.transpose(t, (0, 3, 2, 4, 1))
    return t.reshape(bsz, 8, n_lead * r, seqlen)


def _group_params(p):
    r = p.shape[1] // 8
    return jnp.transpose(p.reshape(2, 8, r), (1, 0, 2)).reshape(8, 2 * r)


def _run_tokens(x, seqlen, mods, states, want_final, w, tag):
    bsz = x.shape[0]
    m = bsz * seqlen
    nb = mods.shape[1]
    rows_per_mod = m // nb
    xt = x.reshape(m, D_MODEL)
    finals = []
    is_grid = tag == "s"
    for i in range(DEPTH):
        def mod(k):
            return mods[i, :, k].reshape(nb, 1, D_MODEL)
        lw = w[i]
        proj, small = _proj_call(xt, mod(1), mod(0), lw["w_in"], lw["w_in_small"], rows_per_mod,
                                 tm=1024, tn=1536, name=f"inproj{i}{tag}")
        proj3 = proj.reshape(bsz, seqlen, 6144)
        if i % 2 == 0:
            dtt = _group_rows(small[:, :2 * SSD_HEADS], bsz, seqlen, (2,))
            res = _ssd_scan_call(proj3, dtt, lw["conv_w"], lw["conv_b"], lw["prm"], lw["dskip"],
                                 states[i], want_final, name=f"ssdscan{tag}")
            mode, z_blk = "ssd", 0
        else:
            smt = _group_rows(small[:, :4 * GDN_V_HEADS], bsz, seqlen, (2, 2))
            res = _gdn_scan_call(proj3, smt, lw["conv_w"], lw["prm"], states[i], want_final,
                                 name=f"gdnscan{tag}")
            mode, z_blk = "gdn", 2
        y = res[0].reshape(m, 2048)
        if want_final:
            finals.append((res[1], res[2]))
        xt = _out_call(mode, y, proj, z_blk, lw["norm_w"], lw["w_out"], xt, mod(2), lw["ln_g0"], lw["ln_b0"],
                       rows_per_mod, tm=256, name=f"outproj{i}{tag}")
        up = _proj_call(xt, mod(4), mod(3), lw["w_up"], None, rows_per_mod,
                        tm=1024, tn=1408, name=f"ffnup{i}{tag}")[0]
        xt = _ffn_down_call(up, lw["ffn_conv_w"], lw["ffn_conv_b"], lw["w_down"], xt, mod(5),
                            lw["ln_g1"], lw["ln_b1"], seqlen, is_grid, rows_per_mod, tm=512,
                            name=f"ffndown{i}{tag}")
    return xt.reshape(bsz, seqlen, D_MODEL), finals


def kernel(x_prompt, x_sample, state_ssd_fwd, state_ssd_bwd, state_gdn_fwd, state_gdn_bwd, c, c_ctx, w_ada, b_ada, ln_g, ln_b, ssd_w_in, ssd_conv_w, ssd_conv_b, ssd_dt_bias, ssd_a_log, ssd_d, ssd_norm_w, ssd_w_out, gdn_w_in, gdn_conv_w, gdn_dt_bias, gdn_a_log, gdn_norm_w, gdn_w_out, ffn_w_up, ffn_conv_w, ffn_conv_b, ffn_w_down):
    n_dec = c.shape[0]
    cond = jnp.concatenate([c_ctx[None, :], c, jnp.zeros((16 - 1 - n_dec, D_MODEL), F32)], axis=0)
    ada = _ada_call(cond, w_ada, b_ada).reshape(DEPTH, 16, 6, D_MODEL)
    mods_p = ada[:, 0:1]
    mods_s = ada[:, 1:1 + n_dec]

    def small_cols(wm):
        return jnp.pad(wm, ((0, 0), (0, 128 - wm.shape[1]))).astype(BF16)

    weights = []
    for i in range(DEPTH):
        j = i // 2
        lw = {
            "ln_g0": ln_g[i, 0:1], "ln_b0": ln_b[i, 0:1], "ln_g1": ln_g[i, 1:2], "ln_b1": ln_b[i, 1:2],
            "w_up": ffn_w_up[i].astype(BF16), "w_down": ffn_w_down[i].astype(BF16),
            "ffn_conv_w": ffn_conv_w[i].reshape(9, D_FF), "ffn_conv_b": ffn_conv_b[i].reshape(1, D_FF),
        }
        if i % 2 == 0:
            lw["w_in"] = ssd_w_in[j][:, :6144].astype(BF16)
            lw["w_in_small"] = small_cols(ssd_w_in[j][:, 6144:])
            lw["conv_w"] = ssd_conv_w[j]
            lw["conv_b"] = ssd_conv_b[j].reshape(1, SSD_XBC)
            lw["prm"] = jnp.stack([_group_params(ssd_dt_bias[j]), _group_params(ssd_a_log[j])], axis=-1)
            lw["dskip"] = jnp.repeat(ssd_d[j], SSD_HEADDIM).reshape(SSD_GROUPS, 1, 256)
            lw["norm_w"] = ssd_norm_w[j].reshape(1, SSD_D_INNER)
            lw["w_out"] = ssd_w_out[j].astype(BF16)
        else:
            lw["w_in"] = gdn_w_in[j][:, :6144].astype(BF16)
            lw["w_in_small"] = small_cols(gdn_w_in[j][:, 6144:])
            lw["conv_w"] = gdn_conv_w[j]
            prm = jnp.stack([_group_params(gdn_dt_bias[j]), _group_params(gdn_a_log[j])], axis=-1)
            lw["prm"] = jnp.pad(prm, ((0, 0), (0, 4), (0, 0)))
            lw["norm_w"] = gdn_norm_w[j].reshape(1, GDN_HEAD)
            lw["w_out"] = gdn_w_out[j].astype(BF16)
        weights.append(lw)

    states_s = [(state_ssd_fwd[:, 0], state_ssd_bwd[:, 0]), (state_gdn_fwd[:, 0], state_gdn_bwd[:, 0])]
    y_prompt, fin = _run_tokens(x_prompt, x_prompt.shape[1], mods_p, [None, None], True, weights, "p")
    y_sample, _ = _run_tokens(x_sample, x_sample.shape[1], mods_s, states_s, False, weights, "s")
    return (y_prompt, y_sample,
            fin[0][0][:, None], fin[0][1][:, None], fin[1][0][:, None], fin[1][1][:, None])
```

```python
import functools
import math

import jax
import jax.numpy as jnp
from jax import lax
from jax.experimental import pallas as pl
from jax.experimental.pallas import tpu as pltpu

F32 = jnp.float32
BF16 = jnp.bfloat16
HI = lax.Precision.HIGHEST

D_MODEL = 1024
DEPTH = 2
GRID_W = 64

SSD_D_INNER = 2048
SSD_HEADDIM = 64
SSD_HEADS = 32
SSD_GROUPS = 8
SSD_STATE = 128
SSD_CHUNK = 128
SSD_BLOCKS = 4
SSD_XBC = 4096

GDN_K_HEADS = 8
GDN_V_HEADS = 16
GDN_HEAD = 128
GDN_QK = 1024
GDN_V = 2048
GDN_QKV = 4096
GDN_BLOCK = 128
GDN_CHAINS = 16

D_FF = 2816
FFN_TC = 256
HALO = 128

DEEPNORM_ALPHA = (2.0 * DEPTH) ** 0.25
LN_EPS = 1e-5
RMS_EPS = 1e-6

VMEM_LIMIT = 56 * 1024 * 1024
BF16_ROWS = 16


def _silu(x):
    h = 0.5 * x
    return h + h * jnp.tanh(h)


def _sigmoid(x):
    return 0.5 + 0.5 * jnp.tanh(0.5 * x)


def _softplus(x):
    return jnp.maximum(x, 0.0) + jnp.log(1.0 + jnp.exp(-jnp.abs(x)))


def _params(*sem):
    return pltpu.CompilerParams(dimension_semantics=sem, vmem_limit_bytes=VMEM_LIMIT)


def _ada_kernel(c_ref, w_ref, b_ref, o_ref):
    h = _silu(c_ref[...]).astype(BF16)
    o_ref[...] = jnp.dot(h, w_ref[...].astype(BF16), preferred_element_type=F32) + b_ref[...]


def _ada_call(cond, w_ada, b_ada):
    rows = cond.shape[0]
    tn = 1536
    return pl.pallas_call(
        _ada_kernel,
        grid=(DEPTH, 6 * D_MODEL // tn),
        in_specs=[
            pl.BlockSpec((rows, D_MODEL), lambda l, j: (0, 0)),
            pl.BlockSpec((None, D_MODEL, tn), lambda l, j: (l, 0, j)),
            pl.BlockSpec((None, 1, tn), lambda l, j: (l, 0, j)),
        ],
        out_specs=pl.BlockSpec((None, rows, tn), lambda l, j: (l, 0, j)),
        out_shape=jax.ShapeDtypeStruct((DEPTH, rows, 6 * D_MODEL), F32),
        compiler_params=_params("arbitrary", "arbitrary"),
        name="adaln",
    )(cond, w_ada, b_ada.reshape(DEPTH, 1, 6 * D_MODEL))


def _proj_kernel(*refs, has_small):
    if has_small:
        x_ref, sc_ref, sh_ref, w_ref, w2_ref, o_ref, o2_ref, h_ref = refs
    else:
        x_ref, sc_ref, sh_ref, w_ref, o_ref, h_ref = refs

    @pl.when(pl.program_id(1) == 0)
    def _():
        h = x_ref[...] * (1.0 + sc_ref[...]) + sh_ref[...]
        h_ref[...] = h.astype(BF16)
        if has_small:
            o2_ref[...] = jnp.dot(h_ref[...], w2_ref[...], preferred_element_type=F32)

    o_ref[...] = jnp.dot(h_ref[...], w_ref[...], preferred_element_type=F32).astype(o_ref.dtype)


def _proj_call(x, scale, shift, w, w_small, rows_per_mod, tm, tn, name):
    m = x.shape[0]
    n = w.shape[1]
    has_small = w_small is not None
    mod_spec = pl.BlockSpec((None, 1, D_MODEL), lambda i, j: ((i * tm) // rows_per_mod, 0, 0))
    in_specs = [pl.BlockSpec((tm, D_MODEL), lambda i, j: (i, 0)), mod_spec, mod_spec,
                pl.BlockSpec((D_MODEL, tn), lambda i, j: (0, j))]
    out_specs = [pl.BlockSpec((tm, tn), lambda i, j: (i, j))]
    out_shape = [jax.ShapeDtypeStruct((m, n), BF16)]
    args = [x, scale, shift, w]
    if has_small:
        in_specs.append(pl.BlockSpec((D_MODEL, 128), lambda i, j: (0, 0)))
        out_specs.append(pl.BlockSpec((tm, 128), lambda i, j: (i, 0)))
        out_shape.append(jax.ShapeDtypeStruct((m, 128), F32))
        args.append(w_small)
    return pl.pallas_call(
        functools.partial(_proj_kernel, has_small=has_small),
        grid=(m // tm, n // tn),
        in_specs=in_specs,
        out_specs=out_specs,
        out_shape=out_shape,
        scratch_shapes=[pltpu.VMEM((tm, D_MODEL), BF16)],
        compiler_params=_params("arbitrary", "arbitrary"),
        name=name,
    )(*args)


def _conv_silu_rows(src_ref, r0, n_rows, seqlen, w_ref, bias):
    cur = src_ref[pl.ds(r0, n_rows), :].astype(F32)
    has_prev = (r0 > 0).astype(F32)
    has_next = (r0 + n_rows < seqlen).astype(F32)
    g_prev = pl.multiple_of(jnp.maximum(r0 - BF16_ROWS, 0), BF16_ROWS)
    g_next = pl.multiple_of(jnp.minimum(r0 + n_rows, seqlen - BF16_ROWS), BF16_ROWS)
    prev_row = src_ref[pl.ds(g_prev, BF16_ROWS), :].astype(F32)[BF16_ROWS - 1:BF16_ROWS] * has_prev
    next_row = src_ref[pl.ds(g_next, BF16_ROWS), :].astype(F32)[0:1] * has_next
    rid = lax.broadcasted_iota(jnp.int32, cur.shape, 0)
    xm1 = jnp.where(rid == 0, prev_row, pltpu.roll(cur, 1, 0))
    xp1 = jnp.where(rid == n_rows - 1, next_row, pltpu.roll(cur, n_rows - 1, 0))
    out = w_ref[0:1, :] * xm1 + w_ref[1:2, :] * cur + w_ref[2:3, :] * xp1
    if bias is not None:
        out = out + bias
    return _silu(out)


def _transpose_rows(rows_list):
    pad = 128 - 8 * len(rows_list)
    stack = jnp.concatenate(rows_list + [jnp.zeros((pad, 128), F32)], axis=0)
    return stack.T


def _ssd_scan_kernel(*refs, seqlen, nseq, has_init, want_final):
    refs = list(refs)
    x_ref, b_ref, c_ref, cwx, cwb, cwc, cbx, cbb, cbc, dtt_ref, prm_ref, dsk_ref = refs[:12]
    pos = 12
    if has_init:
        s0_refs = refs[pos:pos + 2]
        pos += 2
    yout_ref = refs[pos]
    pos += 1
    if want_final:
        sfin_refs = refs[pos:pos + 2]
        pos += 2
    cs_b, ea_s, ns_s, st, y_ref = refs[pos:pos + 5]

    nc = seqlen // SSD_CHUNK
    q = SSD_CHUNK

    for g in range(nseq):
        for d in range(2):
            if has_init:
                halves = []
                for h in range(2):
                    blk = jnp.concatenate([s0_refs[d][g, 2 * h], s0_refs[d][g, 2 * h + 1]], axis=0)
                    halves.append(blk.T)
                st[g, d] = jnp.concatenate(halves, axis=1)
            else:
                st[g, d] = jnp.zeros((SSD_STATE, 256), F32)

    row = lax.broadcasted_iota(jnp.int32, (q, q), 0)
    col = lax.broadcasted_iota(jnp.int32, (q, q), 1)
    lane_head = lax.broadcasted_iota(jnp.int32, (q, 256), 1) // SSD_HEADDIM
    head_masks = [jnp.where(lane_head == r, 1.0, 0.0).astype(BF16) for r in range(4)]
    bias = prm_ref[:, 0:1]
    a_neg = -jnp.exp(prm_ref[:, 1:2])
    dskip = dsk_ref[...]

    cum_ms = ((row <= col).astype(F32), (row >= col).astype(F32))
    keeps = (row >= col, row <= col)
    lasts = (q - 1, 0)

    def intra(i, carry):
        blocks = []
        for g, j in ((g, j) for g in range(nseq) for j in range(unroll_a)):
            r0 = pl.multiple_of((i * unroll_a + j) * q, q)
            xc = _conv_silu_rows(x_ref.at[g], r0, q, seqlen, cwx, cbx[...])
            bc = _conv_silu_rows(b_ref.at[g], r0, q, seqlen, cwb, cbb[...])
            cc = _conv_silu_rows(c_ref.at[g], r0, q, seqlen, cwc, cbc[...])
            cc_b = cc.astype(BF16)
            cs_b[g, pl.ds(r0, q), :] = cc_b
            bc_t = bc.T
            dt_t = _softplus(dtt_ref[g, :, pl.ds(r0, q)] + bias)
            xc_b = xc.astype(BF16)
            x_heads = jnp.concatenate([xc_b * head_masks[r] for r in range(4)], axis=0)
            blocks.append(dict(g=g, r0=r0, xc=xc, bc_t=bc_t, dt_t=dt_t, a_t=dt_t * a_neg, x_heads=x_heads,
                               scores=jnp.dot(cc_b, bc_t.astype(BF16), preferred_element_type=F32)))
        units = [(blk, d) for blk in blocks for d in range(2)]
        acums = [jnp.dot(blk["a_t"], cum_ms[d], precision=HI, preferred_element_type=F32) for blk, d in units]
        cols = _transpose_rows(acums)
        dtdes = [blk["dt_t"] * jnp.exp(ac[:, lasts[d]:lasts[d] + 1] - ac) for (blk, d), ac in zip(units, acums)]
        for (blk, d), ac in zip(units, acums):
            ea_t = jnp.exp(ac)
            ea_rows = jnp.concatenate([jnp.broadcast_to(ea_t[4 * d + r:4 * d + r + 1, :], (SSD_HEADDIM, q))
                                       for r in range(4)], axis=0)
            ea_s[blk["g"], d, pl.ds(blk["r0"], q), :] = jnp.concatenate([ea_rows[0:128].T, ea_rows[128:256].T], axis=1)
        lhss = []
        for u, ((blk, d), ac, dtde) in enumerate(zip(units, acums, dtdes)):
            parts = []
            for r in range(4):
                k = 4 * d + r
                decay = jnp.where(keeps[d], jnp.exp(cols[:, 8 * u + k:8 * u + k + 1] - ac[k:k + 1, :]), 0.0)
                m = decay * blk["scores"] * blk["dt_t"][k:k + 1, :]
                bt = blk["bc_t"] * dtde[k:k + 1, :]
                parts.append(jnp.concatenate([m, bt], axis=0).astype(BF16))
            lhss.append(jnp.concatenate(parts, axis=1))
        ress = [jnp.dot(lhs, blk["x_heads"], preferred_element_type=F32) for lhs, (blk, d) in zip(lhss, units)]
        for res, (blk, d) in zip(ress, units):
            ns_s[blk["g"], d, pl.ds(blk["r0"], q), :] = res[q:]
        for j, blk in enumerate(blocks):
            y_ref[blk["g"], pl.ds(blk["r0"], q), :] = blk["xc"] * dskip + ress[2 * j][:q] + ress[2 * j + 1][:q]
        return carry

    unroll_a = max(1, SSD_BLOCKS // nseq)
    while nc % unroll_a:
        unroll_a //= 2
    lax.fori_loop(0, nc // unroll_a, intra, 0)

    def inter(i, carry):
        jobs = []
        for g, d in ((g, d) for g in range(nseq) for d in range(2)):
            s_t = st[g, d]
            for j in range(unroll_a):
                c = i * unroll_a + j
                r0 = pl.multiple_of((c if d == 0 else nc - 1 - c) * q, q)
                jobs.append((g, d, r0, s_t.astype(BF16)))
                s_t = s_t * ea_s[g, d, pl.ds(r0 + lasts[d], 1), :] + ns_s[g, d, pl.ds(r0, q), :]
            st[g, d] = s_t
        y_inters = [jnp.dot(cs_b[g, pl.ds(r0, q), :], s_b, preferred_element_type=F32) * ea_s[g, d, pl.ds(r0, q), :]
                    for g, d, r0, s_b in jobs]
        for (g, d, r0, _), y_inter in zip(jobs, y_inters):
            y_ref[g, pl.ds(r0, q), :] += y_inter
        return carry

    lax.fori_loop(0, nc // unroll_a, inter, 0)

    def emit(c, carry):
        r0 = pl.multiple_of(c * q, q)
        for g in range(nseq):
            yout_ref[g, pl.ds(r0, q), :] = y_ref[g, pl.ds(r0, q), :].astype(yout_ref.dtype)
        return carry

    lax.fori_loop(0, nc, emit, 0)

    if want_final:
        for g, d in ((g, d) for g in range(nseq) for d in range(2)):
            s_t = st[g, d]
            for h in range(2):
                blk = s_t[:, h * 128:(h + 1) * 128].T
                sfin_refs[d][g, 2 * h] = blk[0:64]
                sfin_refs[d][g, 2 * h + 1] = blk[64:128]


def _ssd_scan_call(proj3, dtt, conv_w, conv_b, prm, dskip, s0, want_final, name):
    bsz, seqlen, _ = proj3.shape
    has_init = s0 is not None
    nseq = max(1, min(bsz, SSD_BLOCKS * SSD_CHUNK // seqlen))
    while bsz % nseq:
        nseq -= 1
    xoff = SSD_D_INNER // 256
    boff = (2 * SSD_D_INNER) // 128
    coff = boff + SSD_GROUPS
    in_specs = [
        pl.BlockSpec((nseq, seqlen, 256), lambda b, g: (b, 0, xoff + g)),
        pl.BlockSpec((nseq, seqlen, 128), lambda b, g: (b, 0, boff + g)),
        pl.BlockSpec((nseq, seqlen, 128), lambda b, g: (b, 0, coff + g)),
        pl.BlockSpec((3, 256), lambda b, g: (0, g)),
        pl.BlockSpec((3, 128), lambda b, g: (0, 16 + g)),
        pl.BlockSpec((3, 128), lambda b, g: (0, 24 + g)),
        pl.BlockSpec((1, 256), lambda b, g: (0, g)),
        pl.BlockSpec((1, 128), lambda b, g: (0, 16 + g)),
        pl.BlockSpec((1, 128), lambda b, g: (0, 24 + g)),
        pl.BlockSpec((nseq, None, 8, seqlen), lambda b, g: (b, g, 0, 0)),
        pl.BlockSpec((None, 8, 2), lambda b, g: (g, 0, 0)),
        pl.BlockSpec((None, 1, 256), lambda b, g: (g, 0, 0)),
    ]
    args = [proj3, proj3, proj3, conv_w, conv_w, conv_w, conv_b, conv_b, conv_b, dtt, prm, dskip]
    state_spec = pl.BlockSpec((nseq, 4, SSD_HEADDIM, SSD_STATE), lambda b, g: (b, g, 0, 0))
    if has_init:
        in_specs += [state_spec, state_spec]
        args += [s0[0], s0[1]]
    out_specs = [pl.BlockSpec((nseq, seqlen, 256), lambda b, g: (b, 0, g))]
    out_shape = [jax.ShapeDtypeStruct((bsz, seqlen, SSD_D_INNER), BF16)]
    if want_final:
        out_specs += [state_spec, state_spec]
        out_shape += [jax.ShapeDtypeStruct((bsz, SSD_HEADS, SSD_HEADDIM, SSD_STATE), F32)] * 2
    return pl.pallas_call(
        functools.partial(_ssd_scan_kernel, seqlen=seqlen, nseq=nseq, has_init=has_init, want_final=want_final),
        grid=(bsz // nseq, SSD_GROUPS),
        in_specs=in_specs,
        out_specs=out_specs,
        out_shape=out_shape,
        scratch_shapes=[pltpu.VMEM((nseq, seqlen, SSD_STATE), BF16),
                        pltpu.VMEM((nseq, 2, seqlen, 256), F32),
                        pltpu.VMEM((nseq, 2, seqlen, 256), F32),
                        pltpu.VMEM((nseq, 2, SSD_STATE, 256), F32),
                        pltpu.VMEM((nseq, seqlen, 256), F32)],
        compiler_params=_params("arbitrary", "arbitrary"),
        name=name,
    )(*args)


def _unit_tri_solve(a_list, rhs_list, merge_masks):
    xs = a_list
    for m in merge_masks:
        xs_b = [x.astype(BF16) for x in xs]
        xs = [x - jnp.dot(x_b, x_b * m, preferred_element_type=F32) for x, x_b in zip(xs, xs_b)]
    return [rhs - jnp.dot(x.astype(BF16), rhs.astype(BF16), preferred_element_type=F32)
            for x, rhs in zip(xs, rhs_list)]


def _gdn_scan_kernel(*refs, seqlen, nseq, has_init, want_final):
    refs = list(refs)
    q_ref, k_ref, v_ref, cwq, cwk, cwv, smt_ref, prm_ref = refs[:8]
    pos = 8
    if has_init:
        s0_refs = refs[pos:pos + 2]
        pos += 2
    oout_ref = refs[pos]
    pos += 1
    if want_final:
        sfin_refs = refs[pos:pos + 2]
        pos += 2
    u_s, wq_s, qkm_s, kdt_s, gl_s, st, o_ref = refs[pos:pos + 7]

    blk = GDN_BLOCK
    nb = seqlen // blk
    qk_scale = GDN_HEAD ** -0.5

    for g in range(nseq):
        for d in range(2):
            for r in range(2):
                if has_init:
                    st[g, d, r] = s0_refs[d][g, r]
                else:
                    st[g, d, r] = jnp.zeros((GDN_HEAD, GDN_HEAD), F32)

    row = lax.broadcasted_iota(jnp.int32, (blk, blk), 0)
    col = lax.broadcasted_iota(jnp.int32, (blk, blk), 1)
    merge_masks = [jnp.where(((row >> (l + 1)) == (col >> (l + 1))) & ((row >> l) != (col >> l)), 1.0, 0.0
                             ).astype(BF16) for l in range(7)]
    bias = prm_ref[:, 0:1]
    a_neg = -jnp.exp(prm_ref[:, 1:2])
    nt = (((1,), (1,)), ((), ()))

    cum_ms = ((row <= col).astype(F32), (row >= col).astype(F32))
    stricts = (row > col, row < col)
    incls = (row >= col, row <= col)
    lasts = (blk - 1, 0)

    def load_block(g, c):
        r0 = pl.multiple_of(c * blk, blk)
        qv = _conv_silu_rows(q_ref.at[g], r0, blk, seqlen, cwq, None)
        kv = _conv_silu_rows(k_ref.at[g], r0, blk, seqlen, cwk, None)
        qc = qv * (lax.rsqrt(jnp.sum(qv * qv, axis=-1, keepdims=True) + RMS_EPS) * qk_scale)
        kc = kv * lax.rsqrt(jnp.sum(kv * kv, axis=-1, keepdims=True) + RMS_EPS)
        vc = _conv_silu_rows(v_ref.at[g], r0, blk, seqlen, cwv, None)
        o_ref[g, pl.ds(r0, blk), :] = jnp.zeros((blk, 256), F32)
        sm = smt_ref[g, :, pl.ds(r0, blk)]
        return dict(g=g, r0=r0, wq0=pl.multiple_of(c * 2 * blk, 2 * blk), g0=pl.multiple_of(c * 8, 8),
                    qc=qc, kc=kc, vc=vc, g_t=a_neg * _softplus(sm + bias), beta_t=_sigmoid(sm))

    def prepare(blocks):
        for b in blocks:
            kc_b = b["kc"].astype(BF16)
            b["kk"] = lax.dot_general(kc_b, kc_b, nt, preferred_element_type=F32)
            b["qk"] = lax.dot_general(b["qc"].astype(BF16), kc_b, nt, preferred_element_type=F32)
        units = [(b, d) for b in blocks for d in range(2)]
        gcs = [jnp.dot(b["g_t"], cum_ms[d], precision=HI, preferred_element_type=F32) for b, d in units]
        tots = [gc[:, lasts[d]:lasts[d] + 1] for (b, d), gc in zip(units, gcs)]
        colss = [_transpose_rows([gc, jnp.exp(gc), b["beta_t"]]) for (b, d), gc in zip(units, gcs)]
        for (b, d), tot in zip(units, tots):
            gl_s[b["g"], d, pl.ds(b["g0"], 8), :] = jnp.broadcast_to(jnp.exp(tot), (8, blk))
        for b in blocks:
            b["kc_t"] = b["kc"].T
        chains = []
        for (b, d), gc, tot, cols in zip(units, gcs, tots, colss):
            kd_t = jnp.exp(tot - gc)
            for r in range(2):
                k = 2 * d + r
                e = jnp.exp(cols[:, k:k + 1] - gc[k:k + 1, :])
                beta = jnp.broadcast_to(cols[:, 20 + k:21 + k], (blk, GDN_HEAD))
                eg = jnp.broadcast_to(cols[:, 8 + k:9 + k], (blk, GDN_HEAD))
                vr = b["vc"][:, r * 128:(r + 1) * 128]
                g = b["g"]
                wq_s[g, d, r, pl.ds(b["wq0"] + blk, blk), :] = (b["qc"] * eg).astype(BF16)
                qkm_s[g, d, r, pl.ds(b["r0"], blk), :] = (b["qk"] * jnp.where(incls[d], e, 0.0)).astype(BF16)
                kdt_s[g, d, r, pl.ds(b["r0"], blk), :] = (b["kc_t"] * kd_t[k:k + 1, :]).astype(BF16)
                chains.append(dict(
                    g=g, d=d, r=r, r0=b["r0"], wq0=b["wq0"],
                    a=beta * b["kk"] * jnp.where(stricts[d], e, 0.0),
                    rhs=jnp.concatenate([vr * beta, b["kc"] * (beta * eg)], axis=1)))
        return chains

    chains_per_block = 4
    per_iter = max(1, GDN_CHAINS // (chains_per_block * nseq))
    while nb % per_iter:
        per_iter //= 2

    def phase1(i, carry):
        chains = prepare([load_block(g, i * per_iter + j) for g in range(nseq) for j in range(per_iter)])
        sols = _unit_tri_solve([ch["a"] for ch in chains], [ch["rhs"] for ch in chains], merge_masks)
        for sol, ch in zip(sols, chains):
            u_s[ch["g"], ch["d"], ch["r"], pl.ds(ch["r0"], blk), :] = sol[:, :GDN_HEAD]
            wq_s[ch["g"], ch["d"], ch["r"], pl.ds(ch["wq0"], blk), :] = sol[:, GDN_HEAD:].astype(BF16)
        return carry

    lax.fori_loop(0, nb // per_iter, phase1, 0)

    def phase2(i, carry):
        jobs = []
        for g in range(nseq):
            for d, c in ((0, i), (1, nb - 1 - i)):
                r0 = pl.multiple_of(c * blk, blk)
                wq0 = pl.multiple_of(c * 2 * blk, 2 * blk)
                gl = gl_s[g, d, pl.ds(pl.multiple_of(c * 8, 8), 8), :]
                for r in range(2):
                    jobs.append(((g, d, r), r0, wq0, gl[2 * d + r:2 * d + r + 1, :]))
        ss = [st[idx] for idx, _, _, _ in jobs]
        wss = [jnp.dot(wq_s[idx + (pl.ds(wq0, 2 * blk), slice(None))], s.astype(BF16), preferred_element_type=F32)
               for (idx, _, wq0, _), s in zip(jobs, ss)]
        vns = [(u_s[idx + (pl.ds(r0, blk), slice(None))] - ws[:blk]).astype(BF16)
               for (idx, r0, _, _), ws in zip(jobs, wss)]
        for (idx, r0, _, gl), s, vn in zip(jobs, ss, vns):
            st[idx] = gl * s + jnp.dot(kdt_s[idx + (pl.ds(r0, blk), slice(None))], vn, preferred_element_type=F32)
        outs = [ws[blk:] + jnp.dot(qkm_s[idx + (pl.ds(r0, blk), slice(None))], vn, preferred_element_type=F32)
                for (idx, r0, _, _), ws, vn in zip(jobs, wss, vns)]
        for pair in range(2 * nseq):
            (g, _, _), r0 = jobs[2 * pair][0], jobs[2 * pair][1]
            o_ref[g, pl.ds(r0, blk), :] += jnp.concatenate(outs[2 * pair:2 * pair + 2], axis=1)
        return carry

    lax.fori_loop(0, nb, phase2, 0)

    def emit(c, carry):
        r0 = pl.multiple_of(c * blk, blk)
        for g in range(nseq):
            oout_ref[g, pl.ds(r0, blk), :] = o_ref[g, pl.ds(r0, blk), :].astype(oout_ref.dtype)
        return carry

    lax.fori_loop(0, nb, emit, 0)

    if want_final:
        for g in range(nseq):
            for d in range(2):
                for r in range(2):
                    sfin_refs[d][g, r] = st[g, d, r]


def _gdn_scan_call(proj3, smt, conv_w, prm, s0, want_final, name):
    bsz, seqlen, _ = proj3.shape
    has_init = s0 is not None
    nseq = max(2, min(bsz, GDN_CHAINS * GDN_BLOCK // (4 * seqlen)))
    while bsz % nseq:
        nseq -= 1
    in_specs = [
        pl.BlockSpec((nseq, seqlen, 128), lambda b, g: (b, 0, g)),
        pl.BlockSpec((nseq, seqlen, 128), lambda b, g: (b, 0, GDN_K_HEADS + g)),
        pl.BlockSpec((nseq, seqlen, 256), lambda b, g: (b, 0, GDN_K_HEADS + g)),
        pl.BlockSpec((3, 128), lambda b, g: (0, g)),
        pl.BlockSpec((3, 128), lambda b, g: (0, GDN_K_HEADS + g)),
        pl.BlockSpec((3, 256), lambda b, g: (0, GDN_K_HEADS + g)),
        pl.BlockSpec((nseq, None, 8, seqlen), lambda b, g: (b, g, 0, 0)),
        pl.BlockSpec((None, 8, 2), lambda b, g: (g, 0, 0)),
    ]
    args = [proj3, proj3, proj3, conv_w, conv_w, conv_w, smt, prm]
    state_spec = pl.BlockSpec((nseq, 2, GDN_HEAD, GDN_HEAD), lambda b, g: (b, g, 0, 0))
    if has_init:
        in_specs += [state_spec, state_spec]
        args += [s0[0], s0[1]]
    out_specs = [pl.BlockSpec((nseq, seqlen, 256), lambda b, g: (b, 0, g))]
    out_shape = [jax.ShapeDtypeStruct((bsz, seqlen, GDN_V), BF16)]
    if want_final:
        out_specs += [state_spec, state_spec]
        out_shape += [jax.ShapeDtypeStruct((bsz, GDN_V_HEADS, GDN_HEAD, GDN_HEAD), F32)] * 2
    return pl.pallas_call(
        functools.partial(_gdn_scan_kernel, seqlen=seqlen, nseq=nseq, has_init=has_init, want_final=want_final),
        grid=(bsz // nseq, GDN_K_HEADS),
        in_specs=in_specs,
        out_specs=out_specs,
        out_shape=out_shape,
        scratch_shapes=[pltpu.VMEM((nseq, 2, 2, seqlen, GDN_HEAD), F32),
                        pltpu.VMEM((nseq, 2, 2, 2 * seqlen, GDN_HEAD), BF16),
                        pltpu.VMEM((nseq, 2, 2, seqlen, GDN_BLOCK), BF16),
                        pltpu.VMEM((nseq, 2, 2, seqlen, GDN_BLOCK), BF16),
                        pltpu.VMEM((nseq, 2, 8 * (seqlen // GDN_BLOCK), GDN_BLOCK), F32),
                        pltpu.VMEM((nseq, 2, 2, GDN_HEAD, GDN_HEAD), F32),
                        pltpu.VMEM((nseq, seqlen, 256), F32)],
        compiler_params=_params("arbitrary", "arbitrary"),
        name=name,
    )(*args)


def _residual_layer_norm(x, gate, o, ln_g, ln_b):
    r = DEEPNORM_ALPHA * x + gate * o
    mu = jnp.mean(r, axis=-1, keepdims=True)
    rc = r - mu
    var = jnp.mean(rc * rc, axis=-1, keepdims=True)
    return rc * lax.rsqrt(var + LN_EPS) * ln_g + ln_b


def _out_kernel(y_ref, z_ref, nw_ref, w_ref, x_ref, gt_ref, lg_ref, lb_ref, o_ref, *, mode):
    if mode == "ssd":
        u = y_ref[...].astype(F32) * _silu(z_ref[...].astype(F32))
        ms = jnp.mean(u * u, axis=-1, keepdims=True)
        a = (u * lax.rsqrt(ms + RMS_EPS) * nw_ref[...]).astype(BF16)
    else:
        parts = []
        for h in range(GDN_V_HEADS):
            sl = slice(h * GDN_HEAD, (h + 1) * GDN_HEAD)
            oh = y_ref[:, sl].astype(F32)
            ms = jnp.mean(oh * oh, axis=-1, keepdims=True)
            parts.append((oh * lax.rsqrt(ms + RMS_EPS) * nw_ref[...] * _silu(z_ref[:, sl].astype(F32))).astype(BF16))
        a = jnp.concatenate(parts, axis=1)
    o = jnp.dot(a, w_ref[...], preferred_element_type=F32)
    o_ref[...] = _residual_layer_norm(x_ref[...], gt_ref[...], o, lg_ref[...], lb_ref[...])


def _out_call(mode, act, z_src, z_blk, norm_w, w, x, gate, ln_g, ln_b, rows_per_mod, tm, name):
    m, kdim = act.shape
    row_spec = pl.BlockSpec((tm, D_MODEL), lambda i: (i, 0))
    vec_spec = pl.BlockSpec((1, D_MODEL), lambda i: (0, 0))
    in_specs = [pl.BlockSpec((tm, kdim), lambda i: (i, 0)),
                pl.BlockSpec((tm, kdim), lambda i: (i, z_blk)),
                pl.BlockSpec((1, norm_w.shape[1]), lambda i: (0, 0)),
                pl.BlockSpec((kdim, D_MODEL), lambda i: (0, 0)), row_spec,
                pl.BlockSpec((None, 1, D_MODEL), lambda i: ((i * tm) // rows_per_mod, 0, 0)),
                vec_spec, vec_spec]
    args = [act, z_src, norm_w, w, x, gate, ln_g, ln_b]
    return pl.pallas_call(
        functools.partial(_out_kernel, mode=mode),
        grid=(m // tm,),
        in_specs=in_specs,
        out_specs=row_spec,
        out_shape=jax.ShapeDtypeStruct((m, D_MODEL), F32),
        compiler_params=_params("arbitrary"),
        name=name,
    )(*args)


def _ffn_down_kernel(*refs, tm, period, row_taps, tiles_per_seq):
    if row_taps:
        (a_ref, ap_ref, an_ref, v_ref, cw_ref, cb_ref, w_ref, x_ref, gt_ref, lg_ref, lb_ref,
         o_ref, acc_ref, ext_ref) = refs
        t = pl.program_id(0) % tiles_per_seq
        has_prev = (t > 0).astype(F32)
        has_next = (t < tiles_per_seq - 1).astype(F32)
        base, taps, n = HALO, (0, 1, 2), tm + 2 * HALO
    else:
        a_ref, v_ref, cw_ref, cb_ref, w_ref, x_ref, gt_ref, lg_ref, lb_ref, o_ref, acc_ref = refs
        base, taps, n = 0, (1,), tm
    colpos = lax.broadcasted_iota(jnp.int32, (n, FFN_TC), 0) % period
    is_first = colpos == 0
    is_last = colpos == period - 1
    n_ct = D_FF // FFN_TC

    def gated_tile(c):
        cols = pl.ds(pl.multiple_of(c * FFN_TC, FFN_TC), FFN_TC)
        if row_taps:
            ext_ref[0:HALO, :] = ap_ref[:, cols].astype(F32) * has_prev
            ext_ref[HALO:HALO + tm, :] = a_ref[:, cols].astype(F32)
            ext_ref[HALO + tm:, :] = an_ref[:, cols].astype(F32) * has_next
            ext = ext_ref[...]
        else:
            ext = a_ref[:, cols].astype(F32)
        em1 = jnp.where(is_first, 0.0, pltpu.roll(ext, 1, 0))
        ep1 = jnp.where(is_last, 0.0, pltpu.roll(ext, n - 1, 0))
        conv = cb_ref[:, cols]
        for dr in taps:
            off = base + (dr - 1) * GRID_W
            conv = (conv + cw_ref[3 * dr:3 * dr + 1, cols] * em1[off:off + tm]
                    + cw_ref[3 * dr + 1:3 * dr + 2, cols] * ext[off:off + tm]
                    + cw_ref[3 * dr + 2:3 * dr + 3, cols] * ep1[off:off + tm])
        return (_silu(conv) * v_ref[:, cols].astype(F32)).astype(BF16)

    acc_ref[...] = jnp.zeros((tm, D_MODEL), F32)

    def channel_tile(c, carry):
        rows = pl.ds(pl.multiple_of(c * FFN_TC, FFN_TC), FFN_TC)
        acc_ref[...] += jnp.dot(gated_tile(c), w_ref[rows, :], preferred_element_type=F32)
        return carry

    lax.fori_loop(0, n_ct, channel_tile, 0)
    o_ref[...] = _residual_layer_norm(x_ref[...], gt_ref[...], acc_ref[...], lg_ref[...], lb_ref[...])


def _ffn_down_call(up, conv_w9, conv_b, w_down, x, gate, ln_g, ln_b, seqlen, is_grid, rows_per_mod, tm, name):
    m = up.shape[0]
    a_spec = pl.BlockSpec((tm, D_FF), lambda i: (i, 0))
    v_spec = pl.BlockSpec((tm, D_FF), lambda i: (i, 1))
    scratch = [pltpu.VMEM((tm, D_MODEL), F32)]
    if is_grid:
        scratch.append(pltpu.VMEM((tm + 2 * HALO, FFN_TC), F32))
        hb = tm // HALO
        nhb = m // HALO
        tiles_per_seq = seqlen // tm
        in_specs = [a_spec,
                    pl.BlockSpec((HALO, D_FF), lambda i: (jnp.maximum(i * hb - 1, 0), 0)),
                    pl.BlockSpec((HALO, D_FF), lambda i: (jnp.minimum((i + 1) * hb, nhb - 1), 0)),
                    v_spec]
        args = [up, up, up, up]
        period = GRID_W
    else:
        tiles_per_seq = 1
        in_specs = [a_spec, v_spec]
        args = [up, up]
        period = seqlen
    row_spec = pl.BlockSpec((tm, D_MODEL), lambda i: (i, 0))
    vec_spec = pl.BlockSpec((1, D_MODEL), lambda i: (0, 0))
    in_specs += [pl.BlockSpec((9, D_FF), lambda i: (0, 0)), pl.BlockSpec((1, D_FF), lambda i: (0, 0)),
                 pl.BlockSpec((D_FF, D_MODEL), lambda i: (0, 0)), row_spec,
                 pl.BlockSpec((None, 1, D_MODEL), lambda i: ((i * tm) // rows_per_mod, 0, 0)),
                 vec_spec, vec_spec]
    args += [conv_w9, conv_b, w_down, x, gate, ln_g, ln_b]
    return pl.pallas_call(
        functools.partial(_ffn_down_kernel, tm=tm, period=period, row_taps=is_grid, tiles_per_seq=tiles_per_seq),
        grid=(m // tm,),
        in_specs=in_specs,
        out_specs=row_spec,
        out_shape=jax.ShapeDtypeStruct((m, D_MODEL), F32),
        scratch_shapes=scratch,
        compiler_params=_params("arbitrary"),
        name=name,
    )(*args)


def _group_rows(t, bsz, seqlen, lead):
    n_lead = int(math.prod(lead))
    t = t.reshape(bsz, seqlen, n_lead, 8, -1)
    r = t.shape[-1]
    t = jnp.transpose(t, (0, 3, 2, 4, 1))
    return t.reshape(bsz, 8, n_lead * r, seqlen)


def _group_params(p):
    r = p.shape[1] // 8
    return jnp.transpose(p.reshape(2, 8, r), (1, 0, 2)).reshape(8, 2 * r)


def _run_tokens(x, seqlen, mods, states, want_final, w, tag):
    bsz = x.shape[0]
    m = bsz * seqlen
    nb = mods.shape[1]
    rows_per_mod = m // nb
    xt = x.reshape(m, D_MODEL)
    finals = []
    is_grid = tag == "s"
    for i in range(DEPTH):
        def mod(k):
            return mods[i, :, k].reshape(nb, 1, D_MODEL)
        lw = w[i]
        proj, small = _proj_call(xt, mod(1), mod(0), lw["w_in"], lw["w_in_small"], rows_per_mod,
                                 tm=1024, tn=1536, name=f"inproj{i}{tag}")
        proj3 = proj.reshape(bsz, seqlen, 6144)
        if i % 2 == 0:
            dtt = _group_rows(small[:, :2 * SSD_HEADS], bsz, seqlen, (2,))
            res = _ssd_scan_call(proj3, dtt, lw["conv_w"], lw["conv_b"], lw["prm"], lw["dskip"],
                                 states[i], want_final, name=f"ssdscan{tag}")
            mode, z_blk = "ssd", 0
        else:
            smt = _group_rows(small[:, :4 * GDN_V_HEADS], bsz, seqlen, (2, 2))
            res = _gdn_scan_call(proj3, smt, lw["conv_w"], lw["prm"], states[i], want_final,
                                 name=f"gdnscan{tag}")
            mode, z_blk = "gdn", 2
        y = res[0].reshape(m, 2048)
        if want_final:
            finals.append((res[1], res[2]))
        xt = _out_call(mode, y, proj, z_blk, lw["norm_w"], lw["w_out"], xt, mod(2), lw["ln_g0"], lw["ln_b0"],
                       rows_per_mod, tm=512, name=f"outproj{i}{tag}")
        up = _proj_call(xt, mod(4), mod(3), lw["w_up"], None, rows_per_mod,
                        tm=1024, tn=1408, name=f"ffnup{i}{tag}")[0]
        xt = _ffn_down_call(up, lw["ffn_conv_w"], lw["ffn_conv_b"], lw["w_down"], xt, mod(5),
                            lw["ln_g1"], lw["ln_b1"], seqlen, is_grid, rows_per_mod, tm=512,
                            name=f"ffndown{i}{tag}")
    return xt.reshape(bsz, seqlen, D_MODEL), finals


def kernel(x_prompt, x_sample, state_ssd_fwd, state_ssd_bwd, state_gdn_fwd, state_gdn_bwd, c, c_ctx, w_ada, b_ada, ln_g, ln_b, ssd_w_in, ssd_conv_w, ssd_conv_b, ssd_dt_bias, ssd_a_log, ssd_d, ssd_norm_w, ssd_w_out, gdn_w_in, gdn_conv_w, gdn_dt_bias, gdn_a_log, gdn_norm_w, gdn_w_out, ffn_w_up, ffn_conv_w, ffn_conv_b, ffn_w_down):
    n_dec = c.shape[0]
    cond = jnp.concatenate([c_ctx[None, :], c, jnp.zeros((16 - 1 - n_dec, D_MODEL), F32)], axis=0)
    ada = _ada_call(cond, w_ada, b_ada).reshape(DEPTH, 16, 6, D_MODEL)
    mods_p = ada[:, 0:1]
    mods_s = ada[:, 1:1 + n_dec]

    def small_cols(wm):
        return jnp.pad(wm, ((0, 0), (0, 128 - wm.shape[1]))).astype(BF16)

    weights = []
    for i in range(DEPTH):
        j = i // 2
        lw = {
            "ln_g0": ln_g[i, 0:1], "ln_b0": ln_b[i, 0:1], "ln_g1": ln_g[i, 1:2], "ln_b1": ln_b[i, 1:2],
            "w_up": ffn_w_up[i].astype(BF16), "w_down": ffn_w_down[i].astype(BF16),
            "ffn_conv_w": ffn_conv_w[i].reshape(9, D_FF), "ffn_conv_b": ffn_conv_b[i].reshape(1, D_FF),
        }
        if i % 2 == 0:
            lw["w_in"] = ssd_w_in[j][:, :6144].astype(BF16)
            lw["w_in_small"] = small_cols(ssd_w_in[j][:, 6144:])
            lw["conv_w"] = ssd_conv_w[j]
            lw["conv_b"] = ssd_conv_b[j].reshape(1, SSD_XBC)
            lw["prm"] = jnp.stack([_group_params(ssd_dt_bias[j]), _group_params(ssd_a_log[j])], axis=-1)
            lw["dskip"] = jnp.repeat(ssd_d[j], SSD_HEADDIM).reshape(SSD_GROUPS, 1, 256)
            lw["norm_w"] = ssd_norm_w[j].reshape(1, SSD_D_INNER)
            lw["w_out"] = ssd_w_out[j].astype(BF16)
        else:
            lw["w_in"] = gdn_w_in[j][:, :6144].astype(BF16)
            lw["w_in_small"] = small_cols(gdn_w_in[j][:, 6144:])
            lw["conv_w"] = gdn_conv_w[j]
            prm = jnp.stack([_group_params(gdn_dt_bias[j]), _group_params(gdn_a_log[j])], axis=-1)
            lw["prm"] = jnp.pad(prm, ((0, 0), (0, 4), (0, 0)))
            lw["norm_w"] = gdn_norm_w[j].reshape(1, GDN_HEAD)
            lw["w_out"] = gdn_w_out[j].astype(BF16)
        weights.append(lw)

    states_s = [(state_ssd_fwd[:, 0], state_ssd_bwd[:, 0]), (state_gdn_fwd[:, 0], state_gdn_bwd[:, 0])]
    y_prompt, fin = _run_tokens(x_prompt, x_prompt.shape[1], mods_p, [None, None], True, weights, "p")
    y_sample, _ = _run_tokens(x_sample, x_sample.shape[1], mods_s, states_s, False, weights, "s")
    return (y_prompt, y_sample,
            fin[0][0][:, None], fin[0][1][:, None], fin[1][0][:, None], fin[1][1][:, None])
```

```python
import functools
import math

import jax
import jax.numpy as jnp
from jax import lax
from jax.experimental import pallas as pl
from jax.experimental.pallas import tpu as pltpu

F32 = jnp.float32
BF16 = jnp.bfloat16
HI = lax.Precision.HIGHEST

D_MODEL = 1024
DEPTH = 2
GRID_W = 64

SSD_D_INNER = 2048
SSD_HEADDIM = 64
SSD_HEADS = 32
SSD_GROUPS = 8
SSD_STATE = 128
SSD_CHUNK = 128
SSD_BLOCKS = 4
SSD_XBC = 4096

GDN_K_HEADS = 8
GDN_V_HEADS = 16
GDN_HEAD = 128
GDN_QK = 1024
GDN_V = 2048
GDN_QKV = 4096
GDN_BLOCK = 128
GDN_CHAINS = 16

D_FF = 2816
FFN_TC = 256
HALO = 128

DEEPNORM_ALPHA = (2.0 * DEPTH) ** 0.25
LN_EPS = 1e-5
RMS_EPS = 1e-6

VMEM_LIMIT = 56 * 1024 * 1024
BF16_ROWS = 16


def _silu(x):
    h = 0.5 * x
    return h + h * jnp.tanh(h)


def _sigmoid(x):
    return 0.5 + 0.5 * jnp.tanh(0.5 * x)


def _softplus(x):
    return jnp.maximum(x, 0.0) + jnp.log(1.0 + jnp.exp(-jnp.abs(x)))


def _params(*sem):
    return pltpu.CompilerParams(dimension_semantics=sem, vmem_limit_bytes=VMEM_LIMIT)


def _ada_kernel(c_ref, w_ref, b_ref, o_ref):
    h = _silu(c_ref[...]).astype(BF16)
    o_ref[...] = jnp.dot(h, w_ref[...].astype(BF16), preferred_element_type=F32) + b_ref[...]


def _ada_call(cond, w_ada, b_ada):
    rows = cond.shape[0]
    tn = 1536
    return pl.pallas_call(
        _ada_kernel,
        grid=(DEPTH, 6 * D_MODEL // tn),
        in_specs=[
            pl.BlockSpec((rows, D_MODEL), lambda l, j: (0, 0)),
            pl.BlockSpec((None, D_MODEL, tn), lambda l, j: (l, 0, j)),
            pl.BlockSpec((None, 1, tn), lambda l, j: (l, 0, j)),
        ],
        out_specs=pl.BlockSpec((None, rows, tn), lambda l, j: (l, 0, j)),
        out_shape=jax.ShapeDtypeStruct((DEPTH, rows, 6 * D_MODEL), F32),
        compiler_params=_params("arbitrary", "arbitrary"),
        name="adaln",
    )(cond, w_ada, b_ada.reshape(DEPTH, 1, 6 * D_MODEL))


def _proj_kernel(*refs, has_small):
    if has_small:
        x_ref, sc_ref, sh_ref, w_ref, w2_ref, o_ref, o2_ref, h_ref = refs
    else:
        x_ref, sc_ref, sh_ref, w_ref, o_ref, h_ref = refs

    @pl.when(pl.program_id(1) == 0)
    def _():
        h = x_ref[...] * (1.0 + sc_ref[...]) + sh_ref[...]
        h_ref[...] = h.astype(BF16)
        if has_small:
            o2_ref[...] = jnp.dot(h_ref[...], w2_ref[...], preferred_element_type=F32)

    o_ref[...] = jnp.dot(h_ref[...], w_ref[...], preferred_element_type=F32).astype(o_ref.dtype)


def _proj_call(x, scale, shift, w, w_small, rows_per_mod, tm, tn, name):
    m = x.shape[0]
    n = (w.shape[1] // tn) * tn
    has_small = w_small is not None
    mod_spec = pl.BlockSpec((None, 1, D_MODEL), lambda i, j: ((i * tm) // rows_per_mod, 0, 0))
    in_specs = [pl.BlockSpec((tm, D_MODEL), lambda i, j: (i, 0)), mod_spec, mod_spec,
                pl.BlockSpec((D_MODEL, tn), lambda i, j: (0, j))]
    out_specs = [pl.BlockSpec((tm, tn), lambda i, j: (i, j))]
    out_shape = [jax.ShapeDtypeStruct((m, n), BF16)]
    args = [x, scale, shift, w]
    if has_small:
        in_specs.append(pl.BlockSpec((D_MODEL, 128), lambda i, j: (0, 0)))
        out_specs.append(pl.BlockSpec((tm, 128), lambda i, j: (i, 0)))
        out_shape.append(jax.ShapeDtypeStruct((m, 128), F32))
        args.append(w_small)
    return pl.pallas_call(
        functools.partial(_proj_kernel, has_small=has_small),
        grid=(m // tm, n // tn),
        in_specs=in_specs,
        out_specs=out_specs,
        out_shape=out_shape,
        scratch_shapes=[pltpu.VMEM((tm, D_MODEL), BF16)],
        compiler_params=_params("arbitrary", "arbitrary"),
        name=name,
    )(*args)


def _conv_silu_rows(src_ref, r0, n_rows, seqlen, w_ref, bias):
    cur = src_ref[pl.ds(r0, n_rows), :].astype(F32)
    has_prev = (r0 > 0).astype(F32)
    has_next = (r0 + n_rows < seqlen).astype(F32)
    g_prev = pl.multiple_of(jnp.maximum(r0 - BF16_ROWS, 0), BF16_ROWS)
    g_next = pl.multiple_of(jnp.minimum(r0 + n_rows, seqlen - BF16_ROWS), BF16_ROWS)
    prev_row = src_ref[pl.ds(g_prev, BF16_ROWS), :].astype(F32)[BF16_ROWS - 1:BF16_ROWS] * has_prev
    next_row = src_ref[pl.ds(g_next, BF16_ROWS), :].astype(F32)[0:1] * has_next
    rid = lax.broadcasted_iota(jnp.int32, cur.shape, 0)
    xm1 = jnp.where(rid == 0, prev_row, pltpu.roll(cur, 1, 0))
    xp1 = jnp.where(rid == n_rows - 1, next_row, pltpu.roll(cur, n_rows - 1, 0))
    out = w_ref[0:1, :] * xm1 + w_ref[1:2, :] * cur + w_ref[2:3, :] * xp1
    if bias is not None:
        out = out + bias
    return _silu(out)


def _transpose_rows(rows_list):
    pad = 128 - 8 * len(rows_list)
    stack = jnp.concatenate(rows_list + [jnp.zeros((pad, 128), F32)], axis=0)
    return stack.T


def _ssd_scan_kernel(*refs, seqlen, nseq, has_init, want_final):
    refs = list(refs)
    x_ref, b_ref, c_ref, cwx, cwb, cwc, cbx, cbb, cbc, dtt_ref, prm_ref, dsk_ref = refs[:12]
    pos = 12
    if has_init:
        s0_refs = refs[pos:pos + 2]
        pos += 2
    yout_ref = refs[pos]
    pos += 1
    if want_final:
        sfin_refs = refs[pos:pos + 2]
        pos += 2
    cs_b, ea_s, ns_s, st, y_ref = refs[pos:pos + 5]

    nc = seqlen // SSD_CHUNK
    q = SSD_CHUNK

    for g in range(nseq):
        for d in range(2):
            if has_init:
                halves = []
                for h in range(2):
                    blk = jnp.concatenate([s0_refs[d][g, 2 * h], s0_refs[d][g, 2 * h + 1]], axis=0)
                    halves.append(blk.T)
                st[g, d] = jnp.concatenate(halves, axis=1)
            else:
                st[g, d] = jnp.zeros((SSD_STATE, 256), F32)

    row = lax.broadcasted_iota(jnp.int32, (q, q), 0)
    col = lax.broadcasted_iota(jnp.int32, (q, q), 1)
    lane_head = lax.broadcasted_iota(jnp.int32, (q, 256), 1) // SSD_HEADDIM
    head_masks = [jnp.where(lane_head == r, 1.0, 0.0).astype(BF16) for r in range(4)]
    bias = prm_ref[:, 0:1]
    a_neg = -jnp.exp(prm_ref[:, 1:2])
    dskip = dsk_ref[...]

    cum_ms = ((row <= col).astype(F32), (row >= col).astype(F32))
    keeps = (row >= col, row <= col)
    lasts = (q - 1, 0)

    def intra(i, carry):
        blocks = []
        for g, j in ((g, j) for g in range(nseq) for j in range(unroll_a)):
            r0 = pl.multiple_of((i * unroll_a + j) * q, q)
            xc = _conv_silu_rows(x_ref.at[g], r0, q, seqlen, cwx, cbx[...])
            bc = _conv_silu_rows(b_ref.at[g], r0, q, seqlen, cwb, cbb[...])
            cc = _conv_silu_rows(c_ref.at[g], r0, q, seqlen, cwc, cbc[...])
            cc_b = cc.astype(BF16)
            cs_b[g, pl.ds(r0, q), :] = cc_b
            bc_t = bc.T
            dt_t = _softplus(dtt_ref[g, :, pl.ds(r0, q)] + bias)
            xc_b = xc.astype(BF16)
            x_heads = jnp.concatenate([xc_b * head_masks[r] for r in range(4)], axis=0)
            blocks.append(dict(g=g, r0=r0, xc=xc, bc_t=bc_t, dt_t=dt_t, a_t=dt_t * a_neg, x_heads=x_heads,
                               scores=jnp.dot(cc_b, bc_t.astype(BF16), preferred_element_type=F32)))
        units = [(blk, d) for blk in blocks for d in range(2)]
        acums = [jnp.dot(blk["a_t"], cum_ms[d], precision=HI, preferred_element_type=F32) for blk, d in units]
        cols = _transpose_rows(acums)
        dtdes = [blk["dt_t"] * jnp.exp(ac[:, lasts[d]:lasts[d] + 1] - ac) for (blk, d), ac in zip(units, acums)]
        for (blk, d), ac in zip(units, acums):
            ea_t = jnp.exp(ac)
            ea_rows = jnp.concatenate([jnp.broadcast_to(ea_t[4 * d + r:4 * d + r + 1, :], (SSD_HEADDIM, q))
                                       for r in range(4)], axis=0)
            ea_s[blk["g"], d, pl.ds(blk["r0"], q), :] = jnp.concatenate([ea_rows[0:128].T, ea_rows[128:256].T], axis=1)
        lhss = []
        for u, ((blk, d), ac, dtde) in enumerate(zip(units, acums, dtdes)):
            parts = []
            for r in range(4):
                k = 4 * d + r
                decay = jnp.where(keeps[d], jnp.exp(cols[:, 8 * u + k:8 * u + k + 1] - ac[k:k + 1, :]), 0.0)
                m = decay * blk["scores"] * blk["dt_t"][k:k + 1, :]
                bt = blk["bc_t"] * dtde[k:k + 1, :]
                parts.append(jnp.concatenate([m, bt], axis=0).astype(BF16))
            lhss.append(jnp.concatenate(parts, axis=1))
        ress = [jnp.dot(lhs, blk["x_heads"], preferred_element_type=F32) for lhs, (blk, d) in zip(lhss, units)]
        for res, (blk, d) in zip(ress, units):
            ns_s[blk["g"], d, pl.ds(blk["r0"], q), :] = res[q:]
        for j, blk in enumerate(blocks):
            y_ref[blk["g"], pl.ds(blk["r0"], q), :] = blk["xc"] * dskip + ress[2 * j][:q] + ress[2 * j + 1][:q]
        return carry

    unroll_a = max(1, SSD_BLOCKS // nseq)
    while nc % unroll_a:
        unroll_a //= 2
    lax.fori_loop(0, nc // unroll_a, intra, 0)

    def inter(i, carry):
        jobs = []
        for g, d in ((g, d) for g in range(nseq) for d in range(2)):
            s_t = st[g, d]
            for j in range(unroll_a):
                c = i * unroll_a + j
                r0 = pl.multiple_of((c if d == 0 else nc - 1 - c) * q, q)
                jobs.append((g, d, r0, s_t.astype(BF16)))
                s_t = s_t * ea_s[g, d, pl.ds(r0 + lasts[d], 1), :] + ns_s[g, d, pl.ds(r0, q), :]
            st[g, d] = s_t
        y_inters = [jnp.dot(cs_b[g, pl.ds(r0, q), :], s_b, preferred_element_type=F32) * ea_s[g, d, pl.ds(r0, q), :]
                    for g, d, r0, s_b in jobs]
        for (g, d, r0, _), y_inter in zip(jobs, y_inters):
            y_ref[g, pl.ds(r0, q), :] += y_inter
        return carry

    lax.fori_loop(0, nc // unroll_a, inter, 0)

    def emit(c, carry):
        r0 = pl.multiple_of(c * q, q)
        for g in range(nseq):
            yout_ref[g, pl.ds(r0, q), :] = y_ref[g, pl.ds(r0, q), :].astype(yout_ref.dtype)
        return carry

    lax.fori_loop(0, nc, emit, 0)

    if want_final:
        for g, d in ((g, d) for g in range(nseq) for d in range(2)):
            s_t = st[g, d]
            for h in range(2):
                blk = s_t[:, h * 128:(h + 1) * 128].T
                sfin_refs[d][g, 2 * h] = blk[0:64]
                sfin_refs[d][g, 2 * h + 1] = blk[64:128]


def _ssd_scan_call(proj3, dtt, conv_w, conv_b, prm, dskip, s0, want_final, name):
    bsz, seqlen, _ = proj3.shape
    has_init = s0 is not None
    nseq = max(1, min(bsz, SSD_BLOCKS * SSD_CHUNK // seqlen))
    while bsz % nseq:
        nseq -= 1
    xoff = SSD_D_INNER // 256
    boff = (2 * SSD_D_INNER) // 128
    coff = boff + SSD_GROUPS
    in_specs = [
        pl.BlockSpec((nseq, seqlen, 256), lambda b, g: (b, 0, xoff + g)),
        pl.BlockSpec((nseq, seqlen, 128), lambda b, g: (b, 0, boff + g)),
        pl.BlockSpec((nseq, seqlen, 128), lambda b, g: (b, 0, coff + g)),
        pl.BlockSpec((3, 256), lambda b, g: (0, g)),
        pl.BlockSpec((3, 128), lambda b, g: (0, 16 + g)),
        pl.BlockSpec((3, 128), lambda b, g: (0, 24 + g)),
        pl.BlockSpec((1, 256), lambda b, g: (0, g)),
        pl.BlockSpec((1, 128), lambda b, g: (0, 16 + g)),
        pl.BlockSpec((1, 128), lambda b, g: (0, 24 + g)),
        pl.BlockSpec((nseq, None, 8, seqlen), lambda b, g: (b, g, 0, 0)),
        pl.BlockSpec((None, 8, 2), lambda b, g: (g, 0, 0)),
        pl.BlockSpec((None, 1, 256), lambda b, g: (g, 0, 0)),
    ]
    args = [proj3, proj3, proj3, conv_w, conv_w, conv_w, conv_b, conv_b, conv_b, dtt, prm, dskip]
    state_spec = pl.BlockSpec((nseq, 4, SSD_HEADDIM, SSD_STATE), lambda b, g: (b, g, 0, 0))
    if has_init:
        in_specs += [state_spec, state_spec]
        args += [s0[0], s0[1]]
    out_specs = [pl.BlockSpec((nseq, seqlen, 256), lambda b, g: (b, 0, g))]
    out_shape = [jax.ShapeDtypeStruct((bsz, seqlen, SSD_D_INNER), BF16)]
    if want_final:
        out_specs += [state_spec, state_spec]
        out_shape += [jax.ShapeDtypeStruct((bsz, SSD_HEADS, SSD_HEADDIM, SSD_STATE), F32)] * 2
    return pl.pallas_call(
        functools.partial(_ssd_scan_kernel, seqlen=seqlen, nseq=nseq, has_init=has_init, want_final=want_final),
        grid=(bsz // nseq, SSD_GROUPS),
        in_specs=in_specs,
        out_specs=out_specs,
        out_shape=out_shape,
        scratch_shapes=[pltpu.VMEM((nseq, seqlen, SSD_STATE), BF16),
                        pltpu.VMEM((nseq, 2, seqlen, 256), F32),
                        pltpu.VMEM((nseq, 2, seqlen, 256), F32),
                        pltpu.VMEM((nseq, 2, SSD_STATE, 256), F32),
                        pltpu.VMEM((nseq, seqlen, 256), F32)],
        compiler_params=_params("arbitrary", "arbitrary"),
        name=name,
    )(*args)


def _unit_tri_solve(a_list, rhs_list, merge_masks):
    xs = a_list
    for m in merge_masks:
        xs_b = [x.astype(BF16) for x in xs]
        xs = [x - jnp.dot(x_b, x_b * m, preferred_element_type=F32) for x, x_b in zip(xs, xs_b)]
    return [rhs - jnp.dot(x.astype(BF16), rhs.astype(BF16), preferred_element_type=F32)
            for x, rhs in zip(xs, rhs_list)]


def _gdn_scan_kernel(*refs, seqlen, nseq, has_init, want_final):
    refs = list(refs)
    q_ref, k_ref, v_ref, cwq, cwk, cwv, smt_ref, prm_ref = refs[:8]
    pos = 8
    if has_init:
        s0_refs = refs[pos:pos + 2]
        pos += 2
    oout_ref = refs[pos]
    pos += 1
    if want_final:
        sfin_refs = refs[pos:pos + 2]
        pos += 2
    u_s, wq_s, qkm_s, kdt_s, gl_s, st, o_ref = refs[pos:pos + 7]

    blk = GDN_BLOCK
    nb = seqlen // blk
    qk_scale = GDN_HEAD ** -0.5

    for g in range(nseq):
        for d in range(2):
            for r in range(2):
                if has_init:
                    st[g, d, r] = s0_refs[d][g, r]
                else:
                    st[g, d, r] = jnp.zeros((GDN_HEAD, GDN_HEAD), F32)

    row = lax.broadcasted_iota(jnp.int32, (blk, blk), 0)
    col = lax.broadcasted_iota(jnp.int32, (blk, blk), 1)
    merge_masks = [jnp.where(((row >> (l + 1)) == (col >> (l + 1))) & ((row >> l) != (col >> l)), 1.0, 0.0
                             ).astype(BF16) for l in range(7)]
    bias = prm_ref[:, 0:1]
    a_neg = -jnp.exp(prm_ref[:, 1:2])
    nt = (((1,), (1,)), ((), ()))

    cum_ms = ((row <= col).astype(F32), (row >= col).astype(F32))
    stricts = (row > col, row < col)
    incls = (row >= col, row <= col)
    lasts = (blk - 1, 0)

    def load_block(g, c):
        r0 = pl.multiple_of(c * blk, blk)
        qv = _conv_silu_rows(q_ref.at[g], r0, blk, seqlen, cwq, None)
        kv = _conv_silu_rows(k_ref.at[g], r0, blk, seqlen, cwk, None)
        qc = qv * (lax.rsqrt(jnp.sum(qv * qv, axis=-1, keepdims=True) + RMS_EPS) * qk_scale)
        kc = kv * lax.rsqrt(jnp.sum(kv * kv, axis=-1, keepdims=True) + RMS_EPS)
        vc = _conv_silu_rows(v_ref.at[g], r0, blk, seqlen, cwv, None)
        o_ref[g, pl.ds(r0, blk), :] = jnp.zeros((blk, 256), F32)
        sm = smt_ref[g, :, pl.ds(r0, blk)]
        return dict(g=g, r0=r0, wq0=pl.multiple_of(c * 2 * blk, 2 * blk), g0=pl.multiple_of(c * 8, 8),
                    qc=qc, kc=kc, vc=vc, g_t=a_neg * _softplus(sm + bias), beta_t=_sigmoid(sm))

    def prepare(blocks):
        for b in blocks:
            kc_b = b["kc"].astype(BF16)
            b["kk"] = lax.dot_general(kc_b, kc_b, nt, preferred_element_type=F32)
            b["qk"] = lax.dot_general(b["qc"].astype(BF16), kc_b, nt, preferred_element_type=F32)
        units = [(b, d) for b in blocks for d in range(2)]
        gcs = [jnp.dot(b["g_t"], cum_ms[d], precision=HI, preferred_element_type=F32) for b, d in units]
        tots = [gc[:, lasts[d]:lasts[d] + 1] for (b, d), gc in zip(units, gcs)]
        colss = [_transpose_rows([gc, jnp.exp(gc), b["beta_t"]]) for (b, d), gc in zip(units, gcs)]
        for (b, d), tot in zip(units, tots):
            gl_s[b["g"], d, pl.ds(b["g0"], 8), :] = jnp.broadcast_to(jnp.exp(tot), (8, blk))
        for b in blocks:
            b["kc_t"] = b["kc"].T
        chains = []
        for (b, d), gc, tot, cols in zip(units, gcs, tots, colss):
            kd_t = jnp.exp(tot - gc)
            for r in range(2):
                k = 2 * d + r
                e = jnp.exp(cols[:, k:k + 1] - gc[k:k + 1, :])
                beta = jnp.broadcast_to(cols[:, 20 + k:21 + k], (blk, GDN_HEAD))
                eg = jnp.broadcast_to(cols[:, 8 + k:9 + k], (blk, GDN_HEAD))
                vr = b["vc"][:, r * 128:(r + 1) * 128]
                g = b["g"]
                wq_s[g, d, r, pl.ds(b["wq0"] + blk, blk), :] = (b["qc"] * eg).astype(BF16)
                qkm_s[g, d, r, pl.ds(b["r0"], blk), :] = (b["qk"] * jnp.where(incls[d], e, 0.0)).astype(BF16)
                kdt_s[g, d, r, pl.ds(b["r0"], blk), :] = (b["kc_t"] * kd_t[k:k + 1, :]).astype(BF16)
                chains.append(dict(
                    g=g, d=d, r=r, r0=b["r0"], wq0=b["wq0"],
                    a=beta * b["kk"] * jnp.where(stricts[d], e, 0.0),
                    rhs=jnp.concatenate([vr * beta, b["kc"] * (beta * eg)], axis=1)))
        return chains

    chains_per_block = 4
    per_iter = max(1, GDN_CHAINS // (chains_per_block * nseq))
    while nb % per_iter:
        per_iter //= 2

    def phase1(i, carry):
        chains = prepare([load_block(g, i * per_iter + j) for g in range(nseq) for j in range(per_iter)])
        sols = _unit_tri_solve([ch["a"] for ch in chains], [ch["rhs"] for ch in chains], merge_masks)
        for sol, ch in zip(sols, chains):
            u_s[ch["g"], ch["d"], ch["r"], pl.ds(ch["r0"], blk), :] = sol[:, :GDN_HEAD]
            wq_s[ch["g"], ch["d"], ch["r"], pl.ds(ch["wq0"], blk), :] = sol[:, GDN_HEAD:].astype(BF16)
        return carry

    lax.fori_loop(0, nb // per_iter, phase1, 0)

    def phase2(i, carry):
        jobs = []
        for g in range(nseq):
            for d, c in ((0, i), (1, nb - 1 - i)):
                r0 = pl.multiple_of(c * blk, blk)
                wq0 = pl.multiple_of(c * 2 * blk, 2 * blk)
                gl = gl_s[g, d, pl.ds(pl.multiple_of(c * 8, 8), 8), :]
                for r in range(2):
                    jobs.append(((g, d, r), r0, wq0, gl[2 * d + r:2 * d + r + 1, :]))
        ss = [st[idx] for idx, _, _, _ in jobs]
        wss = [jnp.dot(wq_s[idx + (pl.ds(wq0, 2 * blk), slice(None))], s.astype(BF16), preferred_element_type=F32)
               for (idx, _, wq0, _), s in zip(jobs, ss)]
        vns = [(u_s[idx + (pl.ds(r0, blk), slice(None))] - ws[:blk]).astype(BF16)
               for (idx, r0, _, _), ws in zip(jobs, wss)]
        for (idx, r0, _, gl), s, vn in zip(jobs, ss, vns):
            st[idx] = gl * s + jnp.dot(kdt_s[idx + (pl.ds(r0, blk), slice(None))], vn, preferred_element_type=F32)
        outs = [ws[blk:] + jnp.dot(qkm_s[idx + (pl.ds(r0, blk), slice(None))], vn, preferred_element_type=F32)
                for (idx, r0, _, _), ws, vn in zip(jobs, wss, vns)]
        for pair in range(2 * nseq):
            (g, _, _), r0 = jobs[2 * pair][0], jobs[2 * pair][1]
            o_ref[g, pl.ds(r0, blk), :] += jnp.concatenate(outs[2 * pair:2 * pair + 2], axis=1)
        return carry

    lax.fori_loop(0, nb, phase2, 0)

    def emit(c, carry):
        r0 = pl.multiple_of(c * blk, blk)
        for g in range(nseq):
            oout_ref[g, pl.ds(r0, blk), :] = o_ref[g, pl.ds(r0, blk), :].astype(oout_ref.dtype)
        return carry

    lax.fori_loop(0, nb, emit, 0)

    if want_final:
        for g in range(nseq):
            for d in range(2):
                for r in range(2):
                    sfin_refs[d][g, r] = st[g, d, r]


def _gdn_scan_call(proj3, smt, conv_w, prm, s0, want_final, name):
    bsz, seqlen, _ = proj3.shape
    has_init = s0 is not None
    nseq = max(2, min(bsz, GDN_CHAINS * GDN_BLOCK // (4 * seqlen)))
    while bsz % nseq:
        nseq -= 1
    in_specs = [
        pl.BlockSpec((nseq, seqlen, 128), lambda b, g: (b, 0, g)),
        pl.BlockSpec((nseq, seqlen, 128), lambda b, g: (b, 0, GDN_K_HEADS + g)),
        pl.BlockSpec((nseq, seqlen, 256), lambda b, g: (b, 0, GDN_K_HEADS + g)),
        pl.BlockSpec((3, 128), lambda b, g: (0, g)),
        pl.BlockSpec((3, 128), lambda b, g: (0, GDN_K_HEADS + g)),
        pl.BlockSpec((3, 256), lambda b, g: (0, GDN_K_HEADS + g)),
        pl.BlockSpec((nseq, None, 8, seqlen), lambda b, g: (b, g, 0, 0)),
        pl.BlockSpec((None, 8, 2), lambda b, g: (g, 0, 0)),
    ]
    args = [proj3, proj3, proj3, conv_w, conv_w, conv_w, smt, prm]
    state_spec = pl.BlockSpec((nseq, 2, GDN_HEAD, GDN_HEAD), lambda b, g: (b, g, 0, 0))
    if has_init:
        in_specs += [state_spec, state_spec]
        args += [s0[0], s0[1]]
    out_specs = [pl.BlockSpec((nseq, seqlen, 256), lambda b, g: (b, 0, g))]
    out_shape = [jax.ShapeDtypeStruct((bsz, seqlen, GDN_V), BF16)]
    if want_final:
        out_specs += [state_spec, state_spec]
        out_shape += [jax.ShapeDtypeStruct((bsz, GDN_V_HEADS, GDN_HEAD, GDN_HEAD), F32)] * 2
    return pl.pallas_call(
        functools.partial(_gdn_scan_kernel, seqlen=seqlen, nseq=nseq, has_init=has_init, want_final=want_final),
        grid=(bsz // nseq, GDN_K_HEADS),
        in_specs=in_specs,
        out_specs=out_specs,
        out_shape=out_shape,
        scratch_shapes=[pltpu.VMEM((nseq, 2, 2, seqlen, GDN_HEAD), F32),
                        pltpu.VMEM((nseq, 2, 2, 2 * seqlen, GDN_HEAD), BF16),
                        pltpu.VMEM((nseq, 2, 2, seqlen, GDN_BLOCK), BF16),
                        pltpu.VMEM((nseq, 2, 2, seqlen, GDN_BLOCK), BF16),
                        pltpu.VMEM((nseq, 2, 8 * (seqlen // GDN_BLOCK), GDN_BLOCK), F32),
                        pltpu.VMEM((nseq, 2, 2, GDN_HEAD, GDN_HEAD), F32),
                        pltpu.VMEM((nseq, seqlen, 256), F32)],
        compiler_params=_params("arbitrary", "arbitrary"),
        name=name,
    )(*args)


def _residual_layer_norm(x, gate, o, ln_g, ln_b):
    r = DEEPNORM_ALPHA * x + gate * o
    mu = jnp.mean(r, axis=-1, keepdims=True)
    rc = r - mu
    var = jnp.mean(rc * rc, axis=-1, keepdims=True)
    return rc * lax.rsqrt(var + LN_EPS) * ln_g + ln_b


def _out_kernel(y_ref, z_ref, nw_ref, w_ref, x_ref, gt_ref, lg_ref, lb_ref, o_ref, *, mode):
    if mode == "ssd":
        u = y_ref[...].astype(F32) * _silu(z_ref[...].astype(F32))
        ms = jnp.mean(u * u, axis=-1, keepdims=True)
        a = (u * lax.rsqrt(ms + RMS_EPS) * nw_ref[...]).astype(BF16)
    else:
        parts = []
        for h in range(GDN_V_HEADS):
            sl = slice(h * GDN_HEAD, (h + 1) * GDN_HEAD)
            oh = y_ref[:, sl].astype(F32)
            ms = jnp.mean(oh * oh, axis=-1, keepdims=True)
            parts.append((oh * lax.rsqrt(ms + RMS_EPS) * nw_ref[...] * _silu(z_ref[:, sl].astype(F32))).astype(BF16))
        a = jnp.concatenate(parts, axis=1)
    o = jnp.dot(a, w_ref[...], preferred_element_type=F32)
    o_ref[...] = _residual_layer_norm(x_ref[...], gt_ref[...], o, lg_ref[...], lb_ref[...])


def _out_call(mode, act, z_src, z_blk, norm_w, w, x, gate, ln_g, ln_b, rows_per_mod, tm, name):
    m, kdim = act.shape
    row_spec = pl.BlockSpec((tm, D_MODEL), lambda i: (i, 0))
    vec_spec = pl.BlockSpec((1, D_MODEL), lambda i: (0, 0))
    in_specs = [pl.BlockSpec((tm, kdim), lambda i: (i, 0)),
                pl.BlockSpec((tm, kdim), lambda i: (i, z_blk)),
                pl.BlockSpec((1, norm_w.shape[1]), lambda i: (0, 0)),
                pl.BlockSpec((kdim, D_MODEL), lambda i: (0, 0)), row_spec,
                pl.BlockSpec((None, 1, D_MODEL), lambda i: ((i * tm) // rows_per_mod, 0, 0)),
                vec_spec, vec_spec]
    args = [act, z_src, norm_w, w, x, gate, ln_g, ln_b]
    return pl.pallas_call(
        functools.partial(_out_kernel, mode=mode),
        grid=(m // tm,),
        in_specs=in_specs,
        out_specs=row_spec,
        out_shape=jax.ShapeDtypeStruct((m, D_MODEL), F32),
        compiler_params=_params("arbitrary"),
        name=name,
    )(*args)


def _ffn_down_kernel(*refs, tm, period, row_taps, tiles_per_seq):
    if row_taps:
        (a_ref, ap_ref, an_ref, v_ref, cw_ref, cb_ref, w_ref, x_ref, gt_ref, lg_ref, lb_ref,
         o_ref, acc_ref, ext_ref) = refs
        t = pl.program_id(0) % tiles_per_seq
        has_prev = (t > 0).astype(F32)
        has_next = (t < tiles_per_seq - 1).astype(F32)
        base, taps, n = HALO, (0, 1, 2), tm + 2 * HALO
    else:
        a_ref, v_ref, cw_ref, cb_ref, w_ref, x_ref, gt_ref, lg_ref, lb_ref, o_ref, acc_ref = refs
        base, taps, n = 0, (1,), tm
    colpos = lax.broadcasted_iota(jnp.int32, (n, FFN_TC), 0) % period
    is_first = colpos == 0
    is_last = colpos == period - 1
    n_ct = D_FF // FFN_TC

    def gated_tile(c):
        cols = pl.ds(pl.multiple_of(c * FFN_TC, FFN_TC), FFN_TC)
        if row_taps:
            ext_ref[0:HALO, :] = ap_ref[:, cols].astype(F32) * has_prev
            ext_ref[HALO:HALO + tm, :] = a_ref[:, cols].astype(F32)
            ext_ref[HALO + tm:, :] = an_ref[:, cols].astype(F32) * has_next
            ext = ext_ref[...]
        else:
            ext = a_ref[:, cols].astype(F32)
        em1 = jnp.where(is_first, 0.0, pltpu.roll(ext, 1, 0))
        ep1 = jnp.where(is_last, 0.0, pltpu.roll(ext, n - 1, 0))
        conv = cb_ref[:, cols]
        for dr in taps:
            off = base + (dr - 1) * GRID_W
            conv = (conv + cw_ref[3 * dr:3 * dr + 1, cols] * em1[off:off + tm]
                    + cw_ref[3 * dr + 1:3 * dr + 2, cols] * ext[off:off + tm]
                    + cw_ref[3 * dr + 2:3 * dr + 3, cols] * ep1[off:off + tm])
        return (_silu(conv) * v_ref[:, cols].astype(F32)).astype(BF16)

    acc_ref[...] = jnp.zeros((tm, D_MODEL), F32)

    def channel_tile(c, carry):
        rows = pl.ds(pl.multiple_of(c * FFN_TC, FFN_TC), FFN_TC)
        acc_ref[...] += jnp.dot(gated_tile(c), w_ref[rows, :], preferred_element_type=F32)
        return carry

    lax.fori_loop(0, n_ct, channel_tile, 0)
    o_ref[...] = _residual_layer_norm(x_ref[...], gt_ref[...], acc_ref[...], lg_ref[...], lb_ref[...])


def _ffn_down_call(up, conv_w9, conv_b, w_down, x, gate, ln_g, ln_b, seqlen, is_grid, rows_per_mod, tm, name):
    m = up.shape[0]
    a_spec = pl.BlockSpec((tm, D_FF), lambda i: (i, 0))
    v_spec = pl.BlockSpec((tm, D_FF), lambda i: (i, 1))
    scratch = [pltpu.VMEM((tm, D_MODEL), F32)]
    if is_grid:
        scratch.append(pltpu.VMEM((tm + 2 * HALO, FFN_TC), F32))
        hb = tm // HALO
        nhb = m // HALO
        tiles_per_seq = seqlen // tm
        in_specs = [a_spec,
                    pl.BlockSpec((HALO, D_FF), lambda i: (jnp.maximum(i * hb - 1, 0), 0)),
                    pl.BlockSpec((HALO, D_FF), lambda i: (jnp.minimum((i + 1) * hb, nhb - 1), 0)),
                    v_spec]
        args = [up, up, up, up]
        period = GRID_W
    else:
        tiles_per_seq = 1
        in_specs = [a_spec, v_spec]
        args = [up, up]
        period = seqlen
    row_spec = pl.BlockSpec((tm, D_MODEL), lambda i: (i, 0))
    vec_spec = pl.BlockSpec((1, D_MODEL), lambda i: (0, 0))
    in_specs += [pl.BlockSpec((9, D_FF), lambda i: (0, 0)), pl.BlockSpec((1, D_FF), lambda i: (0, 0)),
                 pl.BlockSpec((D_FF, D_MODEL), lambda i: (0, 0)), row_spec,
                 pl.BlockSpec((None, 1, D_MODEL), lambda i: ((i * tm) // rows_per_mod, 0, 0)),
                 vec_spec, vec_spec]
    args += [conv_w9, conv_b, w_down, x, gate, ln_g, ln_b]
    return pl.pallas_call(
        functools.partial(_ffn_down_kernel, tm=tm, period=period, row_taps=is_grid, tiles_per_seq=tiles_per_seq),
        grid=(m // tm,),
        in_specs=in_specs,
        out_specs=row_spec,
        out_shape=jax.ShapeDtypeStruct((m, D_MODEL), F32),
        scratch_shapes=scratch,
        compiler_params=_params("arbitrary"),
        name=name,
    )(*args)


def _group_rows(t, bsz, seqlen, lead):
    n_lead = int(math.prod(lead))
    t = t.reshape(bsz, seqlen, n_lead, 8, -1)
    r = t.shape[-1]
    t = jnp.transpose(t, (0, 3, 2, 4, 1))
    return t.reshape(bsz, 8, n_lead * r, seqlen)


def _group_params(p):
    r = p.shape[1] // 8
    return jnp.transpose(p.reshape(2, 8, r), (1, 0, 2)).reshape(8, 2 * r)


def _run_tokens(x, seqlen, mods, states, want_final, w, tag):
    bsz = x.shape[0]
    m = bsz * seqlen
    nb = mods.shape[1]
    rows_per_mod = m // nb
    xt = x.reshape(m, D_MODEL)
    finals = []
    is_grid = tag == "s"
    for i in range(DEPTH):
        def mod(k):
            return mods[i, :, k].reshape(nb, 1, D_MODEL)
        lw = w[i]
        proj, small = _proj_call(xt, mod(1), mod(0), lw["w_in"], lw["w_in_small"], rows_per_mod,
                                 tm=1024, tn=1536, name=f"inproj{i}{tag}")
        proj3 = proj.reshape(bsz, seqlen, 6144)
        if i % 2 == 0:
            dtt = _group_rows(small[:, :2 * SSD_HEADS], bsz, seqlen, (2,))
            res = _ssd_scan_call(proj3, dtt, lw["conv_w"], lw["conv_b"], lw["prm"], lw["dskip"],
                                 states[i], want_final, name=f"ssdscan{tag}")
            mode, z_blk = "ssd", 0
        else:
            smt = _group_rows(small[:, :4 * GDN_V_HEADS], bsz, seqlen, (2, 2))
            res = _gdn_scan_call(proj3, smt, lw["conv_w"], lw["prm"], states[i], want_final,
                                 name=f"gdnscan{tag}")
            mode, z_blk = "gdn", 2
        y = res[0].reshape(m, 2048)
        if want_final:
            finals.append((res[1], res[2]))
        xt = _out_call(mode, y, proj, z_blk, lw["norm_w"], lw["w_out"], xt, mod(2), lw["ln_g0"], lw["ln_b0"],
                       rows_per_mod, tm=512, name=f"outproj{i}{tag}")
        up = _proj_call(xt, mod(4), mod(3), lw["w_up"], None, rows_per_mod,
                        tm=1024, tn=1408, name=f"ffnup{i}{tag}")[0]
        xt = _ffn_down_call(up, lw["ffn_conv_w"], lw["ffn_conv_b"], lw["w_down"], xt, mod(5),
                            lw["ln_g1"], lw["ln_b1"], seqlen, is_grid, rows_per_mod, tm=512,
                            name=f"ffndown{i}{tag}")
    return xt.reshape(bsz, seqlen, D_MODEL), finals


def kernel(x_prompt, x_sample, state_ssd_fwd, state_ssd_bwd, state_gdn_fwd, state_gdn_bwd, c, c_ctx, w_ada, b_ada, ln_g, ln_b, ssd_w_in, ssd_conv_w, ssd_conv_b, ssd_dt_bias, ssd_a_log, ssd_d, ssd_norm_w, ssd_w_out, gdn_w_in, gdn_conv_w, gdn_dt_bias, gdn_a_log, gdn_norm_w, gdn_w_out, ffn_w_up, ffn_conv_w, ffn_conv_b, ffn_w_down):
    n_dec = c.shape[0]
    cond = jnp.concatenate([c_ctx[None, :], c, jnp.zeros((16 - 1 - n_dec, D_MODEL), F32)], axis=0)
    ada = _ada_call(cond, w_ada, b_ada).reshape(DEPTH, 16, 6, D_MODEL)
    mods_p = ada[:, 0:1]
    mods_s = ada[:, 1:1 + n_dec]

    def small_cols(wm):
        return jnp.pad(wm, ((0, 0), (0, 128 - wm.shape[1]))).astype(BF16)

    weights = []
    for i in range(DEPTH):
        j = i // 2
        lw = {
            "ln_g0": ln_g[i, 0:1], "ln_b0": ln_b[i, 0:1], "ln_g1": ln_g[i, 1:2], "ln_b1": ln_b[i, 1:2],
            "w_up": ffn_w_up[i].astype(BF16), "w_down": ffn_w_down[i].astype(BF16),
            "ffn_conv_w": ffn_conv_w[i].reshape(9, D_FF), "ffn_conv_b": ffn_conv_b[i].reshape(1, D_FF),
        }
        if i % 2 == 0:
            lw["w_in"] = ssd_w_in[j].astype(BF16)
            lw["w_in_small"] = small_cols(lw["w_in"][:, 6144:])
            lw["conv_w"] = ssd_conv_w[j]
            lw["conv_b"] = ssd_conv_b[j].reshape(1, SSD_XBC)
            lw["prm"] = jnp.stack([_group_params(ssd_dt_bias[j]), _group_params(ssd_a_log[j])], axis=-1)
            lw["dskip"] = jnp.repeat(ssd_d[j], SSD_HEADDIM).reshape(SSD_GROUPS, 1, 256)
            lw["norm_w"] = ssd_norm_w[j].reshape(1, SSD_D_INNER)
            lw["w_out"] = ssd_w_out[j].astype(BF16)
        else:
            lw["w_in"] = gdn_w_in[j].astype(BF16)
            lw["w_in_small"] = small_cols(lw["w_in"][:, 6144:])
            lw["conv_w"] = gdn_conv_w[j]
            prm = jnp.stack([_group_params(gdn_dt_bias[j]), _group_params(gdn_a_log[j])], axis=-1)
            lw["prm"] = jnp.pad(prm, ((0, 0), (0, 4), (0, 0)))
            lw["norm_w"] = gdn_norm_w[j].reshape(1, GDN_HEAD)
            lw["w_out"] = gdn_w_out[j].astype(BF16)
        weights.append(lw)

    states_s = [(state_ssd_fwd[:, 0], state_ssd_bwd[:, 0]), (state_gdn_fwd[:, 0], state_gdn_bwd[:, 0])]
    y_prompt, fin = _run_tokens(x_prompt, x_prompt.shape[1], mods_p, [None, None], True, weights, "p")
    y_sample, _ = _run_tokens(x_sample, x_sample.shape[1], mods_s, states_s, False, weights, "s")
    return (y_prompt, y_sample,
            fin[0][0][:, None], fin[0][1][:, None], fin[1][0][:, None], fin[1][1][:, None])
```

```python
import functools
import math

import jax
import jax.numpy as jnp
from jax import lax
from jax.experimental import pallas as pl
from jax.experimental.pallas import tpu as pltpu

F32 = jnp.float32
BF16 = jnp.bfloat16
HI = lax.Precision.HIGHEST

D_MODEL = 1024
DEPTH = 2
GRID_W = 64

SSD_D_INNER = 2048
SSD_HEADDIM = 64
SSD_HEADS = 32
SSD_GROUPS = 8
SSD_STATE = 128
SSD_CHUNK = 128
SSD_BLOCKS = 4
SSD_XBC = 4096

GDN_K_HEADS = 8
GDN_V_HEADS = 16
GDN_HEAD = 128
GDN_QK = 1024
GDN_V = 2048
GDN_QKV = 4096
GDN_BLOCK = 128
GDN_CHAINS = 16

D_FF = 2816
FFN_TC = 256
HALO = 128

DEEPNORM_ALPHA = (2.0 * DEPTH) ** 0.25
LN_EPS = 1e-5
RMS_EPS = 1e-6

VMEM_LIMIT = 56 * 1024 * 1024
BF16_ROWS = 16


def _silu(x):
    h = 0.5 * x
    return h + h * jnp.tanh(h)


def _sigmoid(x):
    return 0.5 + 0.5 * jnp.tanh(0.5 * x)


def _softplus(x):
    return jnp.maximum(x, 0.0) + jnp.log(1.0 + jnp.exp(-jnp.abs(x)))


def _params(*sem):
    return pltpu.CompilerParams(dimension_semantics=sem, vmem_limit_bytes=VMEM_LIMIT)


def _ada_kernel(c_ref, w_ref, b_ref, o_ref):
    h = _silu(c_ref[...]).astype(BF16)
    o_ref[...] = jnp.dot(h, w_ref[...].astype(BF16), preferred_element_type=F32) + b_ref[...]


def _ada_call(cond, w_ada, b_ada):
    rows = cond.shape[0]
    tn = 1536
    return pl.pallas_call(
        _ada_kernel,
        grid=(DEPTH, 6 * D_MODEL // tn),
        in_specs=[
            pl.BlockSpec((rows, D_MODEL), lambda l, j: (0, 0)),
            pl.BlockSpec((None, D_MODEL, tn), lambda l, j: (l, 0, j)),
            pl.BlockSpec((None, 1, tn), lambda l, j: (l, 0, j)),
        ],
        out_specs=pl.BlockSpec((None, rows, tn), lambda l, j: (l, 0, j)),
        out_shape=jax.ShapeDtypeStruct((DEPTH, rows, 6 * D_MODEL), F32),
        compiler_params=_params("arbitrary", "arbitrary"),
        name="adaln",
    )(cond, w_ada, b_ada.reshape(DEPTH, 1, 6 * D_MODEL))


def _proj_kernel(*refs, has_small):
    if has_small:
        x_ref, sc_ref, sh_ref, w_ref, w2_ref, o_ref, o2_ref, h_ref = refs
    else:
        x_ref, sc_ref, sh_ref, w_ref, o_ref, h_ref = refs

    @pl.when(pl.program_id(1) == 0)
    def _():
        h = x_ref[...] * (1.0 + sc_ref[...]) + sh_ref[...]
        h_ref[...] = h.astype(BF16)
        if has_small:
            o2_ref[...] = jnp.dot(h_ref[...], w2_ref[...], preferred_element_type=F32)

    o_ref[...] = jnp.dot(h_ref[...], w_ref[...], preferred_element_type=F32).astype(o_ref.dtype)


def _proj_call(x, scale, shift, w, w_small, rows_per_mod, tm, tn, name):
    m = x.shape[0]
    w3, layer = w
    n = (w3.shape[2] // tn) * tn
    has_small = w_small is not None
    mod_spec = pl.BlockSpec((None, 1, D_MODEL), lambda i, j: ((i * tm) // rows_per_mod, 0, 0))
    in_specs = [pl.BlockSpec((tm, D_MODEL), lambda i, j: (i, 0)), mod_spec, mod_spec,
                pl.BlockSpec((None, D_MODEL, tn), lambda i, j: (layer, 0, j))]
    out_specs = [pl.BlockSpec((tm, tn), lambda i, j: (i, j))]
    out_shape = [jax.ShapeDtypeStruct((m, n), BF16)]
    args = [x, scale, shift, w3]
    if has_small:
        in_specs.append(pl.BlockSpec((D_MODEL, 128), lambda i, j: (0, 0)))
        out_specs.append(pl.BlockSpec((tm, 128), lambda i, j: (i, 0)))
        out_shape.append(jax.ShapeDtypeStruct((m, 128), F32))
        args.append(w_small)
    return pl.pallas_call(
        functools.partial(_proj_kernel, has_small=has_small),
        grid=(m // tm, n // tn),
        in_specs=in_specs,
        out_specs=out_specs,
        out_shape=out_shape,
        scratch_shapes=[pltpu.VMEM((tm, D_MODEL), BF16)],
        compiler_params=_params("arbitrary", "arbitrary"),
        name=name,
    )(*args)


def _conv_silu_rows(src_ref, r0, n_rows, seqlen, w_ref, bias):
    cur = src_ref[pl.ds(r0, n_rows), :].astype(F32)
    has_prev = (r0 > 0).astype(F32)
    has_next = (r0 + n_rows < seqlen).astype(F32)
    g_prev = pl.multiple_of(jnp.maximum(r0 - BF16_ROWS, 0), BF16_ROWS)
    g_next = pl.multiple_of(jnp.minimum(r0 + n_rows, seqlen - BF16_ROWS), BF16_ROWS)
    prev_row = src_ref[pl.ds(g_prev, BF16_ROWS), :].astype(F32)[BF16_ROWS - 1:BF16_ROWS] * has_prev
    next_row = src_ref[pl.ds(g_next, BF16_ROWS), :].astype(F32)[0:1] * has_next
    rid = lax.broadcasted_iota(jnp.int32, cur.shape, 0)
    xm1 = jnp.where(rid == 0, prev_row, pltpu.roll(cur, 1, 0))
    xp1 = jnp.where(rid == n_rows - 1, next_row, pltpu.roll(cur, n_rows - 1, 0))
    out = w_ref[0:1, :] * xm1 + w_ref[1:2, :] * cur + w_ref[2:3, :] * xp1
    if bias is not None:
        out = out + bias
    return _silu(out)


def _transpose_rows(rows_list):
    pad = 128 - 8 * len(rows_list)
    stack = jnp.concatenate(rows_list + [jnp.zeros((pad, 128), F32)], axis=0)
    return stack.T


def _ssd_scan_kernel(*refs, seqlen, nseq, has_init, want_final):
    refs = list(refs)
    x_ref, b_ref, c_ref, cwx, cwb, cwc, cbx, cbb, cbc, dtt_ref, prm_ref, dsk_ref = refs[:12]
    pos = 12
    if has_init:
        s0_refs = refs[pos:pos + 2]
        pos += 2
    yout_ref = refs[pos]
    pos += 1
    if want_final:
        sfin_refs = refs[pos:pos + 2]
        pos += 2
    cs_b, ea_s, ns_s, st, y_ref = refs[pos:pos + 5]

    nc = seqlen // SSD_CHUNK
    q = SSD_CHUNK

    for g in range(nseq):
        for d in range(2):
            if has_init:
                halves = []
                for h in range(2):
                    blk = jnp.concatenate([s0_refs[d][g, 2 * h], s0_refs[d][g, 2 * h + 1]], axis=0)
                    halves.append(blk.T)
                st[g, d] = jnp.concatenate(halves, axis=1)
            else:
                st[g, d] = jnp.zeros((SSD_STATE, 256), F32)

    row = lax.broadcasted_iota(jnp.int32, (q, q), 0)
    col = lax.broadcasted_iota(jnp.int32, (q, q), 1)
    lane_head = lax.broadcasted_iota(jnp.int32, (q, 256), 1) // SSD_HEADDIM
    head_masks = [jnp.where(lane_head == r, 1.0, 0.0).astype(BF16) for r in range(4)]
    bias = prm_ref[:, 0:1]
    a_neg = -jnp.exp(prm_ref[:, 1:2])
    dskip = dsk_ref[...]

    cum_ms = ((row <= col).astype(F32), (row >= col).astype(F32))
    keeps = (row >= col, row <= col)
    lasts = (q - 1, 0)

    def intra(i, carry):
        blocks = []
        for g, j in ((g, j) for g in range(nseq) for j in range(unroll_a)):
            r0 = pl.multiple_of((i * unroll_a + j) * q, q)
            xc = _conv_silu_rows(x_ref.at[g], r0, q, seqlen, cwx, cbx[...])
            bc = _conv_silu_rows(b_ref.at[g], r0, q, seqlen, cwb, cbb[...])
            cc = _conv_silu_rows(c_ref.at[g], r0, q, seqlen, cwc, cbc[...])
            cc_b = cc.astype(BF16)
            cs_b[g, pl.ds(r0, q), :] = cc_b
            bc_t = bc.T
            dt_t = _softplus(dtt_ref[g, :, pl.ds(r0, q)] + bias)
            xc_b = xc.astype(BF16)
            x_heads = jnp.concatenate([xc_b * head_masks[r] for r in range(4)], axis=0)
            blocks.append(dict(g=g, r0=r0, xc=xc, bc_t=bc_t, dt_t=dt_t, a_t=dt_t * a_neg, x_heads=x_heads,
                               scores=jnp.dot(cc_b, bc_t.astype(BF16), preferred_element_type=F32)))
        units = [(blk, d) for blk in blocks for d in range(2)]
        acums = [jnp.dot(blk["a_t"], cum_ms[d], precision=HI, preferred_element_type=F32) for blk, d in units]
        cols = _transpose_rows(acums)
        dtdes = [blk["dt_t"] * jnp.exp(ac[:, lasts[d]:lasts[d] + 1] - ac) for (blk, d), ac in zip(units, acums)]
        for (blk, d), ac in zip(units, acums):
            ea_t = jnp.exp(ac)
            ea_rows = jnp.concatenate([jnp.broadcast_to(ea_t[4 * d + r:4 * d + r + 1, :], (SSD_HEADDIM, q))
                                       for r in range(4)], axis=0)
            ea_s[blk["g"], d, pl.ds(blk["r0"], q), :] = jnp.concatenate([ea_rows[0:128].T, ea_rows[128:256].T], axis=1)
        lhss = []
        for u, ((blk, d), ac, dtde) in enumerate(zip(units, acums, dtdes)):
            parts = []
            for r in range(4):
                k = 4 * d + r
                decay = jnp.where(keeps[d], jnp.exp(cols[:, 8 * u + k:8 * u + k + 1] - ac[k:k + 1, :]), 0.0)
                m = decay * blk["scores"] * blk["dt_t"][k:k + 1, :]
                bt = blk["bc_t"] * dtde[k:k + 1, :]
                parts.append(jnp.concatenate([m, bt], axis=0).astype(BF16))
            lhss.append(jnp.concatenate(parts, axis=1))
        ress = [jnp.dot(lhs, blk["x_heads"], preferred_element_type=F32) for lhs, (blk, d) in zip(lhss, units)]
        for res, (blk, d) in zip(ress, units):
            ns_s[blk["g"], d, pl.ds(blk["r0"], q), :] = res[q:]
        for j, blk in enumerate(blocks):
            y_ref[blk["g"], pl.ds(blk["r0"], q), :] = blk["xc"] * dskip + ress[2 * j][:q] + ress[2 * j + 1][:q]
        return carry

    unroll_a = max(1, SSD_BLOCKS // nseq)
    while nc % unroll_a:
        unroll_a //= 2
    lax.fori_loop(0, nc // unroll_a, intra, 0)

    def inter(i, carry):
        jobs = []
        for g, d in ((g, d) for g in range(nseq) for d in range(2)):
            s_t = st[g, d]
            for j in range(unroll_a):
                c = i * unroll_a + j
                r0 = pl.multiple_of((c if d == 0 else nc - 1 - c) * q, q)
                jobs.append((g, d, r0, s_t.astype(BF16)))
                s_t = s_t * ea_s[g, d, pl.ds(r0 + lasts[d], 1), :] + ns_s[g, d, pl.ds(r0, q), :]
            st[g, d] = s_t
        y_inters = [jnp.dot(cs_b[g, pl.ds(r0, q), :], s_b, preferred_element_type=F32) * ea_s[g, d, pl.ds(r0, q), :]
                    for g, d, r0, s_b in jobs]
        for (g, d, r0, _), y_inter in zip(jobs, y_inters):
            y_ref[g, pl.ds(r0, q), :] += y_inter
        return carry

    lax.fori_loop(0, nc // unroll_a, inter, 0)

    def emit(c, carry):
        r0 = pl.multiple_of(c * q, q)
        for g in range(nseq):
            yout_ref[g, pl.ds(r0, q), :] = y_ref[g, pl.ds(r0, q), :].astype(yout_ref.dtype)
        return carry

    lax.fori_loop(0, nc, emit, 0)

    if want_final:
        for g, d in ((g, d) for g in range(nseq) for d in range(2)):
            s_t = st[g, d]
            for h in range(2):
                blk = s_t[:, h * 128:(h + 1) * 128].T
                sfin_refs[d][g, 2 * h] = blk[0:64]
                sfin_refs[d][g, 2 * h + 1] = blk[64:128]


def _ssd_scan_call(proj3, dtt, conv_w, conv_b, prm, dskip, s0, want_final, name):
    bsz, seqlen, _ = proj3.shape
    has_init = s0 is not None
    nseq = max(1, min(bsz, SSD_BLOCKS * SSD_CHUNK // seqlen))
    while bsz % nseq:
        nseq -= 1
    xoff = SSD_D_INNER // 256
    boff = (2 * SSD_D_INNER) // 128
    coff = boff + SSD_GROUPS
    in_specs = [
        pl.BlockSpec((nseq, seqlen, 256), lambda b, g: (b, 0, xoff + g)),
        pl.BlockSpec((nseq, seqlen, 128), lambda b, g: (b, 0, boff + g)),
        pl.BlockSpec((nseq, seqlen, 128), lambda b, g: (b, 0, coff + g)),
        pl.BlockSpec((3, 256), lambda b, g: (0, g)),
        pl.BlockSpec((3, 128), lambda b, g: (0, 16 + g)),
        pl.BlockSpec((3, 128), lambda b, g: (0, 24 + g)),
        pl.BlockSpec((1, 256), lambda b, g: (0, g)),
        pl.BlockSpec((1, 128), lambda b, g: (0, 16 + g)),
        pl.BlockSpec((1, 128), lambda b, g: (0, 24 + g)),
        pl.BlockSpec((nseq, None, 8, seqlen), lambda b, g: (b, g, 0, 0)),
        pl.BlockSpec((None, 8, 2), lambda b, g: (g, 0, 0)),
        pl.BlockSpec((None, 1, 256), lambda b, g: (g, 0, 0)),
    ]
    args = [proj3, proj3, proj3, conv_w, conv_w, conv_w, conv_b, conv_b, conv_b, dtt, prm, dskip]
    state_spec = pl.BlockSpec((nseq, 4, SSD_HEADDIM, SSD_STATE), lambda b, g: (b, g, 0, 0))
    if has_init:
        in_specs += [state_spec, state_spec]
        args += [s0[0], s0[1]]
    out_specs = [pl.BlockSpec((nseq, seqlen, 256), lambda b, g: (b, 0, g))]
    out_shape = [jax.ShapeDtypeStruct((bsz, seqlen, SSD_D_INNER), BF16)]
    if want_final:
        out_specs += [state_spec, state_spec]
        out_shape += [jax.ShapeDtypeStruct((bsz, SSD_HEADS, SSD_HEADDIM, SSD_STATE), F32)] * 2
    return pl.pallas_call(
        functools.partial(_ssd_scan_kernel, seqlen=seqlen, nseq=nseq, has_init=has_init, want_final=want_final),
        grid=(bsz // nseq, SSD_GROUPS),
        in_specs=in_specs,
        out_specs=out_specs,
        out_shape=out_shape,
        scratch_shapes=[pltpu.VMEM((nseq, seqlen, SSD_STATE), BF16),
                        pltpu.VMEM((nseq, 2, seqlen, 256), F32),
                        pltpu.VMEM((nseq, 2, seqlen, 256), F32),
                        pltpu.VMEM((nseq, 2, SSD_STATE, 256), F32),
                        pltpu.VMEM((nseq, seqlen, 256), F32)],
        compiler_params=_params("arbitrary", "arbitrary"),
        name=name,
    )(*args)


def _unit_tri_solve(a_list, rhs_list, merge_masks):
    xs = a_list
    for m in merge_masks:
        xs_b = [x.astype(BF16) for x in xs]
        xs = [x - jnp.dot(x_b, x_b * m, preferred_element_type=F32) for x, x_b in zip(xs, xs_b)]
    return [rhs - jnp.dot(x.astype(BF16), rhs.astype(BF16), preferred_element_type=F32)
            for x, rhs in zip(xs, rhs_list)]


def _gdn_scan_kernel(*refs, seqlen, nseq, has_init, want_final):
    refs = list(refs)
    q_ref, k_ref, v_ref, cwq, cwk, cwv, smt_ref, prm_ref = refs[:8]
    pos = 8
    if has_init:
        s0_refs = refs[pos:pos + 2]
        pos += 2
    oout_ref = refs[pos]
    pos += 1
    if want_final:
        sfin_refs = refs[pos:pos + 2]
        pos += 2
    u_s, wq_s, qkm_s, kdt_s, gl_s, st, o_ref = refs[pos:pos + 7]

    blk = GDN_BLOCK
    nb = seqlen // blk
    qk_scale = GDN_HEAD ** -0.5

    for g in range(nseq):
        for d in range(2):
            for r in range(2):
                if has_init:
                    st[g, d, r] = s0_refs[d][g, r]
                else:
                    st[g, d, r] = jnp.zeros((GDN_HEAD, GDN_HEAD), F32)

    row = lax.broadcasted_iota(jnp.int32, (blk, blk), 0)
    col = lax.broadcasted_iota(jnp.int32, (blk, blk), 1)
    merge_masks = [jnp.where(((row >> (l + 1)) == (col >> (l + 1))) & ((row >> l) != (col >> l)), 1.0, 0.0
                             ).astype(BF16) for l in range(7)]
    bias = prm_ref[:, 0:1]
    a_neg = -jnp.exp(prm_ref[:, 1:2])
    nt = (((1,), (1,)), ((), ()))

    cum_ms = ((row <= col).astype(F32), (row >= col).astype(F32))
    stricts = (row > col, row < col)
    incls = (row >= col, row <= col)
    lasts = (blk - 1, 0)

    def load_block(g, c):
        r0 = pl.multiple_of(c * blk, blk)
        qv = _conv_silu_rows(q_ref.at[g], r0, blk, seqlen, cwq, None)
        kv = _conv_silu_rows(k_ref.at[g], r0, blk, seqlen, cwk, None)
        qc = qv * (lax.rsqrt(jnp.sum(qv * qv, axis=-1, keepdims=True) + RMS_EPS) * qk_scale)
        kc = kv * lax.rsqrt(jnp.sum(kv * kv, axis=-1, keepdims=True) + RMS_EPS)
        vc = _conv_silu_rows(v_ref.at[g], r0, blk, seqlen, cwv, None)
        o_ref[g, pl.ds(r0, blk), :] = jnp.zeros((blk, 256), F32)
        sm = smt_ref[g, :, pl.ds(r0, blk)]
        return dict(g=g, r0=r0, wq0=pl.multiple_of(c * 2 * blk, 2 * blk), g0=pl.multiple_of(c * 8, 8),
                    qc=qc, kc=kc, vc=vc, g_t=a_neg * _softplus(sm + bias), beta_t=_sigmoid(sm))

    def prepare(blocks):
        for b in blocks:
            kc_b = b["kc"].astype(BF16)
            b["kk"] = lax.dot_general(kc_b, kc_b, nt, preferred_element_type=F32)
            b["qk"] = lax.dot_general(b["qc"].astype(BF16), kc_b, nt, preferred_element_type=F32)
        units = [(b, d) for b in blocks for d in range(2)]
        gcs = [jnp.dot(b["g_t"], cum_ms[d], precision=HI, preferred_element_type=F32) for b, d in units]
        tots = [gc[:, lasts[d]:lasts[d] + 1] for (b, d), gc in zip(units, gcs)]
        colss = [_transpose_rows([gc, jnp.exp(gc), b["beta_t"]]) for (b, d), gc in zip(units, gcs)]
        for (b, d), tot in zip(units, tots):
            gl_s[b["g"], d, pl.ds(b["g0"], 8), :] = jnp.broadcast_to(jnp.exp(tot), (8, blk))
        for b in blocks:
            b["kc_t"] = b["kc"].T
        chains = []
        for (b, d), gc, tot, cols in zip(units, gcs, tots, colss):
            kd_t = jnp.exp(tot - gc)
            for r in range(2):
                k = 2 * d + r
                e = jnp.exp(cols[:, k:k + 1] - gc[k:k + 1, :])
                beta = jnp.broadcast_to(cols[:, 20 + k:21 + k], (blk, GDN_HEAD))
                eg = jnp.broadcast_to(cols[:, 8 + k:9 + k], (blk, GDN_HEAD))
                vr = b["vc"][:, r * 128:(r + 1) * 128]
                g = b["g"]
                wq_s[g, d, r, pl.ds(b["wq0"] + blk, blk), :] = (b["qc"] * eg).astype(BF16)
                qkm_s[g, d, r, pl.ds(b["r0"], blk), :] = (b["qk"] * jnp.where(incls[d], e, 0.0)).astype(BF16)
                kdt_s[g, d, r, pl.ds(b["r0"], blk), :] = (b["kc_t"] * kd_t[k:k + 1, :]).astype(BF16)
                chains.append(dict(
                    g=g, d=d, r=r, r0=b["r0"], wq0=b["wq0"],
                    a=beta * b["kk"] * jnp.where(stricts[d], e, 0.0),
                    rhs=jnp.concatenate([vr * beta, b["kc"] * (beta * eg)], axis=1)))
        return chains

    chains_per_block = 4
    per_iter = max(1, GDN_CHAINS // (chains_per_block * nseq))
    while nb % per_iter:
        per_iter //= 2

    def phase1(i, carry):
        chains = prepare([load_block(g, i * per_iter + j) for g in range(nseq) for j in range(per_iter)])
        sols = _unit_tri_solve([ch["a"] for ch in chains], [ch["rhs"] for ch in chains], merge_masks)
        for sol, ch in zip(sols, chains):
            u_s[ch["g"], ch["d"], ch["r"], pl.ds(ch["r0"], blk), :] = sol[:, :GDN_HEAD]
            wq_s[ch["g"], ch["d"], ch["r"], pl.ds(ch["wq0"], blk), :] = sol[:, GDN_HEAD:].astype(BF16)
        return carry

    lax.fori_loop(0, nb // per_iter, phase1, 0)

    def phase2(i, carry):
        jobs = []
        for g in range(nseq):
            for d, c in ((0, i), (1, nb - 1 - i)):
                r0 = pl.multiple_of(c * blk, blk)
                wq0 = pl.multiple_of(c * 2 * blk, 2 * blk)
                gl = gl_s[g, d, pl.ds(pl.multiple_of(c * 8, 8), 8), :]
                for r in range(2):
                    jobs.append(((g, d, r), r0, wq0, gl[2 * d + r:2 * d + r + 1, :]))
        ss = [st[idx] for idx, _, _, _ in jobs]
        wss = [jnp.dot(wq_s[idx + (pl.ds(wq0, 2 * blk), slice(None))], s.astype(BF16), preferred_element_type=F32)
               for (idx, _, wq0, _), s in zip(jobs, ss)]
        vns = [(u_s[idx + (pl.ds(r0, blk), slice(None))] - ws[:blk]).astype(BF16)
               for (idx, r0, _, _), ws in zip(jobs, wss)]
        for (idx, r0, _, gl), s, vn in zip(jobs, ss, vns):
            st[idx] = gl * s + jnp.dot(kdt_s[idx + (pl.ds(r0, blk), slice(None))], vn, preferred_element_type=F32)
        outs = [ws[blk:] + jnp.dot(qkm_s[idx + (pl.ds(r0, blk), slice(None))], vn, preferred_element_type=F32)
                for (idx, r0, _, _), ws, vn in zip(jobs, wss, vns)]
        for pair in range(2 * nseq):
            (g, _, _), r0 = jobs[2 * pair][0], jobs[2 * pair][1]
            o_ref[g, pl.ds(r0, blk), :] += jnp.concatenate(outs[2 * pair:2 * pair + 2], axis=1)
        return carry

    lax.fori_loop(0, nb, phase2, 0)

    def emit(c, carry):
        r0 = pl.multiple_of(c * blk, blk)
        for g in range(nseq):
            oout_ref[g, pl.ds(r0, blk), :] = o_ref[g, pl.ds(r0, blk), :].astype(oout_ref.dtype)
        return carry

    lax.fori_loop(0, nb, emit, 0)

    if want_final:
        for g in range(nseq):
            for d in range(2):
                for r in range(2):
                    sfin_refs[d][g, r] = st[g, d, r]


def _gdn_scan_call(proj3, smt, conv_w, prm, s0, want_final, name):
    bsz, seqlen, _ = proj3.shape
    has_init = s0 is not None
    nseq = max(2, min(bsz, GDN_CHAINS * GDN_BLOCK // (4 * seqlen)))
    while bsz % nseq:
        nseq -= 1
    in_specs = [
        pl.BlockSpec((nseq, seqlen, 128), lambda b, g: (b, 0, g)),
        pl.BlockSpec((nseq, seqlen, 128), lambda b, g: (b, 0, GDN_K_HEADS + g)),
        pl.BlockSpec((nseq, seqlen, 256), lambda b, g: (b, 0, GDN_K_HEADS + g)),
        pl.BlockSpec((3, 128), lambda b, g: (0, g)),
        pl.BlockSpec((3, 128), lambda b, g: (0, GDN_K_HEADS + g)),
        pl.BlockSpec((3, 256), lambda b, g: (0, GDN_K_HEADS + g)),
        pl.BlockSpec((nseq, None, 8, seqlen), lambda b, g: (b, g, 0, 0)),
        pl.BlockSpec((None, 8, 2), lambda b, g: (g, 0, 0)),
    ]
    args = [proj3, proj3, proj3, conv_w, conv_w, conv_w, smt, prm]
    state_spec = pl.BlockSpec((nseq, 2, GDN_HEAD, GDN_HEAD), lambda b, g: (b, g, 0, 0))
    if has_init:
        in_specs += [state_spec, state_spec]
        args += [s0[0], s0[1]]
    out_specs = [pl.BlockSpec((nseq, seqlen, 256), lambda b, g: (b, 0, g))]
    out_shape = [jax.ShapeDtypeStruct((bsz, seqlen, GDN_V), BF16)]
    if want_final:
        out_specs += [state_spec, state_spec]
        out_shape += [jax.ShapeDtypeStruct((bsz, GDN_V_HEADS, GDN_HEAD, GDN_HEAD), F32)] * 2
    return pl.pallas_call(
        functools.partial(_gdn_scan_kernel, seqlen=seqlen, nseq=nseq, has_init=has_init, want_final=want_final),
        grid=(bsz // nseq, GDN_K_HEADS),
        in_specs=in_specs,
        out_specs=out_specs,
        out_shape=out_shape,
        scratch_shapes=[pltpu.VMEM((nseq, 2, 2, seqlen, GDN_HEAD), F32),
                        pltpu.VMEM((nseq, 2, 2, 2 * seqlen, GDN_HEAD), BF16),
                        pltpu.VMEM((nseq, 2, 2, seqlen, GDN_BLOCK), BF16),
                        pltpu.VMEM((nseq, 2, 2, seqlen, GDN_BLOCK), BF16),
                        pltpu.VMEM((nseq, 2, 8 * (seqlen // GDN_BLOCK), GDN_BLOCK), F32),
                        pltpu.VMEM((nseq, 2, 2, GDN_HEAD, GDN_HEAD), F32),
                        pltpu.VMEM((nseq, seqlen, 256), F32)],
        compiler_params=_params("arbitrary", "arbitrary"),
        name=name,
    )(*args)


def _residual_layer_norm(x, gate, o, ln_g, ln_b):
    r = DEEPNORM_ALPHA * x + gate * o
    mu = jnp.mean(r, axis=-1, keepdims=True)
    rc = r - mu
    var = jnp.mean(rc * rc, axis=-1, keepdims=True)
    return rc * lax.rsqrt(var + LN_EPS) * ln_g + ln_b


def _out_kernel(y_ref, z_ref, nw_ref, w_ref, x_ref, gt_ref, lg_ref, lb_ref, o_ref, *, mode):
    if mode == "ssd":
        u = y_ref[...].astype(F32) * _silu(z_ref[...].astype(F32))
        ms = jnp.mean(u * u, axis=-1, keepdims=True)
        a = (u * lax.rsqrt(ms + RMS_EPS) * nw_ref[...]).astype(BF16)
    else:
        parts = []
        for h in range(GDN_V_HEADS):
            sl = slice(h * GDN_HEAD, (h + 1) * GDN_HEAD)
            oh = y_ref[:, sl].astype(F32)
            ms = jnp.mean(oh * oh, axis=-1, keepdims=True)
            parts.append((oh * lax.rsqrt(ms + RMS_EPS) * nw_ref[...] * _silu(z_ref[:, sl].astype(F32))).astype(BF16))
        a = jnp.concatenate(parts, axis=1)
    o = jnp.dot(a, w_ref[...], preferred_element_type=F32)
    o_ref[...] = _residual_layer_norm(x_ref[...], gt_ref[...], o, lg_ref[...], lb_ref[...])


def _out_call(mode, act, z_src, z_blk, norm_w, w, x, gate, ln_g, ln_b, rows_per_mod, tm, name):
    m, kdim = act.shape
    w3, layer = w
    row_spec = pl.BlockSpec((tm, D_MODEL), lambda i: (i, 0))
    vec_spec = pl.BlockSpec((1, D_MODEL), lambda i: (0, 0))
    in_specs = [pl.BlockSpec((tm, kdim), lambda i: (i, 0)),
                pl.BlockSpec((tm, kdim), lambda i: (i, z_blk)),
                pl.BlockSpec((1, norm_w.shape[1]), lambda i: (0, 0)),
                pl.BlockSpec((None, kdim, D_MODEL), lambda i: (layer, 0, 0)), row_spec,
                pl.BlockSpec((None, 1, D_MODEL), lambda i: ((i * tm) // rows_per_mod, 0, 0)),
                vec_spec, vec_spec]
    args = [act, z_src, norm_w, w3, x, gate, ln_g, ln_b]
    return pl.pallas_call(
        functools.partial(_out_kernel, mode=mode),
        grid=(m // tm,),
        in_specs=in_specs,
        out_specs=row_spec,
        out_shape=jax.ShapeDtypeStruct((m, D_MODEL), F32),
        compiler_params=_params("arbitrary"),
        name=name,
    )(*args)


def _ffn_down_kernel(*refs, tm, period, row_taps, tiles_per_seq):
    if row_taps:
        (a_ref, ap_ref, an_ref, v_ref, cw_ref, cb_ref, w_ref, x_ref, gt_ref, lg_ref, lb_ref,
         o_ref, acc_ref, ext_ref) = refs
        t = pl.program_id(0) % tiles_per_seq
        has_prev = (t > 0).astype(F32)
        has_next = (t < tiles_per_seq - 1).astype(F32)
        base, taps, n = HALO, (0, 1, 2), tm + 2 * HALO
    else:
        a_ref, v_ref, cw_ref, cb_ref, w_ref, x_ref, gt_ref, lg_ref, lb_ref, o_ref, acc_ref = refs
        base, taps, n = 0, (1,), tm
    colpos = lax.broadcasted_iota(jnp.int32, (n, FFN_TC), 0) % period
    is_first = colpos == 0
    is_last = colpos == period - 1
    n_ct = D_FF // FFN_TC

    def gated_tile(c):
        cols = pl.ds(pl.multiple_of(c * FFN_TC, FFN_TC), FFN_TC)
        if row_taps:
            ext_ref[0:HALO, :] = ap_ref[:, cols].astype(F32) * has_prev
            ext_ref[HALO:HALO + tm, :] = a_ref[:, cols].astype(F32)
            ext_ref[HALO + tm:, :] = an_ref[:, cols].astype(F32) * has_next
            ext = ext_ref[...]
        else:
            ext = a_ref[:, cols].astype(F32)
        em1 = jnp.where(is_first, 0.0, pltpu.roll(ext, 1, 0))
        ep1 = jnp.where(is_last, 0.0, pltpu.roll(ext, n - 1, 0))
        conv = cb_ref[:, cols]
        for dr in taps:
            off = base + (dr - 1) * GRID_W
            conv = (conv + cw_ref[3 * dr:3 * dr + 1, cols] * em1[off:off + tm]
                    + cw_ref[3 * dr + 1:3 * dr + 2, cols] * ext[off:off + tm]
                    + cw_ref[3 * dr + 2:3 * dr + 3, cols] * ep1[off:off + tm])
        return (_silu(conv) * v_ref[:, cols].astype(F32)).astype(BF16)

    acc_ref[...] = jnp.zeros((tm, D_MODEL), F32)

    def channel_tile(c, carry):
        rows = pl.ds(pl.multiple_of(c * FFN_TC, FFN_TC), FFN_TC)
        acc_ref[...] += jnp.dot(gated_tile(c), w_ref[rows, :], preferred_element_type=F32)
        return carry

    lax.fori_loop(0, n_ct, channel_tile, 0)
    o_ref[...] = _residual_layer_norm(x_ref[...], gt_ref[...], acc_ref[...], lg_ref[...], lb_ref[...])


def _ffn_down_call(up, conv_w9, conv_b, w_down, x, gate, ln_g, ln_b, seqlen, is_grid, rows_per_mod, tm, name):
    m = up.shape[0]
    w3, layer = w_down
    a_spec = pl.BlockSpec((tm, D_FF), lambda i: (i, 0))
    v_spec = pl.BlockSpec((tm, D_FF), lambda i: (i, 1))
    scratch = [pltpu.VMEM((tm, D_MODEL), F32)]
    if is_grid:
        scratch.append(pltpu.VMEM((tm + 2 * HALO, FFN_TC), F32))
        hb = tm // HALO
        nhb = m // HALO
        tiles_per_seq = seqlen // tm
        in_specs = [a_spec,
                    pl.BlockSpec((HALO, D_FF), lambda i: (jnp.maximum(i * hb - 1, 0), 0)),
                    pl.BlockSpec((HALO, D_FF), lambda i: (jnp.minimum((i + 1) * hb, nhb - 1), 0)),
                    v_spec]
        args = [up, up, up, up]
        period = GRID_W
    else:
        tiles_per_seq = 1
        in_specs = [a_spec, v_spec]
        args = [up, up]
        period = seqlen
    row_spec = pl.BlockSpec((tm, D_MODEL), lambda i: (i, 0))
    vec_spec = pl.BlockSpec((1, D_MODEL), lambda i: (0, 0))
    in_specs += [pl.BlockSpec((9, D_FF), lambda i: (0, 0)), pl.BlockSpec((1, D_FF), lambda i: (0, 0)),
                 pl.BlockSpec((None, D_FF, D_MODEL), lambda i: (layer, 0, 0)), row_spec,
                 pl.BlockSpec((None, 1, D_MODEL), lambda i: ((i * tm) // rows_per_mod, 0, 0)),
                 vec_spec, vec_spec]
    args += [conv_w9, conv_b, w3, x, gate, ln_g, ln_b]
    return pl.pallas_call(
        functools.partial(_ffn_down_kernel, tm=tm, period=period, row_taps=is_grid, tiles_per_seq=tiles_per_seq),
        grid=(m // tm,),
        in_specs=in_specs,
        out_specs=row_spec,
        out_shape=jax.ShapeDtypeStruct((m, D_MODEL), F32),
        scratch_shapes=scratch,
        compiler_params=_params("arbitrary"),
        name=name,
    )(*args)


def _group_rows(t, bsz, seqlen, lead):
    n_lead = int(math.prod(lead))
    t = t.reshape(bsz, seqlen, n_lead, 8, -1)
    r = t.shape[-1]
    t = jnp.transpose(t, (0, 3, 2, 4, 1))
    return t.reshape(bsz, 8, n_lead * r, seqlen)


def _group_params(p):
    r = p.shape[1] // 8
    return jnp.transpose(p.reshape(2, 8, r), (1, 0, 2)).reshape(8, 2 * r)


def _run_tokens(x, seqlen, mods, states, want_final, w, tag):
    bsz = x.shape[0]
    m = bsz * seqlen
    nb = mods.shape[1]
    rows_per_mod = m // nb
    xt = x.reshape(m, D_MODEL)
    finals = []
    is_grid = tag == "s"
    for i in range(DEPTH):
        def mod(k):
            return mods[i, :, k].reshape(nb, 1, D_MODEL)
        lw = w[i]
        proj, small = _proj_call(xt, mod(1), mod(0), lw["w_in"], lw["w_in_small"], rows_per_mod,
                                 tm=1024, tn=1536, name=f"inproj{i}{tag}")
        proj3 = proj.reshape(bsz, seqlen, 6144)
        if i % 2 == 0:
            dtt = _group_rows(small[:, :2 * SSD_HEADS], bsz, seqlen, (2,))
            res = _ssd_scan_call(proj3, dtt, lw["conv_w"], lw["conv_b"], lw["prm"], lw["dskip"],
                                 states[i], want_final, name=f"ssdscan{tag}")
            mode, z_blk = "ssd", 0
        else:
            smt = _group_rows(small[:, :4 * GDN_V_HEADS], bsz, seqlen, (2, 2))
            res = _gdn_scan_call(proj3, smt, lw["conv_w"], lw["prm"], states[i], want_final,
                                 name=f"gdnscan{tag}")
            mode, z_blk = "gdn", 2
        y = res[0].reshape(m, 2048)
        if want_final:
            finals.append((res[1], res[2]))
        xt = _out_call(mode, y, proj, z_blk, lw["norm_w"], lw["w_out"], xt, mod(2), lw["ln_g0"], lw["ln_b0"],
                       rows_per_mod, tm=512, name=f"outproj{i}{tag}")
        up = _proj_call(xt, mod(4), mod(3), lw["w_up"], None, rows_per_mod,
                        tm=1024, tn=1408, name=f"ffnup{i}{tag}")[0]
        xt = _ffn_down_call(up, lw["ffn_conv_w"], lw["ffn_conv_b"], lw["w_down"], xt, mod(5),
                            lw["ln_g1"], lw["ln_b1"], seqlen, is_grid, rows_per_mod, tm=512,
                            name=f"ffndown{i}{tag}")
    return xt.reshape(bsz, seqlen, D_MODEL), finals


def kernel(x_prompt, x_sample, state_ssd_fwd, state_ssd_bwd, state_gdn_fwd, state_gdn_bwd, c, c_ctx, w_ada, b_ada, ln_g, ln_b, ssd_w_in, ssd_conv_w, ssd_conv_b, ssd_dt_bias, ssd_a_log, ssd_d, ssd_norm_w, ssd_w_out, gdn_w_in, gdn_conv_w, gdn_dt_bias, gdn_a_log, gdn_norm_w, gdn_w_out, ffn_w_up, ffn_conv_w, ffn_conv_b, ffn_w_down):
    n_dec = c.shape[0]
    cond = jnp.concatenate([c_ctx[None, :], c, jnp.zeros((16 - 1 - n_dec, D_MODEL), F32)], axis=0)
    ada = _ada_call(cond, w_ada, b_ada).reshape(DEPTH, 16, 6, D_MODEL)
    mods_p = ada[:, 0:1]
    mods_s = ada[:, 1:1 + n_dec]

    def small_cols(wm):
        return jnp.pad(wm, ((0, 0), (0, 128 - wm.shape[1]))).astype(BF16)

    ssd_w_in_b, ssd_w_out_b = ssd_w_in.astype(BF16), ssd_w_out.astype(BF16)
    gdn_w_in_b, gdn_w_out_b = gdn_w_in.astype(BF16), gdn_w_out.astype(BF16)
    ffn_w_up_b, ffn_w_down_b = ffn_w_up.astype(BF16), ffn_w_down.astype(BF16)

    weights = []
    for i in range(DEPTH):
        j = i // 2
        lw = {
            "ln_g0": ln_g[i, 0:1], "ln_b0": ln_b[i, 0:1], "ln_g1": ln_g[i, 1:2], "ln_b1": ln_b[i, 1:2],
            "w_up": (ffn_w_up_b, i), "w_down": (ffn_w_down_b, i),
            "ffn_conv_w": ffn_conv_w[i].reshape(9, D_FF), "ffn_conv_b": ffn_conv_b[i].reshape(1, D_FF),
        }
        if i % 2 == 0:
            lw["w_in"] = (ssd_w_in_b, j)
            lw["w_in_small"] = small_cols(ssd_w_in_b[j][:, 6144:])
            lw["conv_w"] = ssd_conv_w[j]
            lw["conv_b"] = ssd_conv_b[j].reshape(1, SSD_XBC)
            lw["prm"] = jnp.stack([_group_params(ssd_dt_bias[j]), _group_params(ssd_a_log[j])], axis=-1)
            lw["dskip"] = jnp.repeat(ssd_d[j], SSD_HEADDIM).reshape(SSD_GROUPS, 1, 256)
            lw["norm_w"] = ssd_norm_w[j].reshape(1, SSD_D_INNER)
            lw["w_out"] = (ssd_w_out_b, j)
        else:
            lw["w_in"] = (gdn_w_in_b, j)
            lw["w_in_small"] = small_cols(gdn_w_in_b[j][:, 6144:])
            lw["conv_w"] = gdn_conv_w[j]
            prm = jnp.stack([_group_params(gdn_dt_bias[j]), _group_params(gdn_a_log[j])], axis=-1)
            lw["prm"] = jnp.pad(prm, ((0, 0), (0, 4), (0, 0)))
            lw["norm_w"] = gdn_norm_w[j].reshape(1, GDN_HEAD)
            lw["w_out"] = (gdn_w_out_b, j)
        weights.append(lw)

    states_s = [(state_ssd_fwd[:, 0], state_ssd_bwd[:, 0]), (state_gdn_fwd[:, 0], state_gdn_bwd[:, 0])]
    y_prompt, fin = _run_tokens(x_prompt, x_prompt.shape[1], mods_p, [None, None], True, weights, "p")
    y_sample, _ = _run_tokens(x_sample, x_sample.shape[1], mods_s, states_s, False, weights, "s")
    return (y_prompt, y_sample,
            fin[0][0][:, None], fin[0][1][:, None], fin[1][0][:, None], fin[1][1][:, None])
```

```python
import functools
import math

import jax
import jax.numpy as jnp
from jax import lax
from jax.experimental import pallas as pl
from jax.experimental.pallas import tpu as pltpu

F32 = jnp.float32
BF16 = jnp.bfloat16
HI = lax.Precision.HIGHEST

D_MODEL = 1024
DEPTH = 2
GRID_W = 64

SSD_D_INNER = 2048
SSD_HEADDIM = 64
SSD_HEADS = 32
SSD_GROUPS = 8
SSD_STATE = 128
SSD_CHUNK = 128
SSD_BLOCKS = 4
SSD_XBC = 4096

GDN_K_HEADS = 8
GDN_V_HEADS = 16
GDN_HEAD = 128
GDN_QK = 1024
GDN_V = 2048
GDN_QKV = 4096
GDN_BLOCK = 128
GDN_CHUNKS_PER_ITER = 2
GDN_MAX_SEQS = 4
GDN_SCRATCH_BUDGET = 24 * 1024 * 1024

D_FF = 2816
FFN_TC = 256
HALO = 128

DEEPNORM_ALPHA = (2.0 * DEPTH) ** 0.25
LN_EPS = 1e-5
RMS_EPS = 1e-6

VMEM_LIMIT = 56 * 1024 * 1024
BF16_ROWS = 16


def _silu(x):
    h = 0.5 * x
    return h + h * jnp.tanh(h)


def _sigmoid(x):
    return 0.5 + 0.5 * jnp.tanh(0.5 * x)


def _softplus(x):
    return jnp.maximum(x, 0.0) + jnp.log(1.0 + jnp.exp(-jnp.abs(x)))


def _params(*sem):
    return pltpu.CompilerParams(dimension_semantics=sem, vmem_limit_bytes=VMEM_LIMIT)


def _ada_kernel(c_ref, w_ref, b_ref, o_ref):
    h = _silu(c_ref[...]).astype(BF16)
    o_ref[...] = jnp.dot(h, w_ref[...].astype(BF16), preferred_element_type=F32) + b_ref[...]


def _ada_call(cond, w_ada, b_ada):
    rows = cond.shape[0]
    tn = 1536
    return pl.pallas_call(
        _ada_kernel,
        grid=(DEPTH, 6 * D_MODEL // tn),
        in_specs=[
            pl.BlockSpec((rows, D_MODEL), lambda l, j: (0, 0)),
            pl.BlockSpec((None, D_MODEL, tn), lambda l, j: (l, 0, j)),
            pl.BlockSpec((None, 1, tn), lambda l, j: (l, 0, j)),
        ],
        out_specs=pl.BlockSpec((None, rows, tn), lambda l, j: (l, 0, j)),
        out_shape=jax.ShapeDtypeStruct((DEPTH, rows, 6 * D_MODEL), F32),
        compiler_params=_params("arbitrary", "arbitrary"),
        name="adaln",
    )(cond, w_ada, b_ada.reshape(DEPTH, 1, 6 * D_MODEL))


def _proj_kernel(*refs, has_small):
    if has_small:
        x_ref, sc_ref, sh_ref, w_ref, w2_ref, o_ref, o2_ref, h_ref = refs
    else:
        x_ref, sc_ref, sh_ref, w_ref, o_ref, h_ref = refs

    @pl.when(pl.program_id(1) == 0)
    def _():
        h = x_ref[...] * (1.0 + sc_ref[...]) + sh_ref[...]
        h_ref[...] = h.astype(BF16)
        if has_small:
            o2_ref[...] = jnp.dot(h_ref[...], w2_ref[...], preferred_element_type=F32)

    o_ref[...] = jnp.dot(h_ref[...], w_ref[...], preferred_element_type=F32).astype(o_ref.dtype)


def _proj_call(x, scale, shift, w, w_small, rows_per_mod, tm, tn, name):
    m = x.shape[0]
    w3, layer = w
    n = (w3.shape[2] // tn) * tn
    has_small = w_small is not None
    mod_spec = pl.BlockSpec((None, 1, D_MODEL), lambda i, j: ((i * tm) // rows_per_mod, 0, 0))
    in_specs = [pl.BlockSpec((tm, D_MODEL), lambda i, j: (i, 0)), mod_spec, mod_spec,
                pl.BlockSpec((None, D_MODEL, tn), lambda i, j: (layer, 0, j))]
    out_specs = [pl.BlockSpec((tm, tn), lambda i, j: (i, j))]
    out_shape = [jax.ShapeDtypeStruct((m, n), BF16)]
    args = [x, scale, shift, w3]
    if has_small:
        in_specs.append(pl.BlockSpec((D_MODEL, 128), lambda i, j: (0, 0)))
        out_specs.append(pl.BlockSpec((tm, 128), lambda i, j: (i, 0)))
        out_shape.append(jax.ShapeDtypeStruct((m, 128), F32))
        args.append(w_small)
    return pl.pallas_call(
        functools.partial(_proj_kernel, has_small=has_small),
        grid=(m // tm, n // tn),
        in_specs=in_specs,
        out_specs=out_specs,
        out_shape=out_shape,
        scratch_shapes=[pltpu.VMEM((tm, D_MODEL), BF16)],
        compiler_params=_params("arbitrary", "arbitrary"),
        name=name,
    )(*args)


def _conv_silu_rows(src_ref, r0, n_rows, seqlen, w_ref, bias):
    cur = src_ref[pl.ds(r0, n_rows), :].astype(F32)
    has_prev = (r0 > 0).astype(F32)
    has_next = (r0 + n_rows < seqlen).astype(F32)
    g_prev = pl.multiple_of(jnp.maximum(r0 - BF16_ROWS, 0), BF16_ROWS)
    g_next = pl.multiple_of(jnp.minimum(r0 + n_rows, seqlen - BF16_ROWS), BF16_ROWS)
    prev_row = src_ref[pl.ds(g_prev, BF16_ROWS), :].astype(F32)[BF16_ROWS - 1:BF16_ROWS] * has_prev
    next_row = src_ref[pl.ds(g_next, BF16_ROWS), :].astype(F32)[0:1] * has_next
    rid = lax.broadcasted_iota(jnp.int32, cur.shape, 0)
    xm1 = jnp.where(rid == 0, prev_row, pltpu.roll(cur, 1, 0))
    xp1 = jnp.where(rid == n_rows - 1, next_row, pltpu.roll(cur, n_rows - 1, 0))
    out = w_ref[0:1, :] * xm1 + w_ref[1:2, :] * cur + w_ref[2:3, :] * xp1
    if bias is not None:
        out = out + bias
    return _silu(out)


def _transpose_rows(rows_list):
    pad = 128 - 8 * len(rows_list)
    stack = jnp.concatenate(rows_list + [jnp.zeros((pad, 128), F32)], axis=0)
    return stack.T


def _ssd_scan_kernel(*refs, seqlen, nseq, has_init, want_final):
    refs = list(refs)
    x_ref, b_ref, c_ref, cwx, cwb, cwc, cbx, cbb, cbc, dtt_ref, prm_ref, dsk_ref = refs[:12]
    pos = 12
    if has_init:
        s0_refs = refs[pos:pos + 2]
        pos += 2
    yout_ref = refs[pos]
    pos += 1
    if want_final:
        sfin_refs = refs[pos:pos + 2]
        pos += 2
    cs_b, ea_s, ns_s, st, y_ref = refs[pos:pos + 5]

    nc = seqlen // SSD_CHUNK
    q = SSD_CHUNK

    for g in range(nseq):
        for d in range(2):
            if has_init:
                halves = []
                for h in range(2):
                    blk = jnp.concatenate([s0_refs[d][g, 2 * h], s0_refs[d][g, 2 * h + 1]], axis=0)
                    halves.append(blk.T)
                st[g, d] = jnp.concatenate(halves, axis=1)
            else:
                st[g, d] = jnp.zeros((SSD_STATE, 256), F32)

    row = lax.broadcasted_iota(jnp.int32, (q, q), 0)
    col = lax.broadcasted_iota(jnp.int32, (q, q), 1)
    lane_head = lax.broadcasted_iota(jnp.int32, (q, 256), 1) // SSD_HEADDIM
    head_masks = [jnp.where(lane_head == r, 1.0, 0.0).astype(BF16) for r in range(4)]
    bias = prm_ref[:, 0:1]
    a_neg = -jnp.exp(prm_ref[:, 1:2])
    dskip = dsk_ref[...]

    cum_ms = ((row <= col).astype(F32), (row >= col).astype(F32))
    keeps = (row >= col, row <= col)
    lasts = (q - 1, 0)

    def intra(i, carry):
        blocks = []
        for g, j in ((g, j) for g in range(nseq) for j in range(unroll_a)):
            r0 = pl.multiple_of((i * unroll_a + j) * q, q)
            xc = _conv_silu_rows(x_ref.at[g], r0, q, seqlen, cwx, cbx[...])
            bc = _conv_silu_rows(b_ref.at[g], r0, q, seqlen, cwb, cbb[...])
            cc = _conv_silu_rows(c_ref.at[g], r0, q, seqlen, cwc, cbc[...])
            cc_b = cc.astype(BF16)
            cs_b[g, pl.ds(r0, q), :] = cc_b
            bc_t = bc.T
            dt_t = _softplus(dtt_ref[g, :, pl.ds(r0, q)] + bias)
            xc_b = xc.astype(BF16)
            x_heads = jnp.concatenate([xc_b * head_masks[r] for r in range(4)], axis=0)
            blocks.append(dict(g=g, r0=r0, xc=xc, bc_t=bc_t, dt_t=dt_t, a_t=dt_t * a_neg, x_heads=x_heads,
                               scores=jnp.dot(cc_b, bc_t.astype(BF16), preferred_element_type=F32)))
        units = [(blk, d) for blk in blocks for d in range(2)]
        acums = [jnp.dot(blk["a_t"], cum_ms[d], precision=HI, preferred_element_type=F32) for blk, d in units]
        cols = _transpose_rows(acums)
        dtdes = [blk["dt_t"] * jnp.exp(ac[:, lasts[d]:lasts[d] + 1] - ac) for (blk, d), ac in zip(units, acums)]
        for (blk, d), ac in zip(units, acums):
            ea_t = jnp.exp(ac)
            ea_rows = jnp.concatenate([jnp.broadcast_to(ea_t[4 * d + r:4 * d + r + 1, :], (SSD_HEADDIM, q))
                                       for r in range(4)], axis=0)
            ea_s[blk["g"], d, pl.ds(blk["r0"], q), :] = jnp.concatenate([ea_rows[0:128].T, ea_rows[128:256].T], axis=1)
        lhss = []
        for u, ((blk, d), ac, dtde) in enumerate(zip(units, acums, dtdes)):
            parts = []
            for r in range(4):
                k = 4 * d + r
                decay = jnp.where(keeps[d], jnp.exp(cols[:, 8 * u + k:8 * u + k + 1] - ac[k:k + 1, :]), 0.0)
                m = decay * blk["scores"] * blk["dt_t"][k:k + 1, :]
                bt = blk["bc_t"] * dtde[k:k + 1, :]
                parts.append(jnp.concatenate([m, bt], axis=0).astype(BF16))
            lhss.append(jnp.concatenate(parts, axis=1))
        ress = [jnp.dot(lhs, blk["x_heads"], preferred_element_type=F32) for lhs, (blk, d) in zip(lhss, units)]
        for res, (blk, d) in zip(ress, units):
            ns_s[blk["g"], d, pl.ds(blk["r0"], q), :] = res[q:]
        for j, blk in enumerate(blocks):
            y_ref[blk["g"], pl.ds(blk["r0"], q), :] = blk["xc"] * dskip + ress[2 * j][:q] + ress[2 * j + 1][:q]
        return carry

    unroll_a = max(1, SSD_BLOCKS // nseq)
    while nc % unroll_a:
        unroll_a //= 2
    lax.fori_loop(0, nc // unroll_a, intra, 0)

    def inter(i, carry):
        jobs = []
        for g, d in ((g, d) for g in range(nseq) for d in range(2)):
            s_t = st[g, d]
            for j in range(unroll_a):
                c = i * unroll_a + j
                r0 = pl.multiple_of((c if d == 0 else nc - 1 - c) * q, q)
                jobs.append((g, d, r0, s_t.astype(BF16)))
                s_t = s_t * ea_s[g, d, pl.ds(r0 + lasts[d], 1), :] + ns_s[g, d, pl.ds(r0, q), :]
            st[g, d] = s_t
        y_inters = [jnp.dot(cs_b[g, pl.ds(r0, q), :], s_b, preferred_element_type=F32) * ea_s[g, d, pl.ds(r0, q), :]
                    for g, d, r0, s_b in jobs]
        for (g, d, r0, _), y_inter in zip(jobs, y_inters):
            y_ref[g, pl.ds(r0, q), :] += y_inter
        return carry

    lax.fori_loop(0, nc // unroll_a, inter, 0)

    def emit(c, carry):
        r0 = pl.multiple_of(c * q, q)
        for g in range(nseq):
            yout_ref[g, pl.ds(r0, q), :] = y_ref[g, pl.ds(r0, q), :].astype(yout_ref.dtype)
        return carry

    lax.fori_loop(0, nc, emit, 0)

    if want_final:
        for g, d in ((g, d) for g in range(nseq) for d in range(2)):
            s_t = st[g, d]
            for h in range(2):
                blk = s_t[:, h * 128:(h + 1) * 128].T
                sfin_refs[d][g, 2 * h] = blk[0:64]
                sfin_refs[d][g, 2 * h + 1] = blk[64:128]


def _ssd_scan_call(proj3, dtt, conv_w, conv_b, prm, dskip, s0, want_final, name):
    bsz, seqlen, _ = proj3.shape
    has_init = s0 is not None
    nseq = max(1, min(bsz, SSD_BLOCKS * SSD_CHUNK // seqlen))
    while bsz % nseq:
        nseq -= 1
    xoff = SSD_D_INNER // 256
    boff = (2 * SSD_D_INNER) // 128
    coff = boff + SSD_GROUPS
    in_specs = [
        pl.BlockSpec((nseq, seqlen, 256), lambda b, g: (b, 0, xoff + g)),
        pl.BlockSpec((nseq, seqlen, 128), lambda b, g: (b, 0, boff + g)),
        pl.BlockSpec((nseq, seqlen, 128), lambda b, g: (b, 0, coff + g)),
        pl.BlockSpec((3, 256), lambda b, g: (0, g)),
        pl.BlockSpec((3, 128), lambda b, g: (0, 16 + g)),
        pl.BlockSpec((3, 128), lambda b, g: (0, 24 + g)),
        pl.BlockSpec((1, 256), lambda b, g: (0, g)),
        pl.BlockSpec((1, 128), lambda b, g: (0, 16 + g)),
        pl.BlockSpec((1, 128), lambda b, g: (0, 24 + g)),
        pl.BlockSpec((nseq, None, 8, seqlen), lambda b, g: (b, g, 0, 0)),
        pl.BlockSpec((None, 8, 2), lambda b, g: (g, 0, 0)),
        pl.BlockSpec((None, 1, 256), lambda b, g: (g, 0, 0)),
    ]
    args = [proj3, proj3, proj3, conv_w, conv_w, conv_w, conv_b, conv_b, conv_b, dtt, prm, dskip]
    state_spec = pl.BlockSpec((nseq, 4, SSD_HEADDIM, SSD_STATE), lambda b, g: (b, g, 0, 0))
    if has_init:
        in_specs += [state_spec, state_spec]
        args += [s0[0], s0[1]]
    out_specs = [pl.BlockSpec((nseq, seqlen, 256), lambda b, g: (b, 0, g))]
    out_shape = [jax.ShapeDtypeStruct((bsz, seqlen, SSD_D_INNER), BF16)]
    if want_final:
        out_specs += [state_spec, state_spec]
        out_shape += [jax.ShapeDtypeStruct((bsz, SSD_HEADS, SSD_HEADDIM, SSD_STATE), F32)] * 2
    return pl.pallas_call(
        functools.partial(_ssd_scan_kernel, seqlen=seqlen, nseq=nseq, has_init=has_init, want_final=want_final),
        grid=(bsz // nseq, SSD_GROUPS),
        in_specs=in_specs,
        out_specs=out_specs,
        out_shape=out_shape,
        scratch_shapes=[pltpu.VMEM((nseq, seqlen, SSD_STATE), BF16),
                        pltpu.VMEM((nseq, 2, seqlen, 256), F32),
                        pltpu.VMEM((nseq, 2, seqlen, 256), F32),
                        pltpu.VMEM((nseq, 2, SSD_STATE, 256), F32),
                        pltpu.VMEM((nseq, seqlen, 256), F32)],
        compiler_params=_params("arbitrary", "arbitrary"),
        name=name,
    )(*args)


def _unit_tri_solve(a_list, rhs_list, merge_masks):
    xs = a_list
    for m in merge_masks:
        xs_b = [x.astype(BF16) for x in xs]
        xs = [x - jnp.dot(x_b, x_b * m, preferred_element_type=F32) for x, x_b in zip(xs, xs_b)]
    return [rhs - jnp.dot(x.astype(BF16), rhs.astype(BF16), preferred_element_type=F32)
            for x, rhs in zip(xs, rhs_list)]


def _gdn_scan_kernel(*refs, seqlen, nseq, has_init, want_final):
    refs = list(refs)
    q_ref, k_ref, v_ref, cwq, cwk, cwv, smt_ref, prm_ref = refs[:8]
    pos = 8
    if has_init:
        s0_refs = refs[pos:pos + 2]
        pos += 2
    oout_ref = refs[pos]
    pos += 1
    if want_final:
        sfin_refs = refs[pos:pos + 2]
        pos += 2
    u_s, wq_s, qkm_s, kdt_s, gl_s, st, o_ref = refs[pos:pos + 7]

    blk = GDN_BLOCK
    nb = seqlen // blk
    qk_scale = GDN_HEAD ** -0.5

    for g in range(nseq):
        for d in range(2):
            for r in range(2):
                if has_init:
                    st[g, d, r] = s0_refs[d][g, r]
                else:
                    st[g, d, r] = jnp.zeros((GDN_HEAD, GDN_HEAD), F32)

    row = lax.broadcasted_iota(jnp.int32, (blk, blk), 0)
    col = lax.broadcasted_iota(jnp.int32, (blk, blk), 1)
    merge_masks = [jnp.where(((row >> (l + 1)) == (col >> (l + 1))) & ((row >> l) != (col >> l)), 1.0, 0.0
                             ).astype(BF16) for l in range(7)]
    bias = prm_ref[:, 0:1]
    a_neg = -jnp.exp(prm_ref[:, 1:2])
    nt = (((1,), (1,)), ((), ()))

    cum_ms = ((row <= col).astype(F32), (row >= col).astype(F32))
    stricts = (row > col, row < col)
    incls = (row >= col, row <= col)
    lasts = (blk - 1, 0)

    def load_block(g, c):
        r0 = pl.multiple_of(c * blk, blk)
        qv = _conv_silu_rows(q_ref.at[g], r0, blk, seqlen, cwq, None)
        kv = _conv_silu_rows(k_ref.at[g], r0, blk, seqlen, cwk, None)
        qc = qv * (lax.rsqrt(jnp.sum(qv * qv, axis=-1, keepdims=True) + RMS_EPS) * qk_scale)
        kc = kv * lax.rsqrt(jnp.sum(kv * kv, axis=-1, keepdims=True) + RMS_EPS)
        vc = _conv_silu_rows(v_ref.at[g], r0, blk, seqlen, cwv, None)
        o_ref[g, pl.ds(r0, blk), :] = jnp.zeros((blk, 256), F32)
        sm = smt_ref[g, :, pl.ds(r0, blk)]
        return dict(g=g, r0=r0, wq0=pl.multiple_of(c * 2 * blk, 2 * blk), g0=pl.multiple_of(c * 8, 8),
                    qc=qc, kc=kc, vc=vc, g_t=a_neg * _softplus(sm + bias), beta_t=_sigmoid(sm))

    def prepare(blocks):
        for b in blocks:
            kc_b = b["kc"].astype(BF16)
            b["kk"] = lax.dot_general(kc_b, kc_b, nt, preferred_element_type=F32)
            b["qk"] = lax.dot_general(b["qc"].astype(BF16), kc_b, nt, preferred_element_type=F32)
        units = [(b, d) for b in blocks for d in range(2)]
        gcs = [jnp.dot(b["g_t"], cum_ms[d], precision=HI, preferred_element_type=F32) for b, d in units]
        tots = [gc[:, lasts[d]:lasts[d] + 1] for (b, d), gc in zip(units, gcs)]
        colss = [_transpose_rows([gc, jnp.exp(gc), b["beta_t"]]) for (b, d), gc in zip(units, gcs)]
        for (b, d), tot in zip(units, tots):
            gl_s[b["g"], d, pl.ds(b["g0"], 8), :] = jnp.broadcast_to(jnp.exp(tot), (8, blk))
        for b in blocks:
            b["kc_t"] = b["kc"].T
        chains = []
        for (b, d), gc, tot, cols in zip(units, gcs, tots, colss):
            kd_t = jnp.exp(tot - gc)
            for r in range(2):
                k = 2 * d + r
                e = jnp.exp(cols[:, k:k + 1] - gc[k:k + 1, :])
                beta = jnp.broadcast_to(cols[:, 20 + k:21 + k], (blk, GDN_HEAD))
                eg = jnp.broadcast_to(cols[:, 8 + k:9 + k], (blk, GDN_HEAD))
                vr = b["vc"][:, r * 128:(r + 1) * 128]
                g = b["g"]
                wq_s[g, d, r, pl.ds(b["wq0"] + blk, blk), :] = (b["qc"] * eg).astype(BF16)
                qkm_s[g, d, r, pl.ds(b["r0"], blk), :] = (b["qk"] * jnp.where(incls[d], e, 0.0)).astype(BF16)
                kdt_s[g, d, r, pl.ds(b["r0"], blk), :] = (b["kc_t"] * kd_t[k:k + 1, :]).astype(BF16)
                chains.append(dict(
                    g=g, d=d, r=r, r0=b["r0"], wq0=b["wq0"],
                    a=beta * b["kk"] * jnp.where(stricts[d], e, 0.0),
                    rhs=jnp.concatenate([vr * beta, b["kc"] * (beta * eg)], axis=1)))
        return chains

    per_iter = GDN_CHUNKS_PER_ITER
    while nb % per_iter:
        per_iter //= 2

    def phase1(i, carry):
        chains = prepare([load_block(g, i * per_iter + j) for g in range(nseq) for j in range(per_iter)])
        sols = _unit_tri_solve([ch["a"] for ch in chains], [ch["rhs"] for ch in chains], merge_masks)
        for sol, ch in zip(sols, chains):
            u_s[ch["g"], ch["d"], ch["r"], pl.ds(ch["r0"], blk), :] = sol[:, :GDN_HEAD]
            wq_s[ch["g"], ch["d"], ch["r"], pl.ds(ch["wq0"], blk), :] = sol[:, GDN_HEAD:].astype(BF16)
        return carry

    lax.fori_loop(0, nb // per_iter, phase1, 0)

    def phase2(i, carry):
        jobs = []
        for g in range(nseq):
            for d, c in ((0, i), (1, nb - 1 - i)):
                r0 = pl.multiple_of(c * blk, blk)
                wq0 = pl.multiple_of(c * 2 * blk, 2 * blk)
                gl = gl_s[g, d, pl.ds(pl.multiple_of(c * 8, 8), 8), :]
                for r in range(2):
                    jobs.append(((g, d, r), r0, wq0, gl[2 * d + r:2 * d + r + 1, :]))
        ss = [st[idx] for idx, _, _, _ in jobs]
        wss = [jnp.dot(wq_s[idx + (pl.ds(wq0, 2 * blk), slice(None))], s.astype(BF16), preferred_element_type=F32)
               for (idx, _, wq0, _), s in zip(jobs, ss)]
        vns = [(u_s[idx + (pl.ds(r0, blk), slice(None))] - ws[:blk]).astype(BF16)
               for (idx, r0, _, _), ws in zip(jobs, wss)]
        for (idx, r0, _, gl), s, vn in zip(jobs, ss, vns):
            st[idx] = gl * s + jnp.dot(kdt_s[idx + (pl.ds(r0, blk), slice(None))], vn, preferred_element_type=F32)
        outs = [ws[blk:] + jnp.dot(qkm_s[idx + (pl.ds(r0, blk), slice(None))], vn, preferred_element_type=F32)
                for (idx, r0, _, _), ws, vn in zip(jobs, wss, vns)]
        for pair in range(2 * nseq):
            (g, _, _), r0 = jobs[2 * pair][0], jobs[2 * pair][1]
            o_ref[g, pl.ds(r0, blk), :] += jnp.concatenate(outs[2 * pair:2 * pair + 2], axis=1)
        return carry

    lax.fori_loop(0, nb, phase2, 0)

    def emit(c, carry):
        r0 = pl.multiple_of(c * blk, blk)
        for g in range(nseq):
            oout_ref[g, pl.ds(r0, blk), :] = o_ref[g, pl.ds(r0, blk), :].astype(oout_ref.dtype)
        return carry

    lax.fori_loop(0, nb, emit, 0)

    if want_final:
        for g in range(nseq):
            for d in range(2):
                for r in range(2):
                    sfin_refs[d][g, r] = st[g, d, r]


def _gdn_scan_call(proj3, smt, conv_w, prm, s0, want_final, name):
    bsz, seqlen, _ = proj3.shape
    has_init = s0 is not None
    scratch_per_seq = seqlen * (4 * GDN_HEAD * (4 + 2 * 2 + 2 + 2) + 256 * 4)
    nseq = max(2, min(bsz, GDN_MAX_SEQS, GDN_SCRATCH_BUDGET // scratch_per_seq))
    while bsz % nseq:
        nseq -= 1
    in_specs = [
        pl.BlockSpec((nseq, seqlen, 128), lambda b, g: (b, 0, g)),
        pl.BlockSpec((nseq, seqlen, 128), lambda b, g: (b, 0, GDN_K_HEADS + g)),
        pl.BlockSpec((nseq, seqlen, 256), lambda b, g: (b, 0, GDN_K_HEADS + g)),
        pl.BlockSpec((3, 128), lambda b, g: (0, g)),
        pl.BlockSpec((3, 128), lambda b, g: (0, GDN_K_HEADS + g)),
        pl.BlockSpec((3, 256), lambda b, g: (0, GDN_K_HEADS + g)),
        pl.BlockSpec((nseq, None, 8, seqlen), lambda b, g: (b, g, 0, 0)),
        pl.BlockSpec((None, 8, 2), lambda b, g: (g, 0, 0)),
    ]
    args = [proj3, proj3, proj3, conv_w, conv_w, conv_w, smt, prm]
    state_spec = pl.BlockSpec((nseq, 2, GDN_HEAD, GDN_HEAD), lambda b, g: (b, g, 0, 0))
    if has_init:
        in_specs += [state_spec, state_spec]
        args += [s0[0], s0[1]]
    out_specs = [pl.BlockSpec((nseq, seqlen, 256), lambda b, g: (b, 0, g))]
    out_shape = [jax.ShapeDtypeStruct((bsz, seqlen, GDN_V), BF16)]
    if want_final:
        out_specs += [state_spec, state_spec]
        out_shape += [jax.ShapeDtypeStruct((bsz, GDN_V_HEADS, GDN_HEAD, GDN_HEAD), F32)] * 2
    return pl.pallas_call(
        functools.partial(_gdn_scan_kernel, seqlen=seqlen, nseq=nseq, has_init=has_init, want_final=want_final),
        grid=(bsz // nseq, GDN_K_HEADS),
        in_specs=in_specs,
        out_specs=out_specs,
        out_shape=out_shape,
        scratch_shapes=[pltpu.VMEM((nseq, 2, 2, seqlen, GDN_HEAD), F32),
                        pltpu.VMEM((nseq, 2, 2, 2 * seqlen, GDN_HEAD), BF16),
                        pltpu.VMEM((nseq, 2, 2, seqlen, GDN_BLOCK), BF16),
                        pltpu.VMEM((nseq, 2, 2, seqlen, GDN_BLOCK), BF16),
                        pltpu.VMEM((nseq, 2, 8 * (seqlen // GDN_BLOCK), GDN_BLOCK), F32),
                        pltpu.VMEM((nseq, 2, 2, GDN_HEAD, GDN_HEAD), F32),
                        pltpu.VMEM((nseq, seqlen, 256), F32)],
        compiler_params=_params("arbitrary", "arbitrary"),
        name=name,
    )(*args)


def _residual_layer_norm(x, gate, o, ln_g, ln_b):
    r = DEEPNORM_ALPHA * x + gate * o
    mu = jnp.mean(r, axis=-1, keepdims=True)
    rc = r - mu
    var = jnp.mean(rc * rc, axis=-1, keepdims=True)
    return rc * lax.rsqrt(var + LN_EPS) * ln_g + ln_b


def _out_kernel(y_ref, z_ref, nw_ref, w_ref, x_ref, gt_ref, lg_ref, lb_ref, o_ref, *, mode):
    if mode == "ssd":
        u = y_ref[...].astype(F32) * _silu(z_ref[...].astype(F32))
        ms = jnp.mean(u * u, axis=-1, keepdims=True)
        a = (u * lax.rsqrt(ms + RMS_EPS) * nw_ref[...]).astype(BF16)
    else:
        parts = []
        for h in range(GDN_V_HEADS):
            sl = slice(h * GDN_HEAD, (h + 1) * GDN_HEAD)
            oh = y_ref[:, sl].astype(F32)
            ms = jnp.mean(oh * oh, axis=-1, keepdims=True)
            parts.append((oh * lax.rsqrt(ms + RMS_EPS) * nw_ref[...] * _silu(z_ref[:, sl].astype(F32))).astype(BF16))
        a = jnp.concatenate(parts, axis=1)
    o = jnp.dot(a, w_ref[...], preferred_element_type=F32)
    o_ref[...] = _residual_layer_norm(x_ref[...], gt_ref[...], o, lg_ref[...], lb_ref[...])


def _out_call(mode, act, z_src, z_blk, norm_w, w, x, gate, ln_g, ln_b, rows_per_mod, tm, name):
    m, kdim = act.shape
    w3, layer = w
    row_spec = pl.BlockSpec((tm, D_MODEL), lambda i: (i, 0))
    vec_spec = pl.BlockSpec((1, D_MODEL), lambda i: (0, 0))
    in_specs = [pl.BlockSpec((tm, kdim), lambda i: (i, 0)),
                pl.BlockSpec((tm, kdim), lambda i: (i, z_blk)),
                pl.BlockSpec((1, norm_w.shape[1]), lambda i: (0, 0)),
                pl.BlockSpec((None, kdim, D_MODEL), lambda i: (layer, 0, 0)), row_spec,
                pl.BlockSpec((None, 1, D_MODEL), lambda i: ((i * tm) // rows_per_mod, 0, 0)),
                vec_spec, vec_spec]
    args = [act, z_src, norm_w, w3, x, gate, ln_g, ln_b]
    return pl.pallas_call(
        functools.partial(_out_kernel, mode=mode),
        grid=(m // tm,),
        in_specs=in_specs,
        out_specs=row_spec,
        out_shape=jax.ShapeDtypeStruct((m, D_MODEL), F32),
        compiler_params=_params("arbitrary"),
        name=name,
    )(*args)


def _ffn_down_kernel(*refs, tm, period, row_taps, tiles_per_seq):
    if row_taps:
        (a_ref, ap_ref, an_ref, v_ref, cw_ref, cb_ref, w_ref, x_ref, gt_ref, lg_ref, lb_ref,
         o_ref, acc_ref, ext_ref) = refs
        t = pl.program_id(0) % tiles_per_seq
        has_prev = (t > 0).astype(F32)
        has_next = (t < tiles_per_seq - 1).astype(F32)
        base, taps, n = HALO, (0, 1, 2), tm + 2 * HALO
    else:
        a_ref, v_ref, cw_ref, cb_ref, w_ref, x_ref, gt_ref, lg_ref, lb_ref, o_ref, acc_ref = refs
        base, taps, n = 0, (1,), tm
    colpos = lax.broadcasted_iota(jnp.int32, (n, FFN_TC), 0) % period
    is_first = colpos == 0
    is_last = colpos == period - 1
    n_ct = D_FF // FFN_TC

    def gated_tile(c):
        cols = pl.ds(pl.multiple_of(c * FFN_TC, FFN_TC), FFN_TC)
        if row_taps:
            ext_ref[0:HALO, :] = ap_ref[:, cols].astype(F32) * has_prev
            ext_ref[HALO:HALO + tm, :] = a_ref[:, cols].astype(F32)
            ext_ref[HALO + tm:, :] = an_ref[:, cols].astype(F32) * has_next
            ext = ext_ref[...]
        else:
            ext = a_ref[:, cols].astype(F32)
        em1 = jnp.where(is_first, 0.0, pltpu.roll(ext, 1, 0))
        ep1 = jnp.where(is_last, 0.0, pltpu.roll(ext, n - 1, 0))
        conv = cb_ref[:, cols]
        for dr in taps:
            off = base + (dr - 1) * GRID_W
            conv = (conv + cw_ref[3 * dr:3 * dr + 1, cols] * em1[off:off + tm]
                    + cw_ref[3 * dr + 1:3 * dr + 2, cols] * ext[off:off + tm]
                    + cw_ref[3 * dr + 2:3 * dr + 3, cols] * ep1[off:off + tm])
        return (_silu(conv) * v_ref[:, cols].astype(F32)).astype(BF16)

    acc_ref[...] = jnp.zeros((tm, D_MODEL), F32)

    def channel_tile(c, carry):
        rows = pl.ds(pl.multiple_of(c * FFN_TC, FFN_TC), FFN_TC)
        acc_ref[...] += jnp.dot(gated_tile(c), w_ref[rows, :], preferred_element_type=F32)
        return carry

    lax.fori_loop(0, n_ct, channel_tile, 0)
    o_ref[...] = _residual_layer_norm(x_ref[...], gt_ref[...], acc_ref[...], lg_ref[...], lb_ref[...])


def _ffn_down_call(up, conv_w9, conv_b, w_down, x, gate, ln_g, ln_b, seqlen, is_grid, rows_per_mod, tm, name):
    m = up.shape[0]
    w3, layer = w_down
    a_spec = pl.BlockSpec((tm, D_FF), lambda i: (i, 0))
    v_spec = pl.BlockSpec((tm, D_FF), lambda i: (i, 1))
    scratch = [pltpu.VMEM((tm, D_MODEL), F32)]
    if is_grid:
        scratch.append(pltpu.VMEM((tm + 2 * HALO, FFN_TC), F32))
        hb = tm // HALO
        nhb = m // HALO
        tiles_per_seq = seqlen // tm
        in_specs = [a_spec,
                    pl.BlockSpec((HALO, D_FF), lambda i: (jnp.maximum(i * hb - 1, 0), 0)),
                    pl.BlockSpec((HALO, D_FF), lambda i: (jnp.minimum((i + 1) * hb, nhb - 1), 0)),
                    v_spec]
        args = [up, up, up, up]
        period = GRID_W
    else:
        tiles_per_seq = 1
        in_specs = [a_spec, v_spec]
        args = [up, up]
        period = seqlen
    row_spec = pl.BlockSpec((tm, D_MODEL), lambda i: (i, 0))
    vec_spec = pl.BlockSpec((1, D_MODEL), lambda i: (0, 0))
    in_specs += [pl.BlockSpec((9, D_FF), lambda i: (0, 0)), pl.BlockSpec((1, D_FF), lambda i: (0, 0)),
                 pl.BlockSpec((None, D_FF, D_MODEL), lambda i: (layer, 0, 0)), row_spec,
                 pl.BlockSpec((None, 1, D_MODEL), lambda i: ((i * tm) // rows_per_mod, 0, 0)),
                 vec_spec, vec_spec]
    args += [conv_w9, conv_b, w3, x, gate, ln_g, ln_b]
    return pl.pallas_call(
        functools.partial(_ffn_down_kernel, tm=tm, period=period, row_taps=is_grid, tiles_per_seq=tiles_per_seq),
        grid=(m // tm,),
        in_specs=in_specs,
        out_specs=row_spec,
        out_shape=jax.ShapeDtypeStruct((m, D_MODEL), F32),
        scratch_shapes=scratch,
        compiler_params=_params("arbitrary"),
        name=name,
    )(*args)


def _group_rows(t, bsz, seqlen, lead):
    n_lead = int(math.prod(lead))
    t = t.reshape(bsz, seqlen, n_lead, 8, -1)
    r = t.shape[-1]
    t = jnp.transpose(t, (0, 3, 2, 4, 1))
    return t.reshape(bsz, 8, n_lead * r, seqlen)


def _group_params(p):
    r = p.shape[1] // 8
    return jnp.transpose(p.reshape(2, 8, r), (1, 0, 2)).reshape(8, 2 * r)


def _run_tokens(x, seqlen, mods, states, want_final, w, tag):
    bsz = x.shape[0]
    m = bsz * seqlen
    nb = mods.shape[1]
    rows_per_mod = m // nb
    xt = x.reshape(m, D_MODEL)
    finals = []
    is_grid = tag == "s"
    for i in range(DEPTH):
        def mod(k):
            return mods[i, :, k].reshape(nb, 1, D_MODEL)
        lw = w[i]
        proj, small = _proj_call(xt, mod(1), mod(0), lw["w_in"], lw["w_in_small"], rows_per_mod,
                                 tm=1024, tn=1536, name=f"inproj{i}{tag}")
        proj3 = proj.reshape(bsz, seqlen, 6144)
        if i % 2 == 0:
            dtt = _group_rows(small[:, :2 * SSD_HEADS], bsz, seqlen, (2,))
            res = _ssd_scan_call(proj3, dtt, lw["conv_w"], lw["conv_b"], lw["prm"], lw["dskip"],
                                 states[i], want_final, name=f"ssdscan{tag}")
            mode, z_blk = "ssd", 0
        else:
            smt = _group_rows(small[:, :4 * GDN_V_HEADS], bsz, seqlen, (2, 2))
            res = _gdn_scan_call(proj3, smt, lw["conv_w"], lw["prm"], states[i], want_final,
                                 name=f"gdnscan{tag}")
            mode, z_blk = "gdn", 2
        y = res[0].reshape(m, 2048)
        if want_final:
            finals.append((res[1], res[2]))
        xt = _out_call(mode, y, proj, z_blk, lw["norm_w"], lw["w_out"], xt, mod(2), lw["ln_g0"], lw["ln_b0"],
                       rows_per_mod, tm=512, name=f"outproj{i}{tag}")
        up = _proj_call(xt, mod(4), mod(3), lw["w_up"], None, rows_per_mod,
                        tm=1024, tn=1408, name=f"ffnup{i}{tag}")[0]
        xt = _ffn_down_call(up, lw["ffn_conv_w"], lw["ffn_conv_b"], lw["w_down"], xt, mod(5),
                            lw["ln_g1"], lw["ln_b1"], seqlen, is_grid, rows_per_mod, tm=512,
                            name=f"ffndown{i}{tag}")
    return xt.reshape(bsz, seqlen, D_MODEL), finals


def kernel(x_prompt, x_sample, state_ssd_fwd, state_ssd_bwd, state_gdn_fwd, state_gdn_bwd, c, c_ctx, w_ada, b_ada, ln_g, ln_b, ssd_w_in, ssd_conv_w, ssd_conv_b, ssd_dt_bias, ssd_a_log, ssd_d, ssd_norm_w, ssd_w_out, gdn_w_in, gdn_conv_w, gdn_dt_bias, gdn_a_log, gdn_norm_w, gdn_w_out, ffn_w_up, ffn_conv_w, ffn_conv_b, ffn_w_down):
    n_dec = c.shape[0]
    cond = jnp.concatenate([c_ctx[None, :], c, jnp.zeros((16 - 1 - n_dec, D_MODEL), F32)], axis=0)
    ada = _ada_call(cond, w_ada, b_ada).reshape(DEPTH, 16, 6, D_MODEL)
    mods_p = ada[:, 0:1]
    mods_s = ada[:, 1:1 + n_dec]

    def small_cols(wm):
        return jnp.pad(wm, ((0, 0), (0, 128 - wm.shape[1]))).astype(BF16)

    ssd_w_in_b, ssd_w_out_b = ssd_w_in.astype(BF16), ssd_w_out.astype(BF16)
    gdn_w_in_b, gdn_w_out_b = gdn_w_in.astype(BF16), gdn_w_out.astype(BF16)
    ffn_w_up_b, ffn_w_down_b = ffn_w_up.astype(BF16), ffn_w_down.astype(BF16)

    weights = []
    for i in range(DEPTH):
        j = i // 2
        lw = {
            "ln_g0": ln_g[i, 0:1], "ln_b0": ln_b[i, 0:1], "ln_g1": ln_g[i, 1:2], "ln_b1": ln_b[i, 1:2],
            "w_up": (ffn_w_up_b, i), "w_down": (ffn_w_down_b, i),
            "ffn_conv_w": ffn_conv_w[i].reshape(9, D_FF), "ffn_conv_b": ffn_conv_b[i].reshape(1, D_FF),
        }
        if i % 2 == 0:
            lw["w_in"] = (ssd_w_in_b, j)
            lw["w_in_small"] = small_cols(ssd_w_in_b[j][:, 6144:])
            lw["conv_w"] = ssd_conv_w[j]
            lw["conv_b"] = ssd_conv_b[j].reshape(1, SSD_XBC)
            lw["prm"] = jnp.stack([_group_params(ssd_dt_bias[j]), _group_params(ssd_a_log[j])], axis=-1)
            lw["dskip"] = jnp.repeat(ssd_d[j], SSD_HEADDIM).reshape(SSD_GROUPS, 1, 256)
            lw["norm_w"] = ssd_norm_w[j].reshape(1, SSD_D_INNER)
            lw["w_out"] = (ssd_w_out_b, j)
        else:
            lw["w_in"] = (gdn_w_in_b, j)
            lw["w_in_small"] = small_cols(gdn_w_in_b[j][:, 6144:])
            lw["conv_w"] = gdn_conv_w[j]
            prm = jnp.stack([_group_params(gdn_dt_bias[j]), _group_params(gdn_a_log[j])], axis=-1)
            lw["prm"] = jnp.pad(prm, ((0, 0), (0, 4), (0, 0)))
            lw["norm_w"] = gdn_norm_w[j].reshape(1, GDN_HEAD)
            lw["w_out"] = (gdn_w_out_b, j)
        weights.append(lw)

    states_s = [(state_ssd_fwd[:, 0], state_ssd_bwd[:, 0]), (state_gdn_fwd[:, 0], state_gdn_bwd[:, 0])]
    y_prompt, fin = _run_tokens(x_prompt, x_prompt.shape[1], mods_p, [None, None], True, weights, "p")
    y_sample, _ = _run_tokens(x_sample, x_sample.shape[1], mods_s, states_s, False, weights, "s")
    return (y_prompt, y_sample,
            fin[0][0][:, None], fin[0][1][:, None], fin[1][0][:, None], fin[1][1][:, None])
```
